```python
import math
import jax, jax.numpy as jnp
from jax import lax
import numpy as np

D_MODEL = 4096
BATCH = 4
SEQ = 4096
DEPTH = 1

HEAD_DIM = 128
D_MIX = D_MODEL
W_A = D_MIX // 2
W_B = D_MIX - W_A
G_A = W_A // HEAD_DIM
H_B = W_B // HEAD_DIM
CHUNK = 128
BLK = 128
DILATED = ((128, 1), (512, 4), (2048, 16))
NUM_BUCKETS = 32
MAX_DISTANCE = 2048
D_PLE = 256
EPS = 1e-6
NEG_INF = -1e30
PROJ_SPLITS = (W_A, W_A, W_A, W_B, W_B, W_B, W_B)
D_IN = W_A * 3 + W_B * 4

kernel_name = "hybrid_gmlp_dilated_attn_layer"


def _rmsnorm(x, g):
    xf = x.astype(jnp.float32)
    y = xf * lax.rsqrt(jnp.mean(xf * xf, axis=-1, keepdims=True) + EPS)
    return (y * g.astype(jnp.float32)).astype(x.dtype)


def _layernorm(x, g, b):
    xf = x.astype(jnp.float32)
    mu = jnp.mean(xf, axis=-1, keepdims=True)
    xc = xf - mu
    y = xc * lax.rsqrt(jnp.mean(xc * xc, axis=-1, keepdims=True) + EPS)
    return (y * g.astype(jnp.float32) + b.astype(jnp.float32)).astype(x.dtype)


def _rel_bucket(dist):
    max_exact = NUM_BUCKETS // 2
    d = jnp.maximum(dist, 1).astype(jnp.float32)
    large = max_exact + (jnp.log(d / max_exact) / math.log(MAX_DISTANCE / max_exact)
                         * (NUM_BUCKETS - max_exact)).astype(jnp.int32)
    large = jnp.minimum(large, NUM_BUCKETS - 1)
    return jnp.where(dist < max_exact, dist, large)


def _dilated_window(q, k, v, rel_bias, window, dil):
    b, s, h, dh = q.shape
    span = dil * BLK
    s_pad = -(-s // span) * span
    L = s_pad // dil
    nb = L // BLK
    pad = ((0, 0), (0, s_pad - s), (0, 0), (0, 0))

    def to_sub(t):
        t = jnp.pad(t, pad).reshape(b, L, dil, h, dh).transpose(0, 2, 1, 3, 4)
        return t.reshape(b, dil, nb, BLK, h, dh)

    def with_prev(t):
        prev = jnp.pad(t, ((0, 0), (0, 0), (1, 0), (0, 0), (0, 0), (0, 0)))[:, :, :-1]
        return jnp.concatenate([prev, t], axis=3)

    qs = to_sub(q)
    kc = with_prev(to_sub(k))
    vc = with_prev(to_sub(v))
    logits = jnp.einsum('brnqhd,brnkhd->brnhqk', qs, kc)

    qi = BLK + jnp.arange(BLK)
    kj = jnp.arange(2 * BLK)
    delta = qi[:, None] - kj[None, :]
    band = (delta >= 0) & (delta <= window // dil)
    bucket = _rel_bucket(jnp.clip(delta, 0, None) * dil)
    bias = rel_bias[bucket].astype(jnp.float32).transpose(2, 0, 1)
    has_prev = (jnp.arange(nb)[:, None, None] > 0) | (kj[None, None, :] >= BLK)
    mask = band[None] & has_prev

    logits = jnp.where(mask[None, None, :, None], logits + bias[None, None, None], NEG_INF)
    m = jnp.max(logits, axis=-1, keepdims=True)
    e = jnp.exp(logits - m)
    den = jnp.sum(e, axis=-1, keepdims=True)
    lse = (m + jnp.log(den))[..., 0].transpose(0, 1, 2, 4, 3)
    o = jnp.einsum('brnhqk,brnkhd->brnqhd', e, vc) / den[..., 0].transpose(0, 1, 2, 4, 3)[..., None]

    def from_sub(t):
        rest = t.shape[4:]
        t = t.reshape((b, dil, L) + rest).swapaxes(1, 2)
        return t.reshape((b, s_pad) + rest)[:, :s]

    return from_sub(o), from_sub(lse)


def setup_inputs(seed: int = 0) -> dict:
    key = jax.random.key(seed)
    ks = jax.random.split(key, 17)
    n = lambda k, shape: jax.random.normal(k, shape, dtype=jnp.float32)
    return {
        "x": n(ks[0], (BATCH, SEQ, D_MODEL)),
        "p": n(ks[1], (DEPTH, BATCH, SEQ, D_PLE)),
        "g_pre": 1.0 + 0.02 * n(ks[2], (DEPTH, D_MODEL)),
        "w_in": n(ks[3], (DEPTH, D_MODEL, D_IN)) * D_MODEL ** -0.5,
        "w_s": n(ks[4], (DEPTH, G_A, CHUNK, CHUNK)) * CHUNK ** -0.5,
        "b_s": 1.0 + 0.1 * n(ks[5], (DEPTH, G_A, CHUNK)),
        "ln_v_g": 1.0 + 0.02 * n(ks[6], (DEPTH, W_A)),
        "ln_v_b": 0.02 * n(ks[7], (DEPTH, W_A)),
        "g_q": 1.0 + 0.02 * n(ks[8], (DEPTH, HEAD_DIM)),
        "g_k": 1.0 + 0.02 * n(ks[9], (DEPTH, HEAD_DIM)),
        "rel_bias": 0.1 * n(ks[10], (NUM_BUCKETS, H_B)),
        "g_out_a": 1.0 + 0.02 * n(ks[11], (DEPTH, W_A)),
        "g_out_b": 1.0 + 0.02 * n(ks[12], (DEPTH, W_B)),
        "w_out": n(ks[13], (DEPTH, D_MIX, D_MODEL)) * D_MIX ** -0.5,
        "g_ple": 1.0 + 0.02 * n(ks[14], (DEPTH, D_MODEL)),
        "w_ple_gate": n(ks[15], (DEPTH, D_MODEL, D_MODEL)) * D_MODEL ** -0.5,
        "w_ple_up": n(ks[16], (DEPTH, D_PLE, D_MODEL)) * D_PLE ** -0.5,
    }


def reference(x, p, g_pre, w_in, w_s, b_s, ln_v_g, ln_v_b, g_q, g_k, rel_bias,
              g_out_a, g_out_b, w_out, g_ple, w_ple_gate, w_ple_up):
    b, s, _ = x.shape
    nc = s // CHUNK
    split_idx = [W_A, 2 * W_A, 3 * W_A, 3 * W_A + W_B, 3 * W_A + 2 * W_B, 3 * W_A + 3 * W_B]
    causal = jnp.tril(jnp.ones((CHUNK, CHUNK), dtype=w_s.dtype))
    for i in range(DEPTH):
        hn = _rmsnorm(x, g_pre[i])
        proj = hn @ w_in[i]
        a_u, a_v, a_z, q, k, v, b_z = jnp.split(proj, split_idx, axis=-1)

        a_u = jax.nn.gelu(a_u, approximate=False)
        a_v = _layernorm(jax.nn.gelu(a_v, approximate=False), ln_v_g[i], ln_v_b[i])
        vg = a_v.reshape(b, nc, CHUNK, G_A, HEAD_DIM)
        z = jnp.einsum('gts,bnsgc->bntgc', w_s[i] * causal, vg) \
            + b_s[i].T[None, None, :, :, None]
        y_a = a_u * z.reshape(b, s, W_A)

        qh = _rmsnorm(q.reshape(b, s, H_B, HEAD_DIM), g_q[i]).astype(jnp.float32) * (HEAD_DIM ** -0.5)
        kh = _rmsnorm(k.reshape(b, s, H_B, HEAD_DIM), g_k[i]).astype(jnp.float32)
        vh = v.reshape(b, s, H_B, HEAD_DIM).astype(jnp.float32)
        outs, lses = [], []
        for window, dil in DILATED:
            o_g, l_g = _dilated_window(qh, kh, vh, rel_bias, window, dil)
            outs.append(o_g)
            lses.append(l_g)
        alpha = jax.nn.softmax(jnp.stack(lses, axis=0), axis=0)
        y_b = jnp.sum(alpha[..., None] * jnp.stack(outs, axis=0), axis=0)
        y_b = y_b.reshape(b, s, W_B).astype(x.dtype)

        y = jnp.concatenate([_rmsnorm(y_a, g_out_a[i]) * jax.nn.silu(a_z),
                             _rmsnorm(y_b, g_out_b[i]) * jax.nn.silu(b_z)], axis=-1)
        h = x + y @ w_out[i]

        gate = jax.nn.sigmoid(_rmsnorm(h, g_ple[i]) @ w_ple_gate[i])
        x = h + gate * (p[i] @ w_ple_up[i])
    return x
```

```python
import functools
import math

import numpy as np
import jax
import jax.numpy as jnp
from jax import lax
from jax.experimental import pallas as pl
from jax.experimental.pallas import tpu as pltpu

HEAD_DIM = 128
CHUNK = 128
BLK = 128
DILATED = ((128, 1), (512, 4), (2048, 16))
NUM_BUCKETS = 32
MAX_DISTANCE = 2048
EPS = 1e-6
NEG_INF = -1e30
N_SEG = 7

V7X_VMEM_LIMIT_BYTES = 56 * 1024 * 1024

BF16 = jnp.bfloat16
F32 = jnp.float32


def _pick(n, pref):
    t = min(n, pref)
    while n % t:
        t //= 2
    return t


def _params(sem):
    return pltpu.CompilerParams(dimension_semantics=sem,
                                vmem_limit_bytes=V7X_VMEM_LIMIT_BYTES)


def _gelu(v):
    return 0.5 * v * (1.0 + lax.erf(v * (1.0 / math.sqrt(2.0))))


def _silu(v):
    return v * jax.nn.sigmoid(v)


def _normalise_rows(x_ref, g_ref, hn_ref, rows):
    tm = x_ref.shape[0]

    def body(c, carry):
        r = pl.ds(pl.multiple_of(c * rows, rows), rows)
        xc = x_ref[r, :]
        ms = jnp.mean(xc * xc, axis=-1, keepdims=True)
        hn_ref[r, :] = (xc * lax.rsqrt(ms + EPS) * g_ref[...]).astype(BF16)
        return carry

    lax.fori_loop(0, tm // rows, body, 0)


def _in_proj_kernel(x_ref, g_ref, w_ref, gq_ref, gk_ref, o_ref, hn_ref, *, tiles_per_seg):
    j = pl.program_id(1)

    @pl.when(j == 0)
    def _():
        _normalise_rows(x_ref, g_ref, hn_ref, 16)

    acc = jnp.dot(hn_ref[...], w_ref[...], preferred_element_type=F32)
    seg = j // tiles_per_seg
    tn = acc.shape[1]

    @pl.when(seg <= 1)
    def _():
        o_ref[...] = _gelu(acc).astype(BF16)

    @pl.when((seg == 2) | (seg == 6))
    def _():
        o_ref[...] = _silu(acc).astype(BF16)

    def head_norm(g, scale):
        for c in range(tn // HEAD_DIM):
            sl = slice(c * HEAD_DIM, (c + 1) * HEAD_DIM)
            blk = acc[:, sl]
            ms = jnp.mean(blk * blk, axis=-1, keepdims=True)
            y = blk * lax.rsqrt(ms + EPS) * g
            if scale is not None:
                y = y * scale
            o_ref[:, sl] = y.astype(BF16)

    @pl.when(seg == 3)
    def _():
        head_norm(gq_ref[...], HEAD_DIM ** -0.5)

    @pl.when(seg == 4)
    def _():
        head_norm(gk_ref[...], None)

    @pl.when(seg == 5)
    def _():
        o_ref[...] = acc.astype(BF16)


def _in_proj(x2, g_pre, w_in_bf, g_q, g_k):
    m, d = x2.shape
    d_in = w_in_bf.shape[1]
    seg_w = d_in // N_SEG
    tm = _pick(m, 512)
    tn = _pick(seg_w, 1024)
    kern = functools.partial(_in_proj_kernel, tiles_per_seg=seg_w // tn)
    return pl.pallas_call(
        kern,
        grid=(m // tm, d_in // tn),
        in_specs=[
            pl.BlockSpec((tm, d), lambda i, j: (i, 0)),
            pl.BlockSpec((1, d), lambda i, j: (0, 0)),
            pl.BlockSpec((d, tn), lambda i, j: (0, j)),
            pl.BlockSpec((1, HEAD_DIM), lambda i, j: (0, 0)),
            pl.BlockSpec((1, HEAD_DIM), lambda i, j: (0, 0)),
        ],
        out_specs=pl.BlockSpec((tm, tn), lambda i, j: (i, j)),
        out_shape=jax.ShapeDtypeStruct((m, d_in), BF16),
        scratch_shapes=[pltpu.VMEM((tm, d), BF16)],
        compiler_params=_params(("parallel", "arbitrary")),
        name="in_proj",
    )(x2, g_pre, w_in_bf, g_q, g_k)


def _gmlp_kernel(au_ref, av_ref, az_ref, ws_ref, bs_ref, lg_ref, lb_ref, go_ref, o_ref, ya_ref):
    av = av_ref[...].astype(F32)
    mu = jnp.mean(av, axis=-1, keepdims=True)
    xc = av - mu
    var = jnp.mean(xc * xc, axis=-1, keepdims=True)
    avn = xc * lax.rsqrt(var + EPS) * lg_ref[...] + lb_ref[...]

    row = lax.broadcasted_iota(jnp.int32, (CHUNK, CHUNK), 0)
    col = lax.broadcasted_iota(jnp.int32, (CHUNK, CHUNK), 1)
    causal = col <= row
    n_groups = ws_ref.shape[0]
    for g in range(n_groups):
        sl = slice(g * HEAD_DIM, (g + 1) * HEAD_DIM)
        wm = jnp.where(causal, ws_ref[g], 0.0).astype(BF16)
        z = jnp.dot(wm, avn[:, sl].astype(BF16), preferred_element_type=F32)
        z = z + bs_ref[:, g:g + 1]
        ya_ref[:, sl] = au_ref[:, sl].astype(F32) * z
    ya = ya_ref[...]
    ms = jnp.mean(ya * ya, axis=-1, keepdims=True)
    o_ref[...] = (ya * lax.rsqrt(ms + EPS) * go_ref[...] * az_ref[...].astype(F32)).astype(BF16)


def _gmlp(proj, w_s, b_s_t, ln_g, ln_b, g_out_a):
    m, d_in = proj.shape
    w_a = d_in // N_SEG
    n_groups = w_s.shape[0]
    vec = lambda: pl.BlockSpec((1, w_a), lambda i: (0, 0))
    return pl.pallas_call(
        _gmlp_kernel,
        grid=(m // CHUNK,),
        in_specs=[
            pl.BlockSpec((CHUNK, w_a), lambda i: (i, 0)),
            pl.BlockSpec((CHUNK, w_a), lambda i: (i, 1)),
            pl.BlockSpec((CHUNK, w_a), lambda i: (i, 2)),
            pl.BlockSpec((n_groups, CHUNK, CHUNK), lambda i: (0, 0, 0)),
            pl.BlockSpec((CHUNK, n_groups), lambda i: (0, 0)),
            vec(), vec(), vec(),
        ],
        out_specs=pl.BlockSpec((CHUNK, w_a), lambda i: (i, 0)),
        out_shape=jax.ShapeDtypeStruct((m, w_a), BF16),
        scratch_shapes=[pltpu.VMEM((CHUNK, w_a), F32)],
        compiler_params=_params(("parallel",)),
        name="gmlp",
    )(proj, proj, proj, w_s, b_s_t, ln_g, ln_b, g_out_a)


def _rel_bucket_np(dist):
    max_exact = NUM_BUCKETS // 2
    d = np.maximum(dist, 1).astype(np.float32)
    large = max_exact + (np.log(d / np.float32(max_exact)) / np.float32(math.log(MAX_DISTANCE / max_exact))
                         * np.float32(NUM_BUCKETS - max_exact)).astype(np.int32)
    large = np.minimum(large, NUM_BUCKETS - 1)
    return np.where(dist < max_exact, dist, large).astype(np.int32)


def _bucket_tables(dil):
    qi = np.arange(BLK)[:, None]
    kj = np.arange(BLK)[None, :]
    d_prev = BLK + qi - kj
    d_cur = qi - kj
    out = []
    for delta in (d_prev, d_cur):
        band = (delta >= 0) & (delta <= BLK)
        b = _rel_bucket_np(np.clip(delta, 0, None) * dil)
        out.append(np.where(band, b, -1))
    return np.stack(out).astype(np.int32)


def _attn_kernel(*refs, n_heads, first, last):
    it = iter(refs)
    bucket_ref, relb_ref = next(it), next(it)
    q_ref, kp_ref, kc_ref, vp_ref, vc_ref = (next(it) for _ in range(5))
    if not first:
        o_in_ref, lse_in_ref = next(it), next(it)
    if last:
        bz_ref, go_ref = next(it), next(it)
        y_ref = next(it)
    else:
        o_out_ref, lse_out_ref = next(it), next(it)
    bias_ref = next(it)
    if last:
        yb_ref = next(it)

    n = pl.program_id(2)
    first_step = (pl.program_id(0) == 0) & (pl.program_id(1) == 0) & (n == 0)

    @pl.when(first_step)
    def _():
        for t in range(2):
            bk = bucket_ref[t]
            for h in range(n_heads):
                tab = jnp.zeros((BLK, BLK), F32)
                for b in range(NUM_BUCKETS):
                    tab = jnp.where(bk == b, relb_ref[b, h], tab)
                bias_ref[t, h] = tab

    mask_prev = (bucket_ref[0] >= 0) & (n > 0)
    mask_cur = bucket_ref[1] >= 0
    lane = lax.broadcasted_iota(jnp.int32, (BLK, BLK), 1)
    nt = (((1,), (1,)), ((), ()))
    lse_tile = jnp.zeros((BLK, BLK), F32)
    for h in range(n_heads):
        sl = slice(h * HEAD_DIM, (h + 1) * HEAD_DIM)
        q = q_ref[0, :, sl]
        s_p = lax.dot_general(q, kp_ref[0, :, sl], nt, preferred_element_type=F32)
        s_c = lax.dot_general(q, kc_ref[0, :, sl], nt, preferred_element_type=F32)
        s_p = jnp.where(mask_prev, s_p + bias_ref[0, h], NEG_INF)
        s_c = jnp.where(mask_cur, s_c + bias_ref[1, h], NEG_INF)
        mx = jnp.maximum(jnp.max(s_p, axis=-1, keepdims=True), jnp.max(s_c, axis=-1, keepdims=True))
        e_p = jnp.exp(s_p - mx)
        e_c = jnp.exp(s_c - mx)
        den = jnp.sum(e_p, axis=-1, keepdims=True) + jnp.sum(e_c, axis=-1, keepdims=True)
        o = (jnp.dot(e_p.astype(BF16), vp_ref[0, :, sl], preferred_element_type=F32)
             + jnp.dot(e_c.astype(BF16), vc_ref[0, :, sl], preferred_element_type=F32)) / den
        lse = mx + jnp.log(den)
        if not first:
            lse_old = lse_in_ref[0, :, h:h + 1]
            top = jnp.maximum(lse_old, lse)
            lse_new = top + jnp.log(jnp.exp(lse_old - top) + jnp.exp(lse - top))
            o = o_in_ref[0, :, sl] * jnp.exp(lse_old - lse_new) + o * jnp.exp(lse - lse_new)
            lse = lse_new
        if last:
            yb_ref[:, sl] = o
        else:
            o_out_ref[0, :, sl] = o
            lse_tile = jnp.where(lane == h, lse, lse_tile)
    if last:
        yb = yb_ref[...]
        ms = jnp.mean(yb * yb, axis=-1, keepdims=True)
        y_ref[0] = (yb * lax.rsqrt(ms + EPS) * go_ref[...] * bz_ref[0].astype(F32)).astype(BF16)
    else:
        lse_out_ref[0] = lse_tile


def _attn_pass(proj3, rel_bias, dil, state, gate_params, *, batch, seq, w_b):
    first = state is None
    last = gate_params is not None
    d_in = proj3.shape[-1]
    n_heads = w_b // HEAD_DIM
    sub = seq // dil
    nb = sub // BLK
    pr = proj3.reshape(batch, sub, dil * d_in)

    def col(seg):
        return lambda b, r, n: (b, n, r * N_SEG + seg)

    def col_prev(seg):
        return lambda b, r, n: (b, jnp.maximum(n - 1, 0), r * N_SEG + seg)

    blk = (1, BLK, w_b)
    in_specs = [
        pl.BlockSpec((2, BLK, BLK), lambda b, r, n: (0, 0, 0)),
        pl.BlockSpec(memory_space=pltpu.SMEM),
        pl.BlockSpec(blk, col(3)),
        pl.BlockSpec(blk, col_prev(4)),
        pl.BlockSpec(blk, col(4)),
        pl.BlockSpec(blk, col_prev(5)),
        pl.BlockSpec(blk, col(5)),
    ]
    args = [jnp.asarray(_bucket_tables(dil)), rel_bias, pr, pr, pr, pr, pr]
    state_spec = pl.BlockSpec(blk, lambda b, r, n: (b, n, r))
    lse_spec = pl.BlockSpec((1, BLK, BLK), lambda b, r, n: (b, n, r))
    if not first:
        o_st, lse_st = state
        in_specs += [state_spec, lse_spec]
        args += [o_st.reshape(batch, sub, dil * w_b), lse_st.reshape(batch, sub, dil * BLK)]
    scratch = [pltpu.VMEM((2, n_heads, BLK, BLK), F32)]
    if last:
        in_specs += [pl.BlockSpec(blk, col(6)), pl.BlockSpec((1, w_b), lambda b, r, n: (0, 0))]
        args += [pr, gate_params]
        out_specs = state_spec
        out_shape = jax.ShapeDtypeStruct((batch, sub, dil * w_b), BF16)
        scratch.append(pltpu.VMEM((BLK, w_b), F32))
    else:
        out_specs = [state_spec, lse_spec]
        out_shape = [jax.ShapeDtypeStruct((batch, sub, dil * w_b), F32),
                     jax.ShapeDtypeStruct((batch, sub, dil * BLK), F32)]
    kern = functools.partial(_attn_kernel, n_heads=n_heads, first=first, last=last)
    out = pl.pallas_call(
        kern,
        grid=(batch, dil, nb),
        in_specs=in_specs,
        out_specs=out_specs,
        out_shape=out_shape,
        scratch_shapes=scratch,
        compiler_params=_params(("arbitrary", "arbitrary", "arbitrary")),
        name=f"attn_d{dil}",
    )(*args)
    if last:
        return out.reshape(batch * seq, w_b)
    o_st, lse_st = out
    return o_st.reshape(batch, seq, w_b), lse_st.reshape(batch, seq, BLK)


def _out_proj_kernel(x_ref, ya_ref, yb_ref, wa_ref, wb_ref, h_ref):
    acc = jnp.dot(ya_ref[...], wa_ref[...], preferred_element_type=F32)
    acc = acc + jnp.dot(yb_ref[...], wb_ref[...], preferred_element_type=F32)
    h_ref[...] = x_ref[...] + acc


def _out_proj(x2, y_a, y_b, w_out_bf):
    m, d = x2.shape
    w_a = y_a.shape[1]
    w_b = y_b.shape[1]
    assert w_a == w_b
    tm = _pick(m, 512)
    tn = _pick(d, 1024)
    return pl.pallas_call(
        _out_proj_kernel,
        grid=(m // tm, d // tn),
        in_specs=[
            pl.BlockSpec((tm, tn), lambda i, j: (i, j)),
            pl.BlockSpec((tm, w_a), lambda i, j: (i, 0)),
            pl.BlockSpec((tm, w_b), lambda i, j: (i, 0)),
            pl.BlockSpec((w_a, tn), lambda i, j: (0, j)),
            pl.BlockSpec((w_b, tn), lambda i, j: (1, j)),
        ],
        out_specs=pl.BlockSpec((tm, tn), lambda i, j: (i, j)),
        out_shape=jax.ShapeDtypeStruct((m, d), F32),
        compiler_params=_params(("parallel", "arbitrary")),
        name="out_proj",
    )(x2, y_a, y_b, w_out_bf, w_out_bf)


def _ple_kernel(h_ref, g_ref, wg_ref, p_ref, wu_ref, o_ref, hn_ref):
    j = pl.program_id(1)

    @pl.when(j == 0)
    def _():
        _normalise_rows(h_ref, g_ref, hn_ref, 16)

    tn = o_ref.shape[1]
    gate = jax.nn.sigmoid(jnp.dot(hn_ref[...], wg_ref[...], preferred_element_type=F32))
    up = jnp.dot(p_ref[...].astype(BF16), wu_ref[...], preferred_element_type=F32)
    cols = pl.ds(pl.multiple_of(j * tn, tn), tn)
    o_ref[...] = h_ref[:, cols] + gate * up


def _ple(h, g_ple, w_gate_bf, p2, w_up_bf):
    m, d = h.shape
    d_ple = p2.shape[1]
    tm = _pick(m, 512)
    tn = _pick(d, 1024)
    return pl.pallas_call(
        _ple_kernel,
        grid=(m // tm, d // tn),
        in_specs=[
            pl.BlockSpec((tm, d), lambda i, j: (i, 0)),
            pl.BlockSpec((1, d), lambda i, j: (0, 0)),
            pl.BlockSpec((d, tn), lambda i, j: (0, j)),
            pl.BlockSpec((tm, d_ple), lambda i, j: (i, 0)),
            pl.BlockSpec((d_ple, tn), lambda i, j: (0, j)),
        ],
        out_specs=pl.BlockSpec((tm, tn), lambda i, j: (i, j)),
        out_shape=jax.ShapeDtypeStruct((m, d), F32),
        scratch_shapes=[pltpu.VMEM((tm, d), BF16)],
        compiler_params=_params(("parallel", "arbitrary")),
        name="ple",
    )(h, g_ple, w_gate_bf, p2, w_up_bf)


def kernel(x, p, g_pre, w_in, w_s, b_s, ln_v_g, ln_v_b, g_q, g_k, rel_bias, g_out_a, g_out_b, w_out, g_ple, w_ple_gate, w_ple_up):
    batch, seq, d = x.shape
    depth = p.shape[0]
    w_a = ln_v_g.shape[-1]
    w_b = g_out_b.shape[-1]
    d_in = w_in.shape[-1]
    assert w_a == w_b and d_in == N_SEG * w_a, "segments of the combined projection must be equally wide"
    assert seq % (DILATED[-1][1] * BLK) == 0 and all(win // dil == BLK for win, dil in DILATED)
    m = batch * seq
    x2 = x.reshape(m, d)
    for i in range(depth):
        proj = _in_proj(x2, g_pre[i][None], w_in[i].astype(BF16), g_q[i][None], g_k[i][None])
        y_a = _gmlp(proj, w_s[i], b_s[i].T, ln_v_g[i][None], ln_v_b[i][None], g_out_a[i][None])
        proj3 = proj.reshape(batch, seq, d_in)
        state = None
        for gi, (_, dil) in enumerate(DILATED):
            gate = g_out_b[i][None] if gi == len(DILATED) - 1 else None
            state = _attn_pass(proj3, rel_bias, dil, state, gate, batch=batch, seq=seq, w_b=w_b)
        y_b = state
        h = _out_proj(x2, y_a, y_b, w_out[i].astype(BF16))
        x2 = _ple(h, g_ple[i][None], w_ple_gate[i].astype(BF16), p[i].reshape(m, -1), w_ple_up[i].astype(BF16))
    return x2.reshape(batch, seq, d)
```

```python
import functools
import math

import numpy as np
import jax
import jax.numpy as jnp
from jax import lax
from jax.experimental import pallas as pl
from jax.experimental.pallas import tpu as pltpu

HEAD_DIM = 128
CHUNK = 128
BLK = 128
DILATED = ((128, 1), (512, 4), (2048, 16))
MAX_DIL = 16
NUM_BUCKETS = 32
MAX_DISTANCE = 2048
EPS = 1e-6
NEG_INF = -1e30
N_SEG = 7
SEG_Q, SEG_K, SEG_V, SEG_BZ = 3, 4, 5, 6

V7X_VMEM_LIMIT_BYTES = 56 * 1024 * 1024

BF16 = jnp.bfloat16
F32 = jnp.float32


def _pick(n, pref):
    t = min(n, pref)
    while n % t:
        t //= 2
    return t


def _params(sem):
    return pltpu.CompilerParams(dimension_semantics=sem,
                                vmem_limit_bytes=V7X_VMEM_LIMIT_BYTES)


def _gelu(v):
    return 0.5 * v * (1.0 + lax.erf(v * (1.0 / math.sqrt(2.0))))


def _silu(v):
    return v * jax.nn.sigmoid(v)


def _normalise_rows(x_ref, g_ref, hn_ref, rows):
    tm = x_ref.shape[0]

    def body(c, carry):
        r = pl.ds(pl.multiple_of(c * rows, rows), rows)
        xc = x_ref[r, :]
        ms = jnp.mean(xc * xc, axis=-1, keepdims=True)
        hn_ref[r, :] = (xc * lax.rsqrt(ms + EPS) * g_ref[...]).astype(BF16)
        return carry

    lax.fori_loop(0, tm // rows, body, 0)


def _in_proj_kernel(x_ref, g_ref, w_ref, gq_ref, gk_ref, perm_ref, o_ref, op_ref, hn_ref, *, tiles_per_seg):
    j = pl.program_id(1)

    @pl.when(j == 0)
    def _():
        _normalise_rows(x_ref, g_ref, hn_ref, 16)

    acc = jnp.dot(hn_ref[...], w_ref[...], preferred_element_type=F32)
    seg = j // tiles_per_seg
    tm, tn = acc.shape

    @pl.when(seg <= 1)
    def _():
        o_ref[...] = _gelu(acc).astype(BF16)

    @pl.when((seg == 2) | (seg == SEG_BZ))
    def _():
        o_ref[...] = _silu(acc).astype(BF16)

    def head_norm(g, scale):
        for c in range(tn // HEAD_DIM):
            sl = slice(c * HEAD_DIM, (c + 1) * HEAD_DIM)
            blk = acc[:, sl]
            ms = jnp.mean(blk * blk, axis=-1, keepdims=True)
            y = blk * lax.rsqrt(ms + EPS) * g
            if scale is not None:
                y = y * scale
            o_ref[:, sl] = y.astype(BF16)

    @pl.when(seg == SEG_Q)
    def _():
        head_norm(gq_ref[...], HEAD_DIM ** -0.5)

    @pl.when(seg == SEG_K)
    def _():
        head_norm(gk_ref[...], None)

    @pl.when(seg == SEG_V)
    def _():
        o_ref[...] = acc.astype(BF16)

    @pl.when((seg >= SEG_Q) & (seg <= SEG_V))
    def _():
        moved = jnp.dot(perm_ref[...], o_ref[...], preferred_element_type=F32).astype(BF16)
        op_ref[0] = moved.reshape(MAX_DIL, tm // MAX_DIL, tn)


def _residue_major_perm(tm):
    runs = tm // MAX_DIL
    rho = np.arange(tm)
    src = MAX_DIL * (rho % runs) + rho // runs
    p = np.zeros((tm, tm), np.float32)
    p[rho, src] = 1.0
    return jnp.asarray(p, BF16)


def _in_proj(x2, g_pre, w_in_bf, g_q, g_k, *, batch, seq):
    m, d = x2.shape
    d_in = w_in_bf.shape[1]
    seg_w = d_in // N_SEG
    tm = _pick(seq, 512)
    tn = _pick(seg_w, 1024)
    tps = seg_w // tn
    assert tm % (MAX_DIL * 16) == 0, "residue-major runs must cover whole bf16 sublane tiles"
    blocks_per_seq = seq // tm
    sub = seq // MAX_DIL
    runs = tm // MAX_DIL
    kern = functools.partial(_in_proj_kernel, tiles_per_seg=tps)

    def perm_idx(i, j):
        jj = jnp.clip(j - SEG_Q * tps, 0, 3 * tps - 1)
        return (i // blocks_per_seq, 0, i % blocks_per_seq, jj)

    return pl.pallas_call(
        kern,
        grid=(m // tm, d_in // tn),
        in_specs=[
            pl.BlockSpec((tm, d), lambda i, j: (i, 0)),
            pl.BlockSpec((1, d), lambda i, j: (0, 0)),
            pl.BlockSpec((d, tn), lambda i, j: (0, j)),
            pl.BlockSpec((1, HEAD_DIM), lambda i, j: (0, 0)),
            pl.BlockSpec((1, HEAD_DIM), lambda i, j: (0, 0)),
            pl.BlockSpec((tm, tm), lambda i, j: (0, 0)),
        ],
        out_specs=[
            pl.BlockSpec((tm, tn), lambda i, j: (i, j)),
            pl.BlockSpec((1, MAX_DIL, runs, tn), perm_idx),
        ],
        out_shape=[
            jax.ShapeDtypeStruct((m, d_in), BF16),
            jax.ShapeDtypeStruct((batch, MAX_DIL, sub, 3 * seg_w), BF16),
        ],
        scratch_shapes=[pltpu.VMEM((tm, d), BF16)],
        compiler_params=_params(("arbitrary", "arbitrary")),
        name="in_proj",
    )(x2, g_pre, w_in_bf, g_q, g_k, _residue_major_perm(tm))


def _gmlp_kernel(au_ref, av_ref, az_ref, ws_ref, bs_ref, lg_ref, lb_ref, go_ref, o_ref, ya_ref):
    av = av_ref[...].astype(F32)
    mu = jnp.mean(av, axis=-1, keepdims=True)
    xc = av - mu
    var = jnp.mean(xc * xc, axis=-1, keepdims=True)
    avn = xc * lax.rsqrt(var + EPS) * lg_ref[...] + lb_ref[...]

    row = lax.broadcasted_iota(jnp.int32, (CHUNK, CHUNK), 0)
    col = lax.broadcasted_iota(jnp.int32, (CHUNK, CHUNK), 1)
    causal = col <= row
    n_groups = ws_ref.shape[0]
    for g in range(n_groups):
        sl = slice(g * HEAD_DIM, (g + 1) * HEAD_DIM)
        wm = jnp.where(causal, ws_ref[g], 0.0).astype(BF16)
        z = jnp.dot(wm, avn[:, sl].astype(BF16), preferred_element_type=F32)
        z = z + bs_ref[:, g:g + 1]
        ya_ref[:, sl] = au_ref[:, sl].astype(F32) * z
    ya = ya_ref[...]
    ms = jnp.mean(ya * ya, axis=-1, keepdims=True)
    o_ref[...] = (ya * lax.rsqrt(ms + EPS) * go_ref[...] * az_ref[...].astype(F32)).astype(BF16)


def _gmlp(proj, w_s, b_s_t, ln_g, ln_b, g_out_a):
    m, d_in = proj.shape
    w_a = d_in // N_SEG
    n_groups = w_s.shape[0]
    vec = lambda: pl.BlockSpec((1, w_a), lambda i: (0, 0))
    return pl.pallas_call(
        _gmlp_kernel,
        grid=(m // CHUNK,),
        in_specs=[
            pl.BlockSpec((CHUNK, w_a), lambda i: (i, 0)),
            pl.BlockSpec((CHUNK, w_a), lambda i: (i, 1)),
            pl.BlockSpec((CHUNK, w_a), lambda i: (i, 2)),
            pl.BlockSpec((n_groups, CHUNK, CHUNK), lambda i: (0, 0, 0)),
            pl.BlockSpec((CHUNK, n_groups), lambda i: (0, 0)),
            vec(), vec(), vec(),
        ],
        out_specs=pl.BlockSpec((CHUNK, w_a), lambda i: (i, 0)),
        out_shape=jax.ShapeDtypeStruct((m, w_a), BF16),
        scratch_shapes=[pltpu.VMEM((CHUNK, w_a), F32)],
        compiler_params=_params(("parallel",)),
        name="gmlp",
    )(proj, proj, proj, w_s, b_s_t, ln_g, ln_b, g_out_a)


def _rel_bucket_np(dist):
    max_exact = NUM_BUCKETS // 2
    d = np.maximum(dist, 1).astype(np.float32)
    large = max_exact + (np.log(d / np.float32(max_exact)) / np.float32(math.log(MAX_DISTANCE / max_exact))
                         * np.float32(NUM_BUCKETS - max_exact)).astype(np.int32)
    large = np.minimum(large, NUM_BUCKETS - 1)
    return np.where(dist < max_exact, dist, large).astype(np.int32)


def _band_tables(dil, pos):
    i_q = pos[:, None]
    i_k = pos[None, :]
    in_cur = i_k <= i_q
    delta = np.where(in_cur, i_q - i_k, BLK + i_q - i_k)
    return in_cur.astype(np.int32), _rel_bucket_np(delta * dil)


class _TileIO:
    def __init__(self, idx, rows_shape):
        self.idx = idx
        self.rows_shape = rows_shape

    def load(self, ref, sl):
        return ref[self.idx + (sl,)].reshape(BLK, sl.stop - sl.start)

    def store(self, ref, sl, val):
        ref[self.idx + (sl,)] = val.reshape(self.rows_shape + (sl.stop - sl.start,))


def _attn_kernel(*refs, n_heads, first, last, diag_bucket, io, io_state):
    it = iter(refs)
    cur_ref, bucket_ref, relb_ref = next(it), next(it), next(it)
    q_ref, kp_ref, kc_ref, vp_ref, vc_ref = (next(it) for _ in range(5))
    if not first:
        acc_in_ref, ml_in_ref = next(it), next(it)
    if last:
        bz_ref, go_ref = next(it), next(it)
        y_ref = next(it)
    else:
        acc_out_ref, ml_out_ref = next(it), next(it)
    bias_ref = next(it)
    if last:
        yb_ref, st_ref = next(it), next(it)

    n = pl.program_id(2)
    first_step = (pl.program_id(0) == 0) & (pl.program_id(1) == 0) & (n == 0)

    @pl.when(first_step)
    def _():
        bk = bucket_ref[...]
        for h in range(n_heads):
            tab = jnp.zeros((BLK, BLK), F32)
            for b in range(NUM_BUCKETS):
                tab = jnp.where(bk == b, relb_ref[b, h], tab)
            bias_ref[h] = tab

    def natural_rows(tile):
        st_ref[...] = tile
        per = BLK // MAX_DIL
        rows = [st_ref[pl.ds((MAX_DIL // 2) * per * (v % 2) + v // 2, 8, stride=per), :]
                for v in range(BLK // 8)]
        return jnp.concatenate(rows, axis=0)

    has_prev = n > 0
    in_cur = cur_ref[...] != 0
    row = lax.broadcasted_iota(jnp.int32, (BLK, BLK), 0)
    lane = lax.broadcasted_iota(jnp.int32, (BLK, BLK), 1)
    eye = row == lane
    valid = in_cur | has_prev
    nt = (((1,), (1,)), ((), ()))
    if not first:
        ml_in = io_state.load(ml_in_ref, slice(0, BLK))
        if last:
            ml_in = natural_rows(ml_in)
    ml_tile = jnp.zeros((BLK, BLK), F32)
    for h in range(n_heads):
        sl = slice(h * HEAD_DIM, (h + 1) * HEAD_DIM)
        q = io.load(q_ref, sl)
        s_p = lax.dot_general(q, io.load(kp_ref, sl), nt, preferred_element_type=F32)
        s_c = lax.dot_general(q, io.load(kc_ref, sl), nt, preferred_element_type=F32)
        s = jnp.where(valid, jnp.where(in_cur, s_c, s_p) + bias_ref[h], NEG_INF)
        dg = jnp.sum(jnp.where(eye, s_p, 0.0), axis=-1, keepdims=True) + relb_ref[diag_bucket, h]
        dg = jnp.where(has_prev, dg, NEG_INF)
        mx = jnp.maximum(jnp.max(s, axis=-1, keepdims=True), dg)
        if not first:
            m_old = ml_in[:, h:h + 1]
            l_old = ml_in[:, n_heads + h:n_heads + h + 1]
            mx = jnp.maximum(mx, m_old)
        e = jnp.exp(s - mx)
        e_d = jnp.exp(dg - mx)
        den = jnp.sum(e, axis=-1, keepdims=True) + e_d
        e_c = jnp.where(in_cur, e, 0.0).astype(BF16)
        e_p = jnp.where(eye, e_d, jnp.where(in_cur, 0.0, e)).astype(BF16)
        o = (jnp.dot(e_c, io.load(vc_ref, sl), preferred_element_type=F32)
             + jnp.dot(e_p, io.load(vp_ref, sl), preferred_element_type=F32))
        if not first:
            a = jnp.exp(m_old - mx)
            acc_old = io_state.load(acc_in_ref, sl)
            if last:
                acc_old = natural_rows(acc_old)
            o = acc_old * a + o
            den = l_old * a + den
        if last:
            yb_ref[:, sl] = o / den
        else:
            io_state.store(acc_out_ref, sl, o)
            ml_tile = jnp.where(lane == h, mx, jnp.where(lane == n_heads + h, den, ml_tile))
    if last:
        yb = yb_ref[...]
        ms = jnp.mean(yb * yb, axis=-1, keepdims=True)
        y_ref[...] = (yb * lax.rsqrt(ms + EPS) * go_ref[...] * bz_ref[...].astype(F32)).astype(BF16)
    else:
        io_state.store(ml_out_ref, slice(0, BLK), ml_tile)


def _attn_pass(dil, qkv_rm, proj, rel_bias, state, g_out_b, *, batch, seq, w_b):
    first = state is None
    last = dil == 1
    n_heads = w_b // HEAD_DIM
    assert 2 * n_heads <= BLK
    sub = seq // MAX_DIL
    nb = seq // dil // BLK
    rep = MAX_DIL // dil
    runs = BLK // rep
    col_q, col_k, col_v = 0, 1, 2

    if last:
        pos = np.arange(BLK)
        io = _TileIO((slice(None),), (BLK,))
        io_state = _TileIO((0, slice(None), slice(None)), (MAX_DIL, BLK // MAX_DIL))
        blk = (BLK, w_b)
        src = proj
        cur = lambda seg: (lambda b, r, n: (b * nb + n, seg))
        prev = lambda seg: (lambda b, r, n: (b * nb + jnp.maximum(n - 1, 0), seg))
        col_q, col_k, col_v = SEG_Q, SEG_K, SEG_V
        st_blk = lambda w: (1, MAX_DIL, BLK // MAX_DIL, w)
        st_idx = lambda b, r, n: (b, 0, n, 0)
        view = lambda a: a
    else:
        rho = np.arange(BLK)
        pos = rep * (rho % runs) + rho // runs
        io = io_state = _TileIO((0, slice(None), 0, slice(None)), (rep, runs))
        blk = (1, rep, 1, runs, w_b)
        src = qkv_rm.reshape(batch, rep, dil, sub, 3 * w_b)
        cur = lambda col: (lambda b, r, n: (b, 0, r, n, col))
        prev = lambda col: (lambda b, r, n: (b, 0, r, jnp.maximum(n - 1, 0), col))
        st_blk = lambda w: (1, rep, 1, runs, w)
        st_idx = lambda b, r, n: (b, 0, r, n, 0)
        view = lambda a: a.reshape(batch, rep, dil, sub, a.shape[-1])

    in_cur, bucket = _band_tables(dil, pos)
    diag_bucket = int(_rel_bucket_np(np.array([BLK * dil]))[0])
    const = lambda: pl.BlockSpec((BLK, BLK), lambda b, r, n: (0, 0))
    in_specs = [
        const(), const(), pl.BlockSpec(memory_space=pltpu.SMEM),
        pl.BlockSpec(blk, cur(col_q)),
        pl.BlockSpec(blk, prev(col_k)), pl.BlockSpec(blk, cur(col_k)),
        pl.BlockSpec(blk, prev(col_v)), pl.BlockSpec(blk, cur(col_v)),
    ]
    args = [jnp.asarray(in_cur), jnp.asarray(bucket), rel_bias, src, src, src, src, src]
    acc_spec = pl.BlockSpec(st_blk(w_b), st_idx)
    ml_spec = pl.BlockSpec(st_blk(BLK), st_idx)
    if not first:
        acc, ml = state
        in_specs += [acc_spec, ml_spec]
        args += [view(acc), view(ml)]
    scratch = [pltpu.VMEM((n_heads, BLK, BLK), F32)]
    if last:
        in_specs += [pl.BlockSpec(blk, cur(SEG_BZ)), pl.BlockSpec((1, w_b), lambda b, r, n: (0, 0))]
        args += [proj, g_out_b]
        out_specs = pl.BlockSpec(blk, cur(0))
        out_shape = jax.ShapeDtypeStruct((batch * seq, w_b), BF16)
        scratch += [pltpu.VMEM((BLK, w_b), F32), pltpu.VMEM((BLK, HEAD_DIM), F32)]
    else:
        out_specs = [acc_spec, ml_spec]
        out_shape = [jax.ShapeDtypeStruct((batch, rep, dil, sub, w_b), F32),
                     jax.ShapeDtypeStruct((batch, rep, dil, sub, BLK), F32)]
    kern = functools.partial(_attn_kernel, n_heads=n_heads, first=first, last=last,
                             diag_bucket=diag_bucket, io=io, io_state=io_state)
    out = pl.pallas_call(
        kern,
        grid=(batch, dil, nb),
        in_specs=in_specs,
        out_specs=out_specs,
        out_shape=out_shape,
        scratch_shapes=scratch,
        compiler_params=_params(("arbitrary", "arbitrary", "arbitrary")),
        name=f"attn_d{dil}",
    )(*args)
    if last:
        return out
    acc, ml = out
    return (acc.reshape(batch, MAX_DIL, sub, w_b), ml.reshape(batch, MAX_DIL, sub, BLK))


def _out_proj_kernel(x_ref, ya_ref, yb_ref, wa_ref, wb_ref, h_ref):
    acc = jnp.dot(ya_ref[...], wa_ref[...], preferred_element_type=F32)
    acc = acc + jnp.dot(yb_ref[...], wb_ref[...], preferred_element_type=F32)
    h_ref[...] = x_ref[...] + acc


def _out_proj(x2, y_a, y_b, w_out_bf):
    m, d = x2.shape
    w_a = y_a.shape[1]
    w_b = y_b.shape[1]
    assert w_a == w_b
    tm = _pick(m, 512)
    tn = _pick(d, 1024)
    return pl.pallas_call(
        _out_proj_kernel,
        grid=(m // tm, d // tn),
        in_specs=[
            pl.BlockSpec((tm, tn), lambda i, j: (i, j)),
            pl.BlockSpec((tm, w_a), lambda i, j: (i, 0)),
            pl.BlockSpec((tm, w_b), lambda i, j: (i, 0)),
            pl.BlockSpec((w_a, tn), lambda i, j: (0, j)),
            pl.BlockSpec((w_b, tn), lambda i, j: (1, j)),
        ],
        out_specs=pl.BlockSpec((tm, tn), lambda i, j: (i, j)),
        out_shape=jax.ShapeDtypeStruct((m, d), F32),
        compiler_params=_params(("parallel", "arbitrary")),
        name="out_proj",
    )(x2, y_a, y_b, w_out_bf, w_out_bf)


def _ple_kernel(h_ref, g_ref, wg_ref, p_ref, wu_ref, o_ref, hn_ref):
    j = pl.program_id(1)

    @pl.when(j == 0)
    def _():
        _normalise_rows(h_ref, g_ref, hn_ref, 16)

    tn = o_ref.shape[1]
    gate = jax.nn.sigmoid(jnp.dot(hn_ref[...], wg_ref[...], preferred_element_type=F32))
    up = jnp.dot(p_ref[...].astype(BF16), wu_ref[...], preferred_element_type=F32)
    cols = pl.ds(pl.multiple_of(j * tn, tn), tn)
    o_ref[...] = h_ref[:, cols] + gate * up


def _ple(h, g_ple, w_gate_bf, p2, w_up_bf):
    m, d = h.shape
    d_ple = p2.shape[1]
    tm = _pick(m, 512)
    tn = _pick(d, 1024)
    return pl.pallas_call(
        _ple_kernel,
        grid=(m // tm, d // tn),
        in_specs=[
            pl.BlockSpec((tm, d), lambda i, j: (i, 0)),
            pl.BlockSpec((1, d), lambda i, j: (0, 0)),
            pl.BlockSpec((d, tn), lambda i, j: (0, j)),
            pl.BlockSpec((tm, d_ple), lambda i, j: (i, 0)),
            pl.BlockSpec((d_ple, tn), lambda i, j: (0, j)),
        ],
        out_specs=pl.BlockSpec((tm, tn), lambda i, j: (i, j)),
        out_shape=jax.ShapeDtypeStruct((m, d), F32),
        scratch_shapes=[pltpu.VMEM((tm, d), BF16)],
        compiler_params=_params(("parallel", "arbitrary")),
        name="ple",
    )(h, g_ple, w_gate_bf, p2, w_up_bf)


def kernel(x, p, g_pre, w_in, w_s, b_s, ln_v_g, ln_v_b, g_q, g_k, rel_bias, g_out_a, g_out_b, w_out, g_ple, w_ple_gate, w_ple_up):
    batch, seq, d = x.shape
    depth = p.shape[0]
    w_a = ln_v_g.shape[-1]
    w_b = g_out_b.shape[-1]
    d_in = w_in.shape[-1]
    assert w_a == w_b and d_in == N_SEG * w_a, "segments of the combined projection must be equally wide"
    assert seq % (MAX_DIL * BLK) == 0 and all(win // dil == BLK for win, dil in DILATED)
    assert sorted(dil for _, dil in DILATED) == [1, 4, MAX_DIL]
    m = batch * seq
    x2 = x.reshape(m, d)
    for i in range(depth):
        proj, qkv_rm = _in_proj(x2, g_pre[i][None], w_in[i].astype(BF16), g_q[i][None], g_k[i][None],
                                batch=batch, seq=seq)
        y_a = _gmlp(proj, w_s[i], b_s[i].T, ln_v_g[i][None], ln_v_b[i][None], g_out_a[i][None])
        state = None
        for dil in sorted((dil for _, dil in DILATED), reverse=True):
            state = _attn_pass(dil, qkv_rm, proj, rel_bias, state, g_out_b[i][None],
                               batch=batch, seq=seq, w_b=w_b)
        y_b = state
        h = _out_proj(x2, y_a, y_b, w_out[i].astype(BF16))
        x2 = _ple(h, g_ple[i][None], w_ple_gate[i].astype(BF16), p[i].reshape(m, -1), w_ple_up[i].astype(BF16))
    return x2.reshape(batch, seq, d)
```

```python
import functools
import math

import numpy as np
import jax
import jax.numpy as jnp
from jax import lax
from jax.experimental import pallas as pl
from jax.experimental.pallas import tpu as pltpu

HEAD_DIM = 128
CHUNK = 128
BLK = 128
QK_LOOKAHEAD = 3
DILATED = ((128, 1), (512, 4), (2048, 16))
MAX_DIL = 16
NUM_BUCKETS = 32
MAX_DISTANCE = 2048
EPS = 1e-6
NEG_INF = -1e30
N_SEG = 7
SEG_Q, SEG_K, SEG_V, SEG_BZ = 3, 4, 5, 6

V7X_VMEM_LIMIT_BYTES = 56 * 1024 * 1024

BF16 = jnp.bfloat16
F32 = jnp.float32


def _pick(n, pref):
    t = min(n, pref)
    while n % t:
        t //= 2
    return t


def _params(sem):
    return pltpu.CompilerParams(dimension_semantics=sem,
                                vmem_limit_bytes=V7X_VMEM_LIMIT_BYTES)


def _gelu(v):
    return 0.5 * v * (1.0 + lax.erf(v * (1.0 / math.sqrt(2.0))))


def _silu(v):
    return v * jax.nn.sigmoid(v)


def _normalise_rows(x_ref, g_ref, hn_ref, rows):
    tm = x_ref.shape[0]

    def body(c, carry):
        r = pl.ds(pl.multiple_of(c * rows, rows), rows)
        xc = x_ref[r, :]
        ms = jnp.mean(xc * xc, axis=-1, keepdims=True)
        hn_ref[r, :] = (xc * lax.rsqrt(ms + EPS) * g_ref[...]).astype(BF16)
        return carry

    lax.fori_loop(0, tm // rows, body, 0)


def _in_proj_kernel(x_ref, g_ref, w_ref, gq_ref, gk_ref, perm_ref, o_ref, op_ref, hn_ref, *, tiles_per_seg):
    j = pl.program_id(1)

    @pl.when(j == 0)
    def _():
        _normalise_rows(x_ref, g_ref, hn_ref, 16)

    acc = jnp.dot(hn_ref[...], w_ref[...], preferred_element_type=F32)
    seg = j // tiles_per_seg
    tm, tn = acc.shape

    @pl.when(seg <= 1)
    def _():
        o_ref[...] = _gelu(acc).astype(BF16)

    @pl.when((seg == 2) | (seg == SEG_BZ))
    def _():
        o_ref[...] = _silu(acc).astype(BF16)

    def head_norm(g, scale):
        for c in range(tn // HEAD_DIM):
            sl = slice(c * HEAD_DIM, (c + 1) * HEAD_DIM)
            blk = acc[:, sl]
            ms = jnp.mean(blk * blk, axis=-1, keepdims=True)
            y = blk * lax.rsqrt(ms + EPS) * g
            if scale is not None:
                y = y * scale
            o_ref[:, sl] = y.astype(BF16)

    @pl.when(seg == SEG_Q)
    def _():
        head_norm(gq_ref[...], HEAD_DIM ** -0.5)

    @pl.when(seg == SEG_K)
    def _():
        head_norm(gk_ref[...], None)

    @pl.when(seg == SEG_V)
    def _():
        o_ref[...] = acc.astype(BF16)

    @pl.when((seg >= SEG_Q) & (seg <= SEG_V))
    def _():
        moved = jnp.dot(perm_ref[...], o_ref[...], preferred_element_type=F32).astype(BF16)
        op_ref[0] = moved.reshape(MAX_DIL, tm // MAX_DIL, tn)


def _residue_major_perm(tm):
    runs = tm // MAX_DIL
    rho = np.arange(tm)
    src = MAX_DIL * (rho % runs) + rho // runs
    p = np.zeros((tm, tm), np.float32)
    p[rho, src] = 1.0
    return jnp.asarray(p, BF16)


def _in_proj(x2, g_pre, w_in_bf, g_q, g_k, *, batch, seq):
    m, d = x2.shape
    d_in = w_in_bf.shape[1]
    seg_w = d_in // N_SEG
    tm = _pick(seq, 512)
    tn = _pick(seg_w, 1024)
    tps = seg_w // tn
    assert tm % (MAX_DIL * 16) == 0, "residue-major runs must cover whole bf16 sublane tiles"
    blocks_per_seq = seq // tm
    sub = seq // MAX_DIL
    runs = tm // MAX_DIL
    kern = functools.partial(_in_proj_kernel, tiles_per_seg=tps)

    def perm_idx(i, j):
        jj = jnp.clip(j - SEG_Q * tps, 0, 3 * tps - 1)
        return (i // blocks_per_seq, 0, i % blocks_per_seq, jj)

    return pl.pallas_call(
        kern,
        grid=(m // tm, d_in // tn),
        in_specs=[
            pl.BlockSpec((tm, d), lambda i, j: (i, 0)),
            pl.BlockSpec((1, d), lambda i, j: (0, 0)),
            pl.BlockSpec((d, tn), lambda i, j: (0, j)),
            pl.BlockSpec((1, HEAD_DIM), lambda i, j: (0, 0)),
            pl.BlockSpec((1, HEAD_DIM), lambda i, j: (0, 0)),
            pl.BlockSpec((tm, tm), lambda i, j: (0, 0)),
        ],
        out_specs=[
            pl.BlockSpec((tm, tn), lambda i, j: (i, j)),
            pl.BlockSpec((1, MAX_DIL, runs, tn), perm_idx),
        ],
        out_shape=[
            jax.ShapeDtypeStruct((m, d_in), BF16),
            jax.ShapeDtypeStruct((batch, MAX_DIL, sub, 3 * seg_w), BF16),
        ],
        scratch_shapes=[pltpu.VMEM((tm, d), BF16)],
        compiler_params=_params(("arbitrary", "arbitrary")),
        name="in_proj",
    )(x2, g_pre, w_in_bf, g_q, g_k, _residue_major_perm(tm))


def _gmlp_kernel(au_ref, av_ref, az_ref, ws_ref, bs_ref, lg_ref, lb_ref, go_ref, o_ref, ya_ref):
    av = av_ref[...].astype(F32)
    mu = jnp.mean(av, axis=-1, keepdims=True)
    xc = av - mu
    var = jnp.mean(xc * xc, axis=-1, keepdims=True)
    avn = xc * lax.rsqrt(var + EPS) * lg_ref[...] + lb_ref[...]

    row = lax.broadcasted_iota(jnp.int32, (CHUNK, CHUNK), 0)
    col = lax.broadcasted_iota(jnp.int32, (CHUNK, CHUNK), 1)
    causal = col <= row
    n_groups = ws_ref.shape[0]
    for g in range(n_groups):
        sl = slice(g * HEAD_DIM, (g + 1) * HEAD_DIM)
        wm = jnp.where(causal, ws_ref[g], 0.0).astype(BF16)
        z = jnp.dot(wm, avn[:, sl].astype(BF16), preferred_element_type=F32)
        z = z + bs_ref[:, g:g + 1]
        ya_ref[:, sl] = au_ref[:, sl].astype(F32) * z
    ya = ya_ref[...]
    ms = jnp.mean(ya * ya, axis=-1, keepdims=True)
    o_ref[...] = (ya * lax.rsqrt(ms + EPS) * go_ref[...] * az_ref[...].astype(F32)).astype(BF16)


def _gmlp(proj, w_s, b_s_t, ln_g, ln_b, g_out_a):
    m, d_in = proj.shape
    w_a = d_in // N_SEG
    n_groups = w_s.shape[0]
    vec = lambda: pl.BlockSpec((1, w_a), lambda i: (0, 0))
    return pl.pallas_call(
        _gmlp_kernel,
        grid=(m // CHUNK,),
        in_specs=[
            pl.BlockSpec((CHUNK, w_a), lambda i: (i, 0)),
            pl.BlockSpec((CHUNK, w_a), lambda i: (i, 1)),
            pl.BlockSpec((CHUNK, w_a), lambda i: (i, 2)),
            pl.BlockSpec((n_groups, CHUNK, CHUNK), lambda i: (0, 0, 0)),
            pl.BlockSpec((CHUNK, n_groups), lambda i: (0, 0)),
            vec(), vec(), vec(),
        ],
        out_specs=pl.BlockSpec((CHUNK, w_a), lambda i: (i, 0)),
        out_shape=jax.ShapeDtypeStruct((m, w_a), BF16),
        scratch_shapes=[pltpu.VMEM((CHUNK, w_a), F32)],
        compiler_params=_params(("parallel",)),
        name="gmlp",
    )(proj, proj, proj, w_s, b_s_t, ln_g, ln_b, g_out_a)


def _rel_bucket_np(dist):
    max_exact = NUM_BUCKETS // 2
    d = np.maximum(dist, 1).astype(np.float32)
    large = max_exact + (np.log(d / np.float32(max_exact)) / np.float32(math.log(MAX_DISTANCE / max_exact))
                         * np.float32(NUM_BUCKETS - max_exact)).astype(np.int32)
    large = np.minimum(large, NUM_BUCKETS - 1)
    return np.where(dist < max_exact, dist, large).astype(np.int32)


def _band_tables(dil, pos):
    i_q = pos[:, None]
    i_k = pos[None, :]
    in_cur = i_k <= i_q
    delta = np.where(in_cur, i_q - i_k, BLK + i_q - i_k)
    return in_cur.astype(np.int32), _rel_bucket_np(delta * dil)


class _TileIO:
    def __init__(self, idx, rows_shape):
        self.idx = idx
        self.rows_shape = rows_shape

    def load(self, ref, sl):
        return ref[self.idx + (sl,)].reshape(BLK, sl.stop - sl.start)

    def store(self, ref, sl, val):
        ref[self.idx + (sl,)] = val.reshape(self.rows_shape + (sl.stop - sl.start,))


def _attn_kernel(*refs, n_heads, first, last, diag_bucket, io, io_state):
    it = iter(refs)
    cur_ref, bucket_ref, relb_ref = next(it), next(it), next(it)
    q_ref, kp_ref, kc_ref, vp_ref, vc_ref = (next(it) for _ in range(5))
    if not first:
        o_in_ref, lse_in_ref = next(it), next(it)
    if last:
        bz_ref, go_ref = next(it), next(it)
        y_ref = next(it)
    else:
        o_out_ref, lse_out_ref = next(it), next(it)
    bias_ref = next(it)
    if last:
        yb_ref, st_ref = next(it), next(it)

    n = pl.program_id(2)
    first_step = (pl.program_id(0) == 0) & (pl.program_id(1) == 0) & (n == 0)

    @pl.when(first_step)
    def _():
        bk = bucket_ref[...]
        for h in range(n_heads):
            tab = jnp.zeros((BLK, BLK), F32)
            for b in range(NUM_BUCKETS):
                tab = jnp.where(bk == b, relb_ref[b, h], tab)
            bias_ref[h] = tab

    def natural_rows(tile):
        st_ref[...] = tile
        per = BLK // MAX_DIL
        rows = [st_ref[pl.ds((MAX_DIL // 2) * per * (v % 2) + v // 2, 8, stride=per), :]
                for v in range(BLK // 8)]
        return jnp.concatenate(rows, axis=0)

    has_prev = n > 0
    in_cur = cur_ref[...] != 0
    row = lax.broadcasted_iota(jnp.int32, (BLK, BLK), 0)
    lane = lax.broadcasted_iota(jnp.int32, (BLK, BLK), 1)
    eye = row == lane
    valid = in_cur | has_prev
    nt = (((1,), (1,)), ((), ()))
    if not first:
        lse_in = io_state.load(lse_in_ref, slice(0, BLK))
        if last:
            lse_in = natural_rows(lse_in)
    lse_tile = jnp.zeros((BLK, BLK), F32)
    ones = jnp.ones((BLK, HEAD_DIM), BF16)

    def logits(h):
        sl = slice(h * HEAD_DIM, (h + 1) * HEAD_DIM)
        keys = jnp.concatenate([io.load(kp_ref, sl), io.load(kc_ref, sl)], axis=0)
        return lax.dot_general(io.load(q_ref, sl), keys, nt, preferred_element_type=F32)

    ahead = {h: logits(h) for h in range(min(QK_LOOKAHEAD, n_heads))}
    for h in range(n_heads):
        sl = slice(h * HEAD_DIM, (h + 1) * HEAD_DIM)
        if h + QK_LOOKAHEAD < n_heads:
            ahead[h + QK_LOOKAHEAD] = logits(h + QK_LOOKAHEAD)
        s2 = ahead.pop(h)
        s_p, s_c = s2[:, :BLK], s2[:, BLK:]
        s = jnp.where(valid, jnp.where(in_cur, s_c, s_p) + bias_ref[h], NEG_INF)
        dg = jnp.sum(jnp.where(eye, s_p, 0.0), axis=-1, keepdims=True) + relb_ref[diag_bucket, h]
        dg = jnp.where(has_prev, dg, NEG_INF)
        mx = jnp.maximum(jnp.max(s, axis=-1, keepdims=True), dg)
        e = jnp.exp(s - mx)
        e_d = jnp.exp(dg - mx)
        probs = jnp.concatenate([jnp.where(eye, e_d, jnp.where(in_cur, 0.0, e)),
                                 jnp.where(in_cur, e, 0.0)], axis=1).astype(BF16)
        vals = jnp.concatenate([jnp.concatenate([io.load(vp_ref, sl), ones], axis=1),
                                jnp.concatenate([io.load(vc_ref, sl), ones], axis=1)], axis=0)
        both = jnp.dot(probs, vals, preferred_element_type=F32)
        num, den = both[:, :HEAD_DIM], both[:, HEAD_DIM:]
        if first:
            o = num / den
            lse = mx + jnp.log(den)
        else:
            lse_old = lse_in[:, h:h + 1]
            o_old = io_state.load(o_in_ref, sl)
            if last:
                o_old = natural_rows(o_old)
            top = jnp.maximum(lse_old, mx)
            w_old = jnp.exp(lse_old - top)
            w_new = jnp.exp(mx - top)
            total = w_old + den * w_new
            o = (o_old * w_old + num * w_new) / total
            lse = top + jnp.log(total)
        if last:
            yb_ref[:, sl] = o
        else:
            io_state.store(o_out_ref, sl, o)
            lse_tile = jnp.where(lane == h, lse, lse_tile)
    if last:
        yb = yb_ref[...]
        ms = jnp.mean(yb * yb, axis=-1, keepdims=True)
        y_ref[...] = (yb * lax.rsqrt(ms + EPS) * go_ref[...] * bz_ref[...].astype(F32)).astype(BF16)
    else:
        io_state.store(lse_out_ref, slice(0, BLK), lse_tile)


def _attn_pass(dil, qkv_rm, proj, rel_bias, state, g_out_b, *, batch, seq, w_b):
    first = state is None
    last = dil == 1
    n_heads = w_b // HEAD_DIM
    assert 2 * n_heads <= BLK
    sub = seq // MAX_DIL
    nb = seq // dil // BLK
    rep = MAX_DIL // dil
    runs = BLK // rep
    col_q, col_k, col_v = 0, 1, 2

    if last:
        pos = np.arange(BLK)
        io = _TileIO((slice(None),), (BLK,))
        io_state = _TileIO((0, slice(None), slice(None)), (MAX_DIL, BLK // MAX_DIL))
        blk = (BLK, w_b)
        src = proj
        cur = lambda seg: (lambda b, r, n: (b * nb + n, seg))
        prev = lambda seg: (lambda b, r, n: (b * nb + jnp.maximum(n - 1, 0), seg))
        col_q, col_k, col_v = SEG_Q, SEG_K, SEG_V
        st_blk = lambda w: (1, MAX_DIL, BLK // MAX_DIL, w)
        st_idx = lambda b, r, n: (b, 0, n, 0)
        view = lambda a: a
    else:
        rho = np.arange(BLK)
        pos = rep * (rho % runs) + rho // runs
        io = io_state = _TileIO((0, slice(None), 0, slice(None)), (rep, runs))
        blk = (1, rep, 1, runs, w_b)
        src = qkv_rm.reshape(batch, rep, dil, sub, 3 * w_b)
        cur = lambda col: (lambda b, r, n: (b, 0, r, n, col))
        prev = lambda col: (lambda b, r, n: (b, 0, r, jnp.maximum(n - 1, 0), col))
        st_blk = lambda w: (1, rep, 1, runs, w)
        st_idx = lambda b, r, n: (b, 0, r, n, 0)
        view = lambda a: a.reshape(batch, rep, dil, sub, a.shape[-1])

    in_cur, bucket = _band_tables(dil, pos)
    diag_bucket = int(_rel_bucket_np(np.array([BLK * dil]))[0])
    const = lambda: pl.BlockSpec((BLK, BLK), lambda b, r, n: (0, 0))
    in_specs = [
        const(), const(), pl.BlockSpec(memory_space=pltpu.SMEM),
        pl.BlockSpec(blk, cur(col_q)),
        pl.BlockSpec(blk, prev(col_k)), pl.BlockSpec(blk, cur(col_k)),
        pl.BlockSpec(blk, prev(col_v)), pl.BlockSpec(blk, cur(col_v)),
    ]
    args = [jnp.asarray(in_cur), jnp.asarray(bucket), rel_bias, src, src, src, src, src]
    acc_spec = pl.BlockSpec(st_blk(w_b), st_idx)
    ml_spec = pl.BlockSpec(st_blk(BLK), st_idx)
    if not first:
        acc, ml = state
        in_specs += [acc_spec, ml_spec]
        args += [view(acc), view(ml)]
    scratch = [pltpu.VMEM((n_heads, BLK, BLK), F32)]
    if last:
        in_specs += [pl.BlockSpec(blk, cur(SEG_BZ)), pl.BlockSpec((1, w_b), lambda b, r, n: (0, 0))]
        args += [proj, g_out_b]
        out_specs = pl.BlockSpec(blk, cur(0))
        out_shape = jax.ShapeDtypeStruct((batch * seq, w_b), BF16)
        scratch += [pltpu.VMEM((BLK, w_b), F32), pltpu.VMEM((BLK, HEAD_DIM), F32)]
    else:
        out_specs = [acc_spec, ml_spec]
        out_shape = [jax.ShapeDtypeStruct((batch, rep, dil, sub, w_b), F32),
                     jax.ShapeDtypeStruct((batch, rep, dil, sub, BLK), F32)]
    kern = functools.partial(_attn_kernel, n_heads=n_heads, first=first, last=last,
                             diag_bucket=diag_bucket, io=io, io_state=io_state)
    out = pl.pallas_call(
        kern,
        grid=(batch, dil, nb),
        in_specs=in_specs,
        out_specs=out_specs,
        out_shape=out_shape,
        scratch_shapes=scratch,
        compiler_params=_params(("arbitrary", "arbitrary", "arbitrary")),
        name=f"attn_d{dil}",
    )(*args)
    if last:
        return out
    acc, ml = out
    return (acc.reshape(batch, MAX_DIL, sub, w_b), ml.reshape(batch, MAX_DIL, sub, BLK))


def _out_proj_kernel(x_ref, ya_ref, yb_ref, wa_ref, wb_ref, h_ref):
    acc = jnp.dot(ya_ref[...], wa_ref[...], preferred_element_type=F32)
    acc = acc + jnp.dot(yb_ref[...], wb_ref[...], preferred_element_type=F32)
    h_ref[...] = x_ref[...] + acc


def _out_proj(x2, y_a, y_b, w_out_bf):
    m, d = x2.shape
    w_a = y_a.shape[1]
    w_b = y_b.shape[1]
    assert w_a == w_b
    tm = _pick(m, 512)
    tn = _pick(d, 1024)
    return pl.pallas_call(
        _out_proj_kernel,
        grid=(m // tm, d // tn),
        in_specs=[
            pl.BlockSpec((tm, tn), lambda i, j: (i, j)),
            pl.BlockSpec((tm, w_a), lambda i, j: (i, 0)),
            pl.BlockSpec((tm, w_b), lambda i, j: (i, 0)),
            pl.BlockSpec((w_a, tn), lambda i, j: (0, j)),
            pl.BlockSpec((w_b, tn), lambda i, j: (1, j)),
        ],
        out_specs=pl.BlockSpec((tm, tn), lambda i, j: (i, j)),
        out_shape=jax.ShapeDtypeStruct((m, d), F32),
        compiler_params=_params(("parallel", "arbitrary")),
        name="out_proj",
    )(x2, y_a, y_b, w_out_bf, w_out_bf)


def _ple_kernel(h_ref, g_ref, wg_ref, p_ref, wu_ref, o_ref, hn_ref):
    j = pl.program_id(1)

    @pl.when(j == 0)
    def _():
        _normalise_rows(h_ref, g_ref, hn_ref, 16)

    tn = o_ref.shape[1]
    gate = jax.nn.sigmoid(jnp.dot(hn_ref[...], wg_ref[...], preferred_element_type=F32))
    up = jnp.dot(p_ref[...].astype(BF16), wu_ref[...], preferred_element_type=F32)
    cols = pl.ds(pl.multiple_of(j * tn, tn), tn)
    o_ref[...] = h_ref[:, cols] + gate * up


def _ple(h, g_ple, w_gate_bf, p2, w_up_bf):
    m, d = h.shape
    d_ple = p2.shape[1]
    tm = _pick(m, 512)
    tn = _pick(d, 1024)
    return pl.pallas_call(
        _ple_kernel,
        grid=(m // tm, d // tn),
        in_specs=[
            pl.BlockSpec((tm, d), lambda i, j: (i, 0)),
            pl.BlockSpec((1, d), lambda i, j: (0, 0)),
            pl.BlockSpec((d, tn), lambda i, j: (0, j)),
            pl.BlockSpec((tm, d_ple), lambda i, j: (i, 0)),
            pl.BlockSpec((d_ple, tn), lambda i, j: (0, j)),
        ],
        out_specs=pl.BlockSpec((tm, tn), lambda i, j: (i, j)),
        out_shape=jax.ShapeDtypeStruct((m, d), F32),
        scratch_shapes=[pltpu.VMEM((tm, d), BF16)],
        compiler_params=_params(("parallel", "arbitrary")),
        name="ple",
    )(h, g_ple, w_gate_bf, p2, w_up_bf)


def kernel(x, p, g_pre, w_in, w_s, b_s, ln_v_g, ln_v_b, g_q, g_k, rel_bias, g_out_a, g_out_b, w_out, g_ple, w_ple_gate, w_ple_up):
    batch, seq, d = x.shape
    depth = p.shape[0]
    w_a = ln_v_g.shape[-1]
    w_b = g_out_b.shape[-1]
    d_in = w_in.shape[-1]
    assert w_a == w_b and d_in == N_SEG * w_a, "segments of the combined projection must be equally wide"
    assert seq % (MAX_DIL * BLK) == 0 and all(win // dil == BLK for win, dil in DILATED)
    assert sorted(dil for _, dil in DILATED) == [1, 4, MAX_DIL]
    m = batch * seq
    x2 = x.reshape(m, d)
    for i in range(depth):
        proj, qkv_rm = _in_proj(x2, g_pre[i][None], w_in[i].astype(BF16), g_q[i][None], g_k[i][None],
                                batch=batch, seq=seq)
        y_a = _gmlp(proj, w_s[i], b_s[i].T, ln_v_g[i][None], ln_v_b[i][None], g_out_a[i][None])
        state = None
        for dil in sorted((dil for _, dil in DILATED), reverse=True):
            state = _attn_pass(dil, qkv_rm, proj, rel_bias, state, g_out_b[i][None],
                               batch=batch, seq=seq, w_b=w_b)
        y_b = state
        h = _out_proj(x2, y_a, y_b, w_out[i].astype(BF16))
        x2 = _ple(h, g_ple[i][None], w_ple_gate[i].astype(BF16), p[i].reshape(m, -1), w_ple_up[i].astype(BF16))
    return x2.reshape(batch, seq, d)
```

```python
import functools
import math

import numpy as np
import jax
import jax.numpy as jnp
from jax import lax
from jax.experimental import pallas as pl
from jax.experimental.pallas import tpu as pltpu

HEAD_DIM = 128
CHUNK = 128
BLK = 128
QK_LOOKAHEAD = 3
DILATED = ((128, 1), (512, 4), (2048, 16))
MAX_DIL = 16
NUM_BUCKETS = 32
MAX_DISTANCE = 2048
EPS = 1e-6
NEG_INF = -1e30
N_SEG = 7
SEG_Q, SEG_K, SEG_V, SEG_BZ = 3, 4, 5, 6

V7X_VMEM_LIMIT_BYTES = 56 * 1024 * 1024
MXU_COLS = 256

BF16 = jnp.bfloat16
F32 = jnp.float32


def _pick(n, pref):
    t = min(n, pref)
    while n % t:
        t //= 2
    return t


def _params(sem):
    return pltpu.CompilerParams(dimension_semantics=sem,
                                vmem_limit_bytes=V7X_VMEM_LIMIT_BYTES)


def _gelu(v):
    return 0.5 * v * (1.0 + lax.erf(v * (1.0 / math.sqrt(2.0))))


def _silu(v):
    return v * jax.nn.sigmoid(v)


def _normalise_rows(x_ref, g_ref, hn_ref, rows):
    tm = x_ref.shape[0]

    def body(c, carry):
        r = pl.ds(pl.multiple_of(c * rows, rows), rows)
        xc = x_ref[r, :]
        ms = jnp.mean(xc * xc, axis=-1, keepdims=True)
        hn_ref[r, :] = (xc * lax.rsqrt(ms + EPS) * g_ref[...]).astype(BF16)
        return carry

    lax.fori_loop(0, tm // rows, body, 0, unroll=4)


def _in_proj_kernel(x_ref, g_ref, w_ref, gq_ref, gk_ref, perm_ref, o_ref, op_ref, hn_ref, *, tiles_per_seg):
    j = pl.program_id(1)

    @pl.when(j == 0)
    def _():
        _normalise_rows(x_ref, g_ref, hn_ref, 16)

    seg = j // tiles_per_seg
    tm, tn = o_ref.shape
    n_sub = tn // MXU_COLS

    def sub_cols(c):
        return slice(c * MXU_COLS, (c + 1) * MXU_COLS)

    def product(c):
        return jnp.dot(hn_ref[...], w_ref[:, sub_cols(c)], preferred_element_type=F32)

    def tile(epilogue, residue_major=False):
        acc = product(0)
        for c in range(n_sub):
            nxt = product(c + 1) if c + 1 < n_sub else None
            y = epilogue(acc).astype(BF16)
            o_ref[:, sub_cols(c)] = y
            if residue_major:
                moved = jnp.dot(perm_ref[...], y, preferred_element_type=F32).astype(BF16)
                op_ref[0, :, :, sub_cols(c)] = moved.reshape(MAX_DIL, tm // MAX_DIL, MXU_COLS)
            acc = nxt

    def head_norm(g, scale):
        def epilogue(acc):
            out = []
            for c in range(MXU_COLS // HEAD_DIM):
                blk = acc[:, c * HEAD_DIM:(c + 1) * HEAD_DIM]
                ms = jnp.mean(blk * blk, axis=-1, keepdims=True)
                y = blk * lax.rsqrt(ms + EPS) * g
                out.append(y if scale is None else y * scale)
            return jnp.concatenate(out, axis=1)
        return epilogue

    @pl.when(seg <= 1)
    def _():
        tile(_gelu)

    @pl.when((seg == 2) | (seg == SEG_BZ))
    def _():
        tile(_silu)

    @pl.when(seg == SEG_Q)
    def _():
        tile(head_norm(gq_ref[...], HEAD_DIM ** -0.5), residue_major=True)

    @pl.when(seg == SEG_K)
    def _():
        tile(head_norm(gk_ref[...], None), residue_major=True)

    @pl.when(seg == SEG_V)
    def _():
        tile(lambda acc: acc, residue_major=True)


def _residue_major_perm(tm):
    runs = tm // MAX_DIL
    rho = np.arange(tm)
    src = MAX_DIL * (rho % runs) + rho // runs
    p = np.zeros((tm, tm), np.float32)
    p[rho, src] = 1.0
    return jnp.asarray(p, BF16)


def _in_proj(x2, g_pre, w_in_bf, g_q, g_k, *, batch, seq):
    m, d = x2.shape
    d_in = w_in_bf.shape[1]
    seg_w = d_in // N_SEG
    tm = _pick(seq, 512)
    tn = _pick(seg_w, 1024)
    tps = seg_w // tn
    assert tm % (MAX_DIL * 16) == 0, "residue-major runs must cover whole bf16 sublane tiles"
    blocks_per_seq = seq // tm
    sub = seq // MAX_DIL
    runs = tm // MAX_DIL
    kern = functools.partial(_in_proj_kernel, tiles_per_seg=tps)

    def perm_idx(i, j):
        jj = jnp.clip(j - SEG_Q * tps, 0, 3 * tps - 1)
        return (i // blocks_per_seq, 0, i % blocks_per_seq, jj)

    return pl.pallas_call(
        kern,
        grid=(m // tm, d_in // tn),
        in_specs=[
            pl.BlockSpec((tm, d), lambda i, j: (i, 0)),
            pl.BlockSpec((1, d), lambda i, j: (0, 0)),
            pl.BlockSpec((d, tn), lambda i, j: (0, j)),
            pl.BlockSpec((1, HEAD_DIM), lambda i, j: (0, 0)),
            pl.BlockSpec((1, HEAD_DIM), lambda i, j: (0, 0)),
            pl.BlockSpec((tm, tm), lambda i, j: (0, 0)),
        ],
        out_specs=[
            pl.BlockSpec((tm, tn), lambda i, j: (i, j)),
            pl.BlockSpec((1, MAX_DIL, runs, tn), perm_idx),
        ],
        out_shape=[
            jax.ShapeDtypeStruct((m, d_in), BF16),
            jax.ShapeDtypeStruct((batch, MAX_DIL, sub, 3 * seg_w), BF16),
        ],
        scratch_shapes=[pltpu.VMEM((tm, d), BF16)],
        compiler_params=_params(("arbitrary", "arbitrary")),
        name="in_proj",
    )(x2, g_pre, w_in_bf, g_q, g_k, _residue_major_perm(tm))


def _gmlp_kernel(au_ref, av_ref, az_ref, ws_ref, bs_ref, lg_ref, lb_ref, go_ref, o_ref, ya_ref):
    av = av_ref[...].astype(F32)
    mu = jnp.mean(av, axis=-1, keepdims=True)
    xc = av - mu
    var = jnp.mean(xc * xc, axis=-1, keepdims=True)
    avn = xc * lax.rsqrt(var + EPS) * lg_ref[...] + lb_ref[...]

    row = lax.broadcasted_iota(jnp.int32, (CHUNK, CHUNK), 0)
    col = lax.broadcasted_iota(jnp.int32, (CHUNK, CHUNK), 1)
    causal = col <= row
    n_groups = ws_ref.shape[0]
    for g in range(n_groups):
        sl = slice(g * HEAD_DIM, (g + 1) * HEAD_DIM)
        wm = jnp.where(causal, ws_ref[g], 0.0).astype(BF16)
        z = jnp.dot(wm, avn[:, sl].astype(BF16), preferred_element_type=F32)
        z = z + bs_ref[:, g:g + 1]
        ya_ref[:, sl] = au_ref[:, sl].astype(F32) * z
    ya = ya_ref[...]
    ms = jnp.mean(ya * ya, axis=-1, keepdims=True)
    o_ref[...] = (ya * lax.rsqrt(ms + EPS) * go_ref[...] * az_ref[...].astype(F32)).astype(BF16)


def _gmlp(proj, w_s, b_s_t, ln_g, ln_b, g_out_a):
    m, d_in = proj.shape
    w_a = d_in // N_SEG
    n_groups = w_s.shape[0]
    vec = lambda: pl.BlockSpec((1, w_a), lambda i: (0, 0))
    return pl.pallas_call(
        _gmlp_kernel,
        grid=(m // CHUNK,),
        in_specs=[
            pl.BlockSpec((CHUNK, w_a), lambda i: (i, 0)),
            pl.BlockSpec((CHUNK, w_a), lambda i: (i, 1)),
            pl.BlockSpec((CHUNK, w_a), lambda i: (i, 2)),
            pl.BlockSpec((n_groups, CHUNK, CHUNK), lambda i: (0, 0, 0)),
            pl.BlockSpec((CHUNK, n_groups), lambda i: (0, 0)),
            vec(), vec(), vec(),
        ],
        out_specs=pl.BlockSpec((CHUNK, w_a), lambda i: (i, 0)),
        out_shape=jax.ShapeDtypeStruct((m, w_a), BF16),
        scratch_shapes=[pltpu.VMEM((CHUNK, w_a), F32)],
        compiler_params=_params(("parallel",)),
        name="gmlp",
    )(proj, proj, proj, w_s, b_s_t, ln_g, ln_b, g_out_a)


def _rel_bucket_np(dist):
    max_exact = NUM_BUCKETS // 2
    d = np.maximum(dist, 1).astype(np.float32)
    large = max_exact + (np.log(d / np.float32(max_exact)) / np.float32(math.log(MAX_DISTANCE / max_exact))
                         * np.float32(NUM_BUCKETS - max_exact)).astype(np.int32)
    large = np.minimum(large, NUM_BUCKETS - 1)
    return np.where(dist < max_exact, dist, large).astype(np.int32)


def _band_tables(dil, pos):
    i_q = pos[:, None]
    i_k = pos[None, :]
    in_cur = i_k <= i_q
    delta = np.where(in_cur, i_q - i_k, BLK + i_q - i_k)
    return in_cur.astype(np.int32), _rel_bucket_np(delta * dil)


class _TileIO:
    def __init__(self, idx, rows_shape):
        self.idx = idx
        self.rows_shape = rows_shape

    def load(self, ref, sl):
        return ref[self.idx + (sl,)].reshape(BLK, sl.stop - sl.start)

    def store(self, ref, sl, val):
        ref[self.idx + (sl,)] = val.reshape(self.rows_shape + (sl.stop - sl.start,))


def _attn_kernel(*refs, n_heads, first, last, diag_bucket, io, io_state):
    it = iter(refs)
    cur_ref, bucket_ref, relb_ref = next(it), next(it), next(it)
    q_ref, kp_ref, kc_ref, vp_ref, vc_ref = (next(it) for _ in range(5))
    if not first:
        o_in_ref, lse_in_ref = next(it), next(it)
    if last:
        bz_ref, go_ref = next(it), next(it)
        y_ref = next(it)
    else:
        o_out_ref, lse_out_ref = next(it), next(it)
    bias_ref = next(it)
    if last:
        yb_ref, st_ref = next(it), next(it)

    n = pl.program_id(2)
    first_step = (pl.program_id(0) == 0) & (pl.program_id(1) == 0) & (n == 0)

    @pl.when(first_step)
    def _():
        bk = bucket_ref[...]
        for h in range(n_heads):
            tab = jnp.zeros((BLK, BLK), F32)
            for b in range(NUM_BUCKETS):
                tab = jnp.where(bk == b, relb_ref[b, h], tab)
            bias_ref[h] = tab

    def natural_rows(tile):
        st_ref[...] = tile
        per = BLK // MAX_DIL
        rows = [st_ref[pl.ds((MAX_DIL // 2) * per * (v % 2) + v // 2, 8, stride=per), :]
                for v in range(BLK // 8)]
        return jnp.concatenate(rows, axis=0)

    has_prev = n > 0
    in_cur = cur_ref[...] != 0
    row = lax.broadcasted_iota(jnp.int32, (BLK, BLK), 0)
    lane = lax.broadcasted_iota(jnp.int32, (BLK, BLK), 1)
    eye = row == lane
    valid = in_cur | has_prev
    nt = (((1,), (1,)), ((), ()))
    if not first:
        lse_in = io_state.load(lse_in_ref, slice(0, BLK))
        if last:
            lse_in = natural_rows(lse_in)
    lse_tile = jnp.zeros((BLK, BLK), F32)
    ones = jnp.ones((BLK, HEAD_DIM), BF16)

    def logits(h):
        sl = slice(h * HEAD_DIM, (h + 1) * HEAD_DIM)
        keys = jnp.concatenate([io.load(kp_ref, sl), io.load(kc_ref, sl)], axis=0)
        return lax.dot_general(io.load(q_ref, sl), keys, nt, preferred_element_type=F32)

    ahead = {h: logits(h) for h in range(min(QK_LOOKAHEAD, n_heads))}
    for h in range(n_heads):
        sl = slice(h * HEAD_DIM, (h + 1) * HEAD_DIM)
        if h + QK_LOOKAHEAD < n_heads:
            ahead[h + QK_LOOKAHEAD] = logits(h + QK_LOOKAHEAD)
        s2 = ahead.pop(h)
        s_p, s_c = s2[:, :BLK], s2[:, BLK:]
        s = jnp.where(valid, jnp.where(in_cur, s_c, s_p) + bias_ref[h], NEG_INF)
        dg = jnp.sum(jnp.where(eye, s_p, 0.0), axis=-1, keepdims=True) + relb_ref[diag_bucket, h]
        dg = jnp.where(has_prev, dg, NEG_INF)
        mx = jnp.maximum(jnp.max(s, axis=-1, keepdims=True), dg)
        e = jnp.exp(s - mx)
        e_d = jnp.exp(dg - mx)
        probs = jnp.concatenate([jnp.where(eye, e_d, jnp.where(in_cur, 0.0, e)),
                                 jnp.where(in_cur, e, 0.0)], axis=1).astype(BF16)
        vals = jnp.concatenate([jnp.concatenate([io.load(vp_ref, sl), ones], axis=1),
                                jnp.concatenate([io.load(vc_ref, sl), ones], axis=1)], axis=0)
        both = jnp.dot(probs, vals, preferred_element_type=F32)
        num, den = both[:, :HEAD_DIM], both[:, HEAD_DIM:]
        if first:
            o = num / den
            lse = mx + jnp.log(den)
        else:
            lse_old = lse_in[:, h:h + 1]
            o_old = io_state.load(o_in_ref, sl)
            if last:
                o_old = natural_rows(o_old)
            top = jnp.maximum(lse_old, mx)
            w_old = jnp.exp(lse_old - top)
            w_new = jnp.exp(mx - top)
            total = w_old + den * w_new
            o = (o_old * w_old + num * w_new) / total
            lse = top + jnp.log(total)
        if last:
            yb_ref[:, sl] = o
        else:
            io_state.store(o_out_ref, sl, o)
            lse_tile = jnp.where(lane == h, lse, lse_tile)
    if last:
        yb = yb_ref[...]
        ms = jnp.mean(yb * yb, axis=-1, keepdims=True)
        y_ref[...] = (yb * lax.rsqrt(ms + EPS) * go_ref[...] * bz_ref[...].astype(F32)).astype(BF16)
    else:
        io_state.store(lse_out_ref, slice(0, BLK), lse_tile)


def _attn_pass(dil, qkv_rm, proj, rel_bias, state, g_out_b, *, batch, seq, w_b):
    first = state is None
    last = dil == 1
    n_heads = w_b // HEAD_DIM
    assert 2 * n_heads <= BLK
    sub = seq // MAX_DIL
    nb = seq // dil // BLK
    rep = MAX_DIL // dil
    runs = BLK // rep
    col_q, col_k, col_v = 0, 1, 2

    if last:
        pos = np.arange(BLK)
        io = _TileIO((slice(None),), (BLK,))
        io_state = _TileIO((0, slice(None), slice(None)), (MAX_DIL, BLK // MAX_DIL))
        blk = (BLK, w_b)
        src = proj
        cur = lambda seg: (lambda b, r, n: (b * nb + n, seg))
        prev = lambda seg: (lambda b, r, n: (b * nb + jnp.maximum(n - 1, 0), seg))
        col_q, col_k, col_v = SEG_Q, SEG_K, SEG_V
        st_blk = lambda w: (1, MAX_DIL, BLK // MAX_DIL, w)
        st_idx = lambda b, r, n: (b, 0, n, 0)
        view = lambda a: a
    else:
        rho = np.arange(BLK)
        pos = rep * (rho % runs) + rho // runs
        io = io_state = _TileIO((0, slice(None), 0, slice(None)), (rep, runs))
        blk = (1, rep, 1, runs, w_b)
        src = qkv_rm.reshape(batch, rep, dil, sub, 3 * w_b)
        cur = lambda col: (lambda b, r, n: (b, 0, r, n, col))
        prev = lambda col: (lambda b, r, n: (b, 0, r, jnp.maximum(n - 1, 0), col))
        st_blk = lambda w: (1, rep, 1, runs, w)
        st_idx = lambda b, r, n: (b, 0, r, n, 0)
        view = lambda a: a.reshape(batch, rep, dil, sub, a.shape[-1])

    in_cur, bucket = _band_tables(dil, pos)
    diag_bucket = int(_rel_bucket_np(np.array([BLK * dil]))[0])
    const = lambda: pl.BlockSpec((BLK, BLK), lambda b, r, n: (0, 0))
    in_specs = [
        const(), const(), pl.BlockSpec(memory_space=pltpu.SMEM),
        pl.BlockSpec(blk, cur(col_q)),
        pl.BlockSpec(blk, prev(col_k)), pl.BlockSpec(blk, cur(col_k)),
        pl.BlockSpec(blk, prev(col_v)), pl.BlockSpec(blk, cur(col_v)),
    ]
    args = [jnp.asarray(in_cur), jnp.asarray(bucket), rel_bias, src, src, src, src, src]
    acc_spec = pl.BlockSpec(st_blk(w_b), st_idx)
    ml_spec = pl.BlockSpec(st_blk(BLK), st_idx)
    if not first:
        acc, ml = state
        in_specs += [acc_spec, ml_spec]
        args += [view(acc), view(ml)]
    scratch = [pltpu.VMEM((n_heads, BLK, BLK), F32)]
    if last:
        in_specs += [pl.BlockSpec(blk, cur(SEG_BZ)), pl.BlockSpec((1, w_b), lambda b, r, n: (0, 0))]
        args += [proj, g_out_b]
        out_specs = pl.BlockSpec(blk, cur(0))
        out_shape = jax.ShapeDtypeStruct((batch * seq, w_b), BF16)
        scratch += [pltpu.VMEM((BLK, w_b), F32), pltpu.VMEM((BLK, HEAD_DIM), F32)]
    else:
        out_specs = [acc_spec, ml_spec]
        out_shape = [jax.ShapeDtypeStruct((batch, rep, dil, sub, w_b), F32),
                     jax.ShapeDtypeStruct((batch, rep, dil, sub, BLK), F32)]
    kern = functools.partial(_attn_kernel, n_heads=n_heads, first=first, last=last,
                             diag_bucket=diag_bucket, io=io, io_state=io_state)
    out = pl.pallas_call(
        kern,
        grid=(batch, dil, nb),
        in_specs=in_specs,
        out_specs=out_specs,
        out_shape=out_shape,
        scratch_shapes=scratch,
        compiler_params=_params(("arbitrary", "arbitrary", "arbitrary")),
        name=f"attn_d{dil}",
    )(*args)
    if last:
        return out
    acc, ml = out
    return (acc.reshape(batch, MAX_DIL, sub, w_b), ml.reshape(batch, MAX_DIL, sub, BLK))


def _out_proj_kernel(x_ref, ya_ref, yb_ref, wa_ref, wb_ref, h_ref):
    def product(c):
        cols = slice(c * MXU_COLS, (c + 1) * MXU_COLS)
        return (jnp.dot(ya_ref[...], wa_ref[:, cols], preferred_element_type=F32)
                + jnp.dot(yb_ref[...], wb_ref[:, cols], preferred_element_type=F32))

    n_sub = h_ref.shape[1] // MXU_COLS
    acc = product(0)
    for c in range(n_sub):
        nxt = product(c + 1) if c + 1 < n_sub else None
        cols = slice(c * MXU_COLS, (c + 1) * MXU_COLS)
        h_ref[:, cols] = x_ref[:, cols] + acc
        acc = nxt


def _out_proj(x2, y_a, y_b, w_out_bf):
    m, d = x2.shape
    w_a = y_a.shape[1]
    w_b = y_b.shape[1]
    assert w_a == w_b
    tm = _pick(m, 1024)
    tn = _pick(d, 512)
    return pl.pallas_call(
        _out_proj_kernel,
        grid=(m // tm, d // tn),
        in_specs=[
            pl.BlockSpec((tm, tn), lambda i, j: (i, j)),
            pl.BlockSpec((tm, w_a), lambda i, j: (i, 0)),
            pl.BlockSpec((tm, w_b), lambda i, j: (i, 0)),
            pl.BlockSpec((w_a, tn), lambda i, j: (0, j)),
            pl.BlockSpec((w_b, tn), lambda i, j: (1, j)),
        ],
        out_specs=pl.BlockSpec((tm, tn), lambda i, j: (i, j)),
        out_shape=jax.ShapeDtypeStruct((m, d), F32),
        compiler_params=_params(("parallel", "arbitrary")),
        name="out_proj",
    )(x2, y_a, y_b, w_out_bf, w_out_bf)


def _ple_kernel(h_ref, g_ref, wg_ref, p_ref, wu_ref, o_ref, hn_ref):
    j = pl.program_id(1)

    @pl.when(j == 0)
    def _():
        _normalise_rows(h_ref, g_ref, hn_ref, 16)

    tn = o_ref.shape[1]
    n_sub = tn // MXU_COLS
    p_bf = p_ref[...].astype(BF16)

    def product(c):
        return jnp.dot(hn_ref[...], wg_ref[:, c * MXU_COLS:(c + 1) * MXU_COLS], preferred_element_type=F32)

    acc = product(0)
    for c in range(n_sub):
        nxt = product(c + 1) if c + 1 < n_sub else None
        cols = slice(c * MXU_COLS, (c + 1) * MXU_COLS)
        up = jnp.dot(p_bf, wu_ref[:, cols], preferred_element_type=F32)
        h_cols = pl.ds(pl.multiple_of(j * tn + c * MXU_COLS, MXU_COLS), MXU_COLS)
        o_ref[:, cols] = h_ref[:, h_cols] + jax.nn.sigmoid(acc) * up
        acc = nxt


def _ple(h, g_ple, w_gate_bf, p2, w_up_bf):
    m, d = h.shape
    d_ple = p2.shape[1]
    tm = _pick(m, 512)
    tn = _pick(d, 1024)
    return pl.pallas_call(
        _ple_kernel,
        grid=(m // tm, d // tn),
        in_specs=[
            pl.BlockSpec((tm, d), lambda i, j: (i, 0)),
            pl.BlockSpec((1, d), lambda i, j: (0, 0)),
            pl.BlockSpec((d, tn), lambda i, j: (0, j)),
            pl.BlockSpec((tm, d_ple), lambda i, j: (i, 0)),
            pl.BlockSpec((d_ple, tn), lambda i, j: (0, j)),
        ],
        out_specs=pl.BlockSpec((tm, tn), lambda i, j: (i, j)),
        out_shape=jax.ShapeDtypeStruct((m, d), F32),
        scratch_shapes=[pltpu.VMEM((tm, d), BF16)],
        compiler_params=_params(("parallel", "arbitrary")),
        name="ple",
    )(h, g_ple, w_gate_bf, p2, w_up_bf)


def kernel(x, p, g_pre, w_in, w_s, b_s, ln_v_g, ln_v_b, g_q, g_k, rel_bias, g_out_a, g_out_b, w_out, g_ple, w_ple_gate, w_ple_up):
    batch, seq, d = x.shape
    depth = p.shape[0]
    w_a = ln_v_g.shape[-1]
    w_b = g_out_b.shape[-1]
    d_in = w_in.shape[-1]
    assert w_a == w_b and d_in == N_SEG * w_a, "segments of the combined projection must be equally wide"
    assert seq % (MAX_DIL * BLK) == 0 and all(win // dil == BLK for win, dil in DILATED)
    assert sorted(dil for _, dil in DILATED) == [1, 4, MAX_DIL]
    m = batch * seq
    x2 = x.reshape(m, d)
    for i in range(depth):
        proj, qkv_rm = _in_proj(x2, g_pre[i][None], w_in[i].astype(BF16), g_q[i][None], g_k[i][None],
                                batch=batch, seq=seq)
        y_a = _gmlp(proj, w_s[i], b_s[i].T, ln_v_g[i][None], ln_v_b[i][None], g_out_a[i][None])
        state = None
        for dil in sorted((dil for _, dil in DILATED), reverse=True):
            state = _attn_pass(dil, qkv_rm, proj, rel_bias, state, g_out_b[i][None],
                               batch=batch, seq=seq, w_b=w_b)
        y_b = state
        h = _out_proj(x2, y_a, y_b, w_out[i].astype(BF16))
        x2 = _ple(h, g_ple[i][None], w_ple_gate[i].astype(BF16), p[i].reshape(m, -1), w_ple_up[i].astype(BF16))
    return x2.reshape(batch, seq, d)
```

```python
import functools
import math

import numpy as np
import jax
import jax.numpy as jnp
from jax import lax
from jax.experimental import pallas as pl
from jax.experimental.pallas import tpu as pltpu

HEAD_DIM = 128
CHUNK = 128
BLK = 128
QK_LOOKAHEAD = 3
DILATED = ((128, 1), (512, 4), (2048, 16))
MAX_DIL = 16
NUM_BUCKETS = 32
MAX_DISTANCE = 2048
EPS = 1e-6
NEG_INF = -1e30
N_SEG = 7
SEG_Q, SEG_K, SEG_V, SEG_BZ = 3, 4, 5, 6

V7X_VMEM_LIMIT_BYTES = 56 * 1024 * 1024
MXU_COLS = 256

BF16 = jnp.bfloat16
F32 = jnp.float32


def _pick(n, pref):
    t = min(n, pref)
    while n % t:
        t //= 2
    return t


def _params(sem):
    return pltpu.CompilerParams(dimension_semantics=sem,
                                vmem_limit_bytes=V7X_VMEM_LIMIT_BYTES)


def _gelu(v):
    return 0.5 * v * (1.0 + lax.erf(v * (1.0 / math.sqrt(2.0))))


def _silu(v):
    return v * jax.nn.sigmoid(v)


def _normalise_rows(x_ref, g_ref, hn_ref, rows):
    tm = x_ref.shape[0]

    def body(c, carry):
        r = pl.ds(pl.multiple_of(c * rows, rows), rows)
        xc = x_ref[r, :]
        ms = jnp.mean(xc * xc, axis=-1, keepdims=True)
        hn_ref[r, :] = (xc * lax.rsqrt(ms + EPS) * g_ref[...]).astype(BF16)
        return carry

    lax.fori_loop(0, tm // rows, body, 0, unroll=4)


def _in_proj_kernel(x_ref, g_ref, w_ref, gq_ref, gk_ref, o_ref, op_ref, hn_ref, slab_ref, *, tiles_per_seg):
    j = pl.program_id(1)

    @pl.when(j == 0)
    def _():
        _normalise_rows(x_ref, g_ref, hn_ref, 16)

    seg = j // tiles_per_seg
    tm, tn = o_ref.shape
    n_sub = tn // MXU_COLS

    def sub_cols(c):
        return slice(c * MXU_COLS, (c + 1) * MXU_COLS)

    def product(c):
        return jnp.dot(hn_ref[...], w_ref[:, sub_cols(c)], preferred_element_type=F32)

    def tile(epilogue, residue_major=False):
        acc = product(0)
        for c in range(n_sub):
            nxt = product(c + 1) if c + 1 < n_sub else None
            y = epilogue(acc)
            o_ref[:, sub_cols(c)] = y.astype(BF16)
            if residue_major:
                lanes = MXU_COLS // HEAD_DIM
                for k in range(lanes):
                    slab_ref[c * lanes + k] = y[:, k * HEAD_DIM:(k + 1) * HEAD_DIM]
                for r in range(MAX_DIL):
                    rows = [slab_ref[c * lanes + k, pl.ds(r, tm // MAX_DIL, stride=MAX_DIL), :]
                            for k in range(lanes)]
                    op_ref[0, r, :, sub_cols(c)] = jnp.concatenate(rows, axis=1).astype(BF16)
            acc = nxt

    def head_norm(g, scale):
        def epilogue(acc):
            out = []
            for c in range(MXU_COLS // HEAD_DIM):
                blk = acc[:, c * HEAD_DIM:(c + 1) * HEAD_DIM]
                ms = jnp.mean(blk * blk, axis=-1, keepdims=True)
                y = blk * lax.rsqrt(ms + EPS) * g
                out.append(y if scale is None else y * scale)
            return jnp.concatenate(out, axis=1)
        return epilogue

    @pl.when(seg <= 1)
    def _():
        tile(_gelu)

    @pl.when((seg == 2) | (seg == SEG_BZ))
    def _():
        tile(_silu)

    @pl.when(seg == SEG_Q)
    def _():
        tile(head_norm(gq_ref[...], HEAD_DIM ** -0.5), residue_major=True)

    @pl.when(seg == SEG_K)
    def _():
        tile(head_norm(gk_ref[...], None), residue_major=True)

    @pl.when(seg == SEG_V)
    def _():
        tile(lambda acc: acc, residue_major=True)


def _in_proj(x2, g_pre, w_in_bf, g_q, g_k, *, batch, seq):
    m, d = x2.shape
    d_in = w_in_bf.shape[1]
    seg_w = d_in // N_SEG
    tm = _pick(seq, 512)
    tn = _pick(seg_w, 1024)
    assert tn % MXU_COLS == 0
    tps = seg_w // tn
    assert tm % (MAX_DIL * 16) == 0, "residue-major runs must cover whole bf16 sublane tiles"
    blocks_per_seq = seq // tm
    sub = seq // MAX_DIL
    runs = tm // MAX_DIL
    kern = functools.partial(_in_proj_kernel, tiles_per_seg=tps)

    def perm_idx(i, j):
        jj = jnp.clip(j - SEG_Q * tps, 0, 3 * tps - 1)
        return (i // blocks_per_seq, 0, i % blocks_per_seq, jj)

    return pl.pallas_call(
        kern,
        grid=(m // tm, d_in // tn),
        in_specs=[
            pl.BlockSpec((tm, d), lambda i, j: (i, 0)),
            pl.BlockSpec((1, d), lambda i, j: (0, 0)),
            pl.BlockSpec((d, tn), lambda i, j: (0, j)),
            pl.BlockSpec((1, HEAD_DIM), lambda i, j: (0, 0)),
            pl.BlockSpec((1, HEAD_DIM), lambda i, j: (0, 0)),
        ],
        out_specs=[
            pl.BlockSpec((tm, tn), lambda i, j: (i, j)),
            pl.BlockSpec((1, MAX_DIL, runs, tn), perm_idx),
        ],
        out_shape=[
            jax.ShapeDtypeStruct((m, d_in), BF16),
            jax.ShapeDtypeStruct((batch, MAX_DIL, sub, 3 * seg_w), BF16),
        ],
        scratch_shapes=[pltpu.VMEM((tm, d), BF16), pltpu.VMEM((tn // HEAD_DIM, tm, HEAD_DIM), F32)],
        compiler_params=_params(("arbitrary", "arbitrary")),
        name="in_proj",
    )(x2, g_pre, w_in_bf, g_q, g_k)


def _gmlp_kernel(au_ref, av_ref, az_ref, ws_ref, bs_ref, lg_ref, lb_ref, go_ref, o_ref, ya_ref):
    av = av_ref[...].astype(F32)
    mu = jnp.mean(av, axis=-1, keepdims=True)
    xc = av - mu
    var = jnp.mean(xc * xc, axis=-1, keepdims=True)
    avn = xc * lax.rsqrt(var + EPS) * lg_ref[...] + lb_ref[...]

    row = lax.broadcasted_iota(jnp.int32, (CHUNK, CHUNK), 0)
    col = lax.broadcasted_iota(jnp.int32, (CHUNK, CHUNK), 1)
    causal = col <= row
    n_groups = ws_ref.shape[0]
    for g in range(n_groups):
        sl = slice(g * HEAD_DIM, (g + 1) * HEAD_DIM)
        wm = jnp.where(causal, ws_ref[g], 0.0).astype(BF16)
        z = jnp.dot(wm, avn[:, sl].astype(BF16), preferred_element_type=F32)
        z = z + bs_ref[:, g:g + 1]
        ya_ref[:, sl] = au_ref[:, sl].astype(F32) * z
    ya = ya_ref[...]
    ms = jnp.mean(ya * ya, axis=-1, keepdims=True)
    o_ref[...] = (ya * lax.rsqrt(ms + EPS) * go_ref[...] * az_ref[...].astype(F32)).astype(BF16)


def _gmlp(proj, w_s, b_s_t, ln_g, ln_b, g_out_a):
    m, d_in = proj.shape
    w_a = d_in // N_SEG
    n_groups = w_s.shape[0]
    vec = lambda: pl.BlockSpec((1, w_a), lambda i: (0, 0))
    return pl.pallas_call(
        _gmlp_kernel,
        grid=(m // CHUNK,),
        in_specs=[
            pl.BlockSpec((CHUNK, w_a), lambda i: (i, 0)),
            pl.BlockSpec((CHUNK, w_a), lambda i: (i, 1)),
            pl.BlockSpec((CHUNK, w_a), lambda i: (i, 2)),
            pl.BlockSpec((n_groups, CHUNK, CHUNK), lambda i: (0, 0, 0)),
            pl.BlockSpec((CHUNK, n_groups), lambda i: (0, 0)),
            vec(), vec(), vec(),
        ],
        out_specs=pl.BlockSpec((CHUNK, w_a), lambda i: (i, 0)),
        out_shape=jax.ShapeDtypeStruct((m, w_a), BF16),
        scratch_shapes=[pltpu.VMEM((CHUNK, w_a), F32)],
        compiler_params=_params(("parallel",)),
        name="gmlp",
    )(proj, proj, proj, w_s, b_s_t, ln_g, ln_b, g_out_a)


def _rel_bucket_np(dist):
    max_exact = NUM_BUCKETS // 2
    d = np.maximum(dist, 1).astype(np.float32)
    large = max_exact + (np.log(d / np.float32(max_exact)) / np.float32(math.log(MAX_DISTANCE / max_exact))
                         * np.float32(NUM_BUCKETS - max_exact)).astype(np.int32)
    large = np.minimum(large, NUM_BUCKETS - 1)
    return np.where(dist < max_exact, dist, large).astype(np.int32)


def _band_tables(dil, pos):
    i_q = pos[:, None]
    i_k = pos[None, :]
    in_cur = i_k <= i_q
    delta = np.where(in_cur, i_q - i_k, BLK + i_q - i_k)
    return in_cur.astype(np.int32), _rel_bucket_np(delta * dil)


class _TileIO:
    def __init__(self, idx, rows_shape):
        self.idx = idx
        self.rows_shape = rows_shape

    def load(self, ref, sl):
        return ref[self.idx + (sl,)].reshape(BLK, sl.stop - sl.start)

    def store(self, ref, sl, val):
        ref[self.idx + (sl,)] = val.reshape(self.rows_shape + (sl.stop - sl.start,))


def _attn_kernel(*refs, n_heads, first, last, diag_bucket, io, io_state):
    it = iter(refs)
    cur_ref, bucket_ref, relb_ref = next(it), next(it), next(it)
    q_ref, kp_ref, kc_ref, vp_ref, vc_ref = (next(it) for _ in range(5))
    if not first:
        o_in_ref, lse_in_ref = next(it), next(it)
    if last:
        bz_ref, go_ref = next(it), next(it)
        y_ref = next(it)
    else:
        o_out_ref, lse_out_ref = next(it), next(it)
    bias_ref = next(it)
    if last:
        yb_ref, st_ref = next(it), next(it)

    n = pl.program_id(2)
    first_step = (pl.program_id(0) == 0) & (pl.program_id(1) == 0) & (n == 0)

    @pl.when(first_step)
    def _():
        bk = bucket_ref[...]
        for h in range(n_heads):
            tab = jnp.zeros((BLK, BLK), F32)
            for b in range(NUM_BUCKETS):
                tab = jnp.where(bk == b, relb_ref[b, h], tab)
            bias_ref[h] = tab

    def natural_rows(tile):
        st_ref[...] = tile
        per = BLK // MAX_DIL
        rows = [st_ref[pl.ds((MAX_DIL // 2) * per * (v % 2) + v // 2, 8, stride=per), :]
                for v in range(BLK // 8)]
        return jnp.concatenate(rows, axis=0)

    has_prev = n > 0
    in_cur = cur_ref[...] != 0
    row = lax.broadcasted_iota(jnp.int32, (BLK, BLK), 0)
    lane = lax.broadcasted_iota(jnp.int32, (BLK, BLK), 1)
    eye = row == lane
    valid = in_cur | has_prev
    nt = (((1,), (1,)), ((), ()))
    if not first:
        lse_in = io_state.load(lse_in_ref, slice(0, BLK))
        if last:
            lse_in = natural_rows(lse_in)
    lse_tile = jnp.zeros((BLK, BLK), F32)
    ones = jnp.ones((BLK, HEAD_DIM), BF16)

    def logits(h):
        sl = slice(h * HEAD_DIM, (h + 1) * HEAD_DIM)
        keys = jnp.concatenate([io.load(kp_ref, sl), io.load(kc_ref, sl)], axis=0)
        return lax.dot_general(io.load(q_ref, sl), keys, nt, preferred_element_type=F32)

    def softmax(h, s2):
        s_p, s_c = s2[:, :BLK], s2[:, BLK:]
        s = jnp.where(valid, jnp.where(in_cur, s_c, s_p) + bias_ref[h], NEG_INF)
        dg = jnp.sum(jnp.where(eye, s_p, 0.0), axis=-1, keepdims=True) + relb_ref[diag_bucket, h]
        dg = jnp.where(has_prev, dg, NEG_INF)
        mx = jnp.maximum(jnp.max(s, axis=-1, keepdims=True), dg)
        lse_old = None
        if not first:
            lse_old = jnp.broadcast_to(lse_in[:, h:h + 1], (BLK, HEAD_DIM))
            mx = jnp.maximum(mx, lse_old)
        e = jnp.exp(s - mx)
        e_d = jnp.exp(dg - mx)
        probs = jnp.concatenate([jnp.where(eye, e_d, jnp.where(in_cur, 0.0, e)),
                                 jnp.where(in_cur, e, 0.0)], axis=1).astype(BF16)
        return probs, mx, lse_old

    def values(h, probs, mx, lse_old):
        sl = slice(h * HEAD_DIM, (h + 1) * HEAD_DIM)
        vals = jnp.concatenate([jnp.concatenate([io.load(vp_ref, sl), ones], axis=1),
                                jnp.concatenate([io.load(vc_ref, sl), ones], axis=1)], axis=0)
        both = jnp.dot(probs, vals, preferred_element_type=F32)
        num, den = both[:, :HEAD_DIM], both[:, HEAD_DIM:]
        if first:
            total = den
            o = num / total
        else:
            o_old = io_state.load(o_in_ref, sl)
            if last:
                o_old = natural_rows(o_old)
            w_old = jnp.exp(lse_old - mx)
            total = w_old + den
            o = (o_old * w_old + num) / total
        if last:
            yb_ref[:, sl] = o
            return None
        io_state.store(o_out_ref, sl, o)
        return mx + jnp.log(total)

    ahead = {h: logits(h) for h in range(min(QK_LOOKAHEAD, n_heads))}
    soft = softmax(0, ahead.pop(0))
    for h in range(n_heads):
        if h + QK_LOOKAHEAD < n_heads:
            ahead[h + QK_LOOKAHEAD] = logits(h + QK_LOOKAHEAD)
        nxt = softmax(h + 1, ahead.pop(h + 1)) if h + 1 < n_heads else None
        lse = values(h, *soft)
        if not last:
            lse_tile = jnp.where(lane == h, lse, lse_tile)
        soft = nxt
    if last:
        yb = yb_ref[...]
        ms = jnp.mean(yb * yb, axis=-1, keepdims=True)
        y_ref[...] = (yb * lax.rsqrt(ms + EPS) * go_ref[...] * bz_ref[...].astype(F32)).astype(BF16)
    else:
        io_state.store(lse_out_ref, slice(0, BLK), lse_tile)


def _attn_pass(dil, qkv_rm, proj, rel_bias, state, g_out_b, *, batch, seq, w_b):
    first = state is None
    last = dil == 1
    n_heads = w_b // HEAD_DIM
    assert 2 * n_heads <= BLK
    sub = seq // MAX_DIL
    nb = seq // dil // BLK
    rep = MAX_DIL // dil
    runs = BLK // rep
    col_q, col_k, col_v = 0, 1, 2

    if last:
        pos = np.arange(BLK)
        io = _TileIO((slice(None),), (BLK,))
        io_state = _TileIO((0, slice(None), slice(None)), (MAX_DIL, BLK // MAX_DIL))
        blk = (BLK, w_b)
        src = proj
        cur = lambda seg: (lambda b, r, n: (b * nb + n, seg))
        prev = lambda seg: (lambda b, r, n: (b * nb + jnp.maximum(n - 1, 0), seg))
        col_q, col_k, col_v = SEG_Q, SEG_K, SEG_V
        st_blk = lambda w: (1, MAX_DIL, BLK // MAX_DIL, w)
        st_idx = lambda b, r, n: (b, 0, n, 0)
        view = lambda a: a
    else:
        rho = np.arange(BLK)
        pos = rep * (rho % runs) + rho // runs
        io = io_state = _TileIO((0, slice(None), 0, slice(None)), (rep, runs))
        blk = (1, rep, 1, runs, w_b)
        src = qkv_rm.reshape(batch, rep, dil, sub, 3 * w_b)
        cur = lambda col: (lambda b, r, n: (b, 0, r, n, col))
        prev = lambda col: (lambda b, r, n: (b, 0, r, jnp.maximum(n - 1, 0), col))
        st_blk = lambda w: (1, rep, 1, runs, w)
        st_idx = lambda b, r, n: (b, 0, r, n, 0)
        view = lambda a: a.reshape(batch, rep, dil, sub, a.shape[-1])

    in_cur, bucket = _band_tables(dil, pos)
    diag_bucket = int(_rel_bucket_np(np.array([BLK * dil]))[0])
    const = lambda: pl.BlockSpec((BLK, BLK), lambda b, r, n: (0, 0))
    in_specs = [
        const(), const(), pl.BlockSpec(memory_space=pltpu.SMEM),
        pl.BlockSpec(blk, cur(col_q)),
        pl.BlockSpec(blk, prev(col_k)), pl.BlockSpec(blk, cur(col_k)),
        pl.BlockSpec(blk, prev(col_v)), pl.BlockSpec(blk, cur(col_v)),
    ]
    args = [jnp.asarray(in_cur), jnp.asarray(bucket), rel_bias, src, src, src, src, src]
    acc_spec = pl.BlockSpec(st_blk(w_b), st_idx)
    ml_spec = pl.BlockSpec(st_blk(BLK), st_idx)
    if not first:
        acc, ml = state
        in_specs += [acc_spec, ml_spec]
        args += [view(acc), view(ml)]
    scratch = [pltpu.VMEM((n_heads, BLK, BLK), F32)]
    if last:
        in_specs += [pl.BlockSpec(blk, cur(SEG_BZ)), pl.BlockSpec((1, w_b), lambda b, r, n: (0, 0))]
        args += [proj, g_out_b]
        out_specs = pl.BlockSpec(blk, cur(0))
        out_shape = jax.ShapeDtypeStruct((batch * seq, w_b), BF16)
        scratch += [pltpu.VMEM((BLK, w_b), F32), pltpu.VMEM((BLK, HEAD_DIM), F32)]
    else:
        out_specs = [acc_spec, ml_spec]
        out_shape = [jax.ShapeDtypeStruct((batch, rep, dil, sub, w_b), F32),
                     jax.ShapeDtypeStruct((batch, rep, dil, sub, BLK), F32)]
    kern = functools.partial(_attn_kernel, n_heads=n_heads, first=first, last=last,
                             diag_bucket=diag_bucket, io=io, io_state=io_state)
    out = pl.pallas_call(
        kern,
        grid=(batch, dil, nb),
        in_specs=in_specs,
        out_specs=out_specs,
        out_shape=out_shape,
        scratch_shapes=scratch,
        compiler_params=_params(("arbitrary", "arbitrary", "arbitrary")),
        name=f"attn_d{dil}",
    )(*args)
    if last:
        return out
    acc, ml = out
    return (acc.reshape(batch, MAX_DIL, sub, w_b), ml.reshape(batch, MAX_DIL, sub, BLK))


def _out_proj_kernel(x_ref, ya_ref, yb_ref, wa_ref, wb_ref, h_ref):
    def product(c):
        cols = slice(c * MXU_COLS, (c + 1) * MXU_COLS)
        return (jnp.dot(ya_ref[...], wa_ref[:, cols], preferred_element_type=F32)
                + jnp.dot(yb_ref[...], wb_ref[:, cols], preferred_element_type=F32))

    n_sub = h_ref.shape[1] // MXU_COLS
    acc = product(0)
    for c in range(n_sub):
        nxt = product(c + 1) if c + 1 < n_sub else None
        cols = slice(c * MXU_COLS, (c + 1) * MXU_COLS)
        h_ref[:, cols] = x_ref[:, cols] + acc
        acc = nxt


def _out_proj(x2, y_a, y_b, w_out_bf):
    m, d = x2.shape
    w_a = y_a.shape[1]
    w_b = y_b.shape[1]
    assert w_a == w_b
    tm = _pick(m, 1024)
    tn = _pick(d, 512)
    assert tn % MXU_COLS == 0
    return pl.pallas_call(
        _out_proj_kernel,
        grid=(m // tm, d // tn),
        in_specs=[
            pl.BlockSpec((tm, tn), lambda i, j: (i, j)),
            pl.BlockSpec((tm, w_a), lambda i, j: (i, 0)),
            pl.BlockSpec((tm, w_b), lambda i, j: (i, 0)),
            pl.BlockSpec((w_a, tn), lambda i, j: (0, j)),
            pl.BlockSpec((w_b, tn), lambda i, j: (1, j)),
        ],
        out_specs=pl.BlockSpec((tm, tn), lambda i, j: (i, j)),
        out_shape=jax.ShapeDtypeStruct((m, d), F32),
        compiler_params=_params(("parallel", "arbitrary")),
        name="out_proj",
    )(x2, y_a, y_b, w_out_bf, w_out_bf)


def _ple_kernel(h_ref, g_ref, wg_ref, p_ref, wu_ref, o_ref, hn_ref):
    j = pl.program_id(1)

    @pl.when(j == 0)
    def _():
        _normalise_rows(h_ref, g_ref, hn_ref, 16)

    tn = o_ref.shape[1]
    n_sub = tn // MXU_COLS
    p_bf = p_ref[...].astype(BF16)

    def product(c):
        return jnp.dot(hn_ref[...], wg_ref[:, c * MXU_COLS:(c + 1) * MXU_COLS], preferred_element_type=F32)

    acc = product(0)
    for c in range(n_sub):
        nxt = product(c + 1) if c + 1 < n_sub else None
        cols = slice(c * MXU_COLS, (c + 1) * MXU_COLS)
        up = jnp.dot(p_bf, wu_ref[:, cols], preferred_element_type=F32)
        h_cols = pl.ds(pl.multiple_of(j * tn + c * MXU_COLS, MXU_COLS), MXU_COLS)
        o_ref[:, cols] = h_ref[:, h_cols] + jax.nn.sigmoid(acc) * up
        acc = nxt


def _ple(h, g_ple, w_gate_bf, p2, w_up_bf):
    m, d = h.shape
    d_ple = p2.shape[1]
    tm = _pick(m, 512)
    tn = _pick(d, 1024)
    assert tn % MXU_COLS == 0
    return pl.pallas_call(
        _ple_kernel,
        grid=(m // tm, d // tn),
        in_specs=[
            pl.BlockSpec((tm, d), lambda i, j: (i, 0)),
            pl.BlockSpec((1, d), lambda i, j: (0, 0)),
            pl.BlockSpec((d, tn), lambda i, j: (0, j)),
            pl.BlockSpec((tm, d_ple), lambda i, j: (i, 0)),
            pl.BlockSpec((d_ple, tn), lambda i, j: (0, j)),
        ],
        out_specs=pl.BlockSpec((tm, tn), lambda i, j: (i, j)),
        out_shape=jax.ShapeDtypeStruct((m, d), F32),
        scratch_shapes=[pltpu.VMEM((tm, d), BF16)],
        compiler_params=_params(("parallel", "arbitrary")),
        name="ple",
    )(h, g_ple, w_gate_bf, p2, w_up_bf)


def kernel(x, p, g_pre, w_in, w_s, b_s, ln_v_g, ln_v_b, g_q, g_k, rel_bias, g_out_a, g_out_b, w_out, g_ple, w_ple_gate, w_ple_up):
    batch, seq, d = x.shape
    depth = p.shape[0]
    w_a = ln_v_g.shape[-1]
    w_b = g_out_b.shape[-1]
    d_in = w_in.shape[-1]
    assert w_a == w_b and d_in == N_SEG * w_a, "segments of the combined projection must be equally wide"
    assert seq % (MAX_DIL * BLK) == 0 and all(win // dil == BLK for win, dil in DILATED)
    assert sorted(dil for _, dil in DILATED) == [1, 4, MAX_DIL]
    m = batch * seq
    x2 = x.reshape(m, d)
    for i in range(depth):
        proj, qkv_rm = _in_proj(x2, g_pre[i][None], w_in[i].astype(BF16), g_q[i][None], g_k[i][None],
                                batch=batch, seq=seq)
        y_a = _gmlp(proj, w_s[i], b_s[i].T, ln_v_g[i][None], ln_v_b[i][None], g_out_a[i][None])
        state = None
        for dil in sorted((dil for _, dil in DILATED), reverse=True):
            state = _attn_pass(dil, qkv_rm, proj, rel_bias, state, g_out_b[i][None],
                               batch=batch, seq=seq, w_b=w_b)
        y_b = state
        h = _out_proj(x2, y_a, y_b, w_out[i].astype(BF16))
        x2 = _ple(h, g_ple[i][None], w_ple_gate[i].astype(BF16), p[i].reshape(m, -1), w_ple_up[i].astype(BF16))
    return x2.reshape(batch, seq, d)
```

```python
import functools
import math

import numpy as np
import jax
import jax.numpy as jnp
from jax import lax
from jax.experimental import pallas as pl
from jax.experimental.pallas import tpu as pltpu

HEAD_DIM = 128
CHUNK = 128
BLK = 128
QK_LOOKAHEAD = 3
DILATED = ((128, 1), (512, 4), (2048, 16))
MAX_DIL = 16
NUM_BUCKETS = 32
MAX_DISTANCE = 2048
EPS = 1e-6
NEG_INF = -1e30
N_SEG = 7
SEG_Q, SEG_K, SEG_V, SEG_BZ = 3, 4, 5, 6

V7X_VMEM_LIMIT_BYTES = 56 * 1024 * 1024
MXU_COLS = 256
SLAB_PITCH = 24

BF16 = jnp.bfloat16
F32 = jnp.float32


def _pick(n, pref):
    t = min(n, pref)
    while n % t:
        t //= 2
    return t


def _params(sem):
    return pltpu.CompilerParams(dimension_semantics=sem,
                                vmem_limit_bytes=V7X_VMEM_LIMIT_BYTES)


def _gelu(v):
    return 0.5 * v * (1.0 + lax.erf(v * (1.0 / math.sqrt(2.0))))


def _silu(v):
    return v * jax.nn.sigmoid(v)


def _normalise_rows(x_ref, g_ref, hn_ref, rows):
    tm = x_ref.shape[0]

    def body(c, carry):
        r = pl.ds(pl.multiple_of(c * rows, rows), rows)
        xc = x_ref[r, :]
        ms = jnp.mean(xc * xc, axis=-1, keepdims=True)
        hn_ref[r, :] = (xc * lax.rsqrt(ms + EPS) * g_ref[...]).astype(BF16)
        return carry

    lax.fori_loop(0, tm // rows, body, 0, unroll=4)


def _in_proj_kernel(x_ref, g_ref, w_ref, gq_ref, gk_ref, o_ref, op_ref, hn_ref, slab_ref, *, tiles_per_seg):
    j = pl.program_id(1)

    @pl.when(j == 0)
    def _():
        _normalise_rows(x_ref, g_ref, hn_ref, 16)

    seg = j // tiles_per_seg
    tm, tn = o_ref.shape
    n_sub = tn // MXU_COLS

    def sub_cols(c):
        return slice(c * MXU_COLS, (c + 1) * MXU_COLS)

    def product(c):
        return jnp.dot(hn_ref[...], w_ref[:, sub_cols(c)], preferred_element_type=F32)

    def tile(epilogue, residue_major=False):
        acc = product(0)
        for c in range(n_sub):
            nxt = product(c + 1) if c + 1 < n_sub else None
            lanes = MXU_COLS // HEAD_DIM
            y = epilogue(acc)
            o_ref[:, sub_cols(c)] = y.astype(BF16)
            if residue_major:
                for k in range(lanes):
                    for a in range(tm // MAX_DIL):
                        slab_ref[c * lanes + k, a * SLAB_PITCH:a * SLAB_PITCH + MAX_DIL, :] = (
                            y[a * MAX_DIL:(a + 1) * MAX_DIL, k * HEAD_DIM:(k + 1) * HEAD_DIM])
                for r in range(MAX_DIL):
                    rows = [slab_ref[c * lanes + k, pl.ds(r, tm // MAX_DIL, stride=SLAB_PITCH), :]
                            for k in range(lanes)]
                    op_ref[0, r, :, sub_cols(c)] = jnp.concatenate(rows, axis=1).astype(BF16)
            acc = nxt

    def head_norm(g, scale):
        def epilogue(acc):
            out = []
            for c in range(MXU_COLS // HEAD_DIM):
                blk = acc[:, c * HEAD_DIM:(c + 1) * HEAD_DIM]
                ms = jnp.mean(blk * blk, axis=-1, keepdims=True)
                y = blk * lax.rsqrt(ms + EPS) * g
                out.append(y if scale is None else y * scale)
            return jnp.concatenate(out, axis=1)
        return epilogue

    @pl.when(seg <= 1)
    def _():
        tile(_gelu)

    @pl.when((seg == 2) | (seg == SEG_BZ))
    def _():
        tile(_silu)

    @pl.when(seg == SEG_Q)
    def _():
        tile(head_norm(gq_ref[...], HEAD_DIM ** -0.5), residue_major=True)

    @pl.when(seg == SEG_K)
    def _():
        tile(head_norm(gk_ref[...], None), residue_major=True)

    @pl.when(seg == SEG_V)
    def _():
        tile(lambda acc: acc, residue_major=True)


def _in_proj(x2, g_pre, w_in_bf, g_q, g_k, *, batch, seq):
    m, d = x2.shape
    d_in = w_in_bf.shape[1]
    seg_w = d_in // N_SEG
    tm = _pick(seq, 512)
    tn = _pick(seg_w, 1024)
    assert tn % MXU_COLS == 0
    tps = seg_w // tn
    assert tm % (MAX_DIL * 16) == 0, "residue-major runs must cover whole bf16 sublane tiles"
    blocks_per_seq = seq // tm
    sub = seq // MAX_DIL
    runs = tm // MAX_DIL
    kern = functools.partial(_in_proj_kernel, tiles_per_seg=tps)

    def perm_idx(i, j):
        jj = jnp.clip(j - SEG_Q * tps, 0, 3 * tps - 1)
        return (i // blocks_per_seq, 0, i % blocks_per_seq, jj)

    return pl.pallas_call(
        kern,
        grid=(m // tm, d_in // tn),
        in_specs=[
            pl.BlockSpec((tm, d), lambda i, j: (i, 0)),
            pl.BlockSpec((1, d), lambda i, j: (0, 0)),
            pl.BlockSpec((d, tn), lambda i, j: (0, j)),
            pl.BlockSpec((1, HEAD_DIM), lambda i, j: (0, 0)),
            pl.BlockSpec((1, HEAD_DIM), lambda i, j: (0, 0)),
        ],
        out_specs=[
            pl.BlockSpec((tm, tn), lambda i, j: (i, j)),
            pl.BlockSpec((1, MAX_DIL, runs, tn), perm_idx),
        ],
        out_shape=[
            jax.ShapeDtypeStruct((m, d_in), BF16),
            jax.ShapeDtypeStruct((batch, MAX_DIL, sub, 3 * seg_w), BF16),
        ],
        scratch_shapes=[pltpu.VMEM((tm, d), BF16), pltpu.VMEM((tn // HEAD_DIM, runs * SLAB_PITCH, HEAD_DIM), F32)],
        compiler_params=_params(("arbitrary", "arbitrary")),
        name="in_proj",
    )(x2, g_pre, w_in_bf, g_q, g_k)


def _gmlp_kernel(au_ref, av_ref, az_ref, ws_ref, bs_ref, lg_ref, lb_ref, go_ref, o_ref, ya_ref):
    av = av_ref[...].astype(F32)
    mu = jnp.mean(av, axis=-1, keepdims=True)
    xc = av - mu
    var = jnp.mean(xc * xc, axis=-1, keepdims=True)
    avn = xc * lax.rsqrt(var + EPS) * lg_ref[...] + lb_ref[...]

    row = lax.broadcasted_iota(jnp.int32, (CHUNK, CHUNK), 0)
    col = lax.broadcasted_iota(jnp.int32, (CHUNK, CHUNK), 1)
    causal = col <= row
    n_groups = ws_ref.shape[0]
    for g in range(n_groups):
        sl = slice(g * HEAD_DIM, (g + 1) * HEAD_DIM)
        wm = jnp.where(causal, ws_ref[g], 0.0).astype(BF16)
        z = jnp.dot(wm, avn[:, sl].astype(BF16), preferred_element_type=F32)
        z = z + bs_ref[:, g:g + 1]
        ya_ref[:, sl] = au_ref[:, sl].astype(F32) * z
    ya = ya_ref[...]
    ms = jnp.mean(ya * ya, axis=-1, keepdims=True)
    o_ref[...] = (ya * lax.rsqrt(ms + EPS) * go_ref[...] * az_ref[...].astype(F32)).astype(BF16)


def _gmlp(proj, w_s, b_s_t, ln_g, ln_b, g_out_a):
    m, d_in = proj.shape
    w_a = d_in // N_SEG
    n_groups = w_s.shape[0]
    vec = lambda: pl.BlockSpec((1, w_a), lambda i: (0, 0))
    return pl.pallas_call(
        _gmlp_kernel,
        grid=(m // CHUNK,),
        in_specs=[
            pl.BlockSpec((CHUNK, w_a), lambda i: (i, 0)),
            pl.BlockSpec((CHUNK, w_a), lambda i: (i, 1)),
            pl.BlockSpec((CHUNK, w_a), lambda i: (i, 2)),
            pl.BlockSpec((n_groups, CHUNK, CHUNK), lambda i: (0, 0, 0)),
            pl.BlockSpec((CHUNK, n_groups), lambda i: (0, 0)),
            vec(), vec(), vec(),
        ],
        out_specs=pl.BlockSpec((CHUNK, w_a), lambda i: (i, 0)),
        out_shape=jax.ShapeDtypeStruct((m, w_a), BF16),
        scratch_shapes=[pltpu.VMEM((CHUNK, w_a), F32)],
        compiler_params=_params(("parallel",)),
        name="gmlp",
    )(proj, proj, proj, w_s, b_s_t, ln_g, ln_b, g_out_a)


def _rel_bucket_np(dist):
    max_exact = NUM_BUCKETS // 2
    d = np.maximum(dist, 1).astype(np.float32)
    large = max_exact + (np.log(d / np.float32(max_exact)) / np.float32(math.log(MAX_DISTANCE / max_exact))
                         * np.float32(NUM_BUCKETS - max_exact)).astype(np.int32)
    large = np.minimum(large, NUM_BUCKETS - 1)
    return np.where(dist < max_exact, dist, large).astype(np.int32)


def _band_tables(dil, pos):
    i_q = pos[:, None]
    i_k = pos[None, :]
    in_cur = i_k <= i_q
    delta = np.where(in_cur, i_q - i_k, BLK + i_q - i_k)
    return in_cur.astype(np.int32), _rel_bucket_np(delta * dil)


class _TileIO:
    def __init__(self, idx, rows_shape):
        self.idx = idx
        self.rows_shape = rows_shape

    def load(self, ref, sl):
        return ref[self.idx + (sl,)].reshape(BLK, sl.stop - sl.start)

    def store(self, ref, sl, val):
        ref[self.idx + (sl,)] = val.reshape(self.rows_shape + (sl.stop - sl.start,))


def _attn_kernel(*refs, n_heads, first, last, diag_bucket, io, io_state):
    it = iter(refs)
    cur_ref, bucket_ref, relb_ref = next(it), next(it), next(it)
    q_ref, kc_ref, vc_ref = next(it), next(it), next(it)
    if not first:
        o_in_ref, lse_in_ref = next(it), next(it)
    if last:
        bz_ref, go_ref = next(it), next(it)
        y_ref = next(it)
    else:
        o_out_ref, lse_out_ref = next(it), next(it)
    bias_ref, kp_ref, vp_ref = next(it), next(it), next(it)
    if last:
        yb_ref, st_ref = next(it), next(it)

    n = pl.program_id(2)
    first_step = (pl.program_id(0) == 0) & (pl.program_id(1) == 0) & (n == 0)
    prev_io = _TileIO((slice(None),), (BLK,))

    @pl.when(n == 0)
    def _():
        kp_ref[...] = jnp.zeros(kp_ref.shape, BF16)
        vp_ref[...] = jnp.zeros(vp_ref.shape, BF16)

    @pl.when(first_step)
    def _():
        bk = bucket_ref[...]
        for h in range(n_heads):
            tab = jnp.zeros((BLK, BLK), F32)
            for b in range(NUM_BUCKETS):
                tab = jnp.where(bk == b, relb_ref[b, h], tab)
            bias_ref[h] = tab

    def natural_rows(tile):
        st_ref[...] = tile
        per = BLK // MAX_DIL
        rows = [st_ref[pl.ds((MAX_DIL // 2) * per * (v % 2) + v // 2, 8, stride=per), :]
                for v in range(BLK // 8)]
        return jnp.concatenate(rows, axis=0)

    has_prev = n > 0
    in_cur = cur_ref[...] != 0
    row = lax.broadcasted_iota(jnp.int32, (BLK, BLK), 0)
    lane = lax.broadcasted_iota(jnp.int32, (BLK, BLK), 1)
    eye = row == lane
    valid = in_cur | has_prev
    nt = (((1,), (1,)), ((), ()))
    if not first:
        lse_in = io_state.load(lse_in_ref, slice(0, BLK))
        if last:
            lse_in = natural_rows(lse_in)
    lse_tile = jnp.zeros((BLK, BLK), F32)
    ones = jnp.ones((BLK, HEAD_DIM), BF16)

    def logits(h):
        sl = slice(h * HEAD_DIM, (h + 1) * HEAD_DIM)
        keys = jnp.concatenate([prev_io.load(kp_ref, sl), io.load(kc_ref, sl)], axis=0)
        return lax.dot_general(io.load(q_ref, sl), keys, nt, preferred_element_type=F32)

    def softmax(h, s2):
        s_p, s_c = s2[:, :BLK], s2[:, BLK:]
        s = jnp.where(valid, jnp.where(in_cur, s_c, s_p) + bias_ref[h], NEG_INF)
        dg = jnp.sum(jnp.where(eye, s_p, 0.0), axis=-1, keepdims=True) + relb_ref[diag_bucket, h]
        dg = jnp.where(has_prev, dg, NEG_INF)
        mx = jnp.maximum(jnp.max(s, axis=-1, keepdims=True), dg)
        lse_old = None
        if not first:
            lse_old = jnp.broadcast_to(lse_in[:, h:h + 1], (BLK, HEAD_DIM))
            mx = jnp.maximum(mx, lse_old)
        e = jnp.exp(s - mx)
        e_d = jnp.exp(dg - mx)
        probs = jnp.concatenate([jnp.where(eye, e_d, jnp.where(in_cur, 0.0, e)),
                                 jnp.where(in_cur, e, 0.0)], axis=1).astype(BF16)
        return probs, mx, lse_old

    def values(h, probs, mx, lse_old):
        sl = slice(h * HEAD_DIM, (h + 1) * HEAD_DIM)
        vals = jnp.concatenate([jnp.concatenate([prev_io.load(vp_ref, sl), ones], axis=1),
                                jnp.concatenate([io.load(vc_ref, sl), ones], axis=1)], axis=0)
        both = jnp.dot(probs, vals, preferred_element_type=F32)
        num, den = both[:, :HEAD_DIM], both[:, HEAD_DIM:]
        if first:
            total = den
            o = num / total
        else:
            o_old = io_state.load(o_in_ref, sl)
            if last:
                o_old = natural_rows(o_old)
            w_old = jnp.exp(lse_old - mx)
            total = w_old + den
            o = (o_old * w_old + num) / total
        if last:
            yb_ref[:, sl] = o
            return None
        io_state.store(o_out_ref, sl, o)
        return mx + jnp.log(total)

    ahead = {h: logits(h) for h in range(min(QK_LOOKAHEAD, n_heads))}
    soft = softmax(0, ahead.pop(0))
    for h in range(n_heads):
        if h + QK_LOOKAHEAD < n_heads:
            ahead[h + QK_LOOKAHEAD] = logits(h + QK_LOOKAHEAD)
        nxt = softmax(h + 1, ahead.pop(h + 1)) if h + 1 < n_heads else None
        lse = values(h, *soft)
        if not last:
            lse_tile = jnp.where(lane == h, lse, lse_tile)
        soft = nxt
    whole = slice(0, kp_ref.shape[1])
    kp_ref[...] = io.load(kc_ref, whole)
    vp_ref[...] = io.load(vc_ref, whole)
    if last:
        yb = yb_ref[...]
        ms = jnp.mean(yb * yb, axis=-1, keepdims=True)
        y_ref[...] = (yb * lax.rsqrt(ms + EPS) * go_ref[...] * bz_ref[...].astype(F32)).astype(BF16)
    else:
        io_state.store(lse_out_ref, slice(0, BLK), lse_tile)


def _attn_pass(dil, qkv_rm, proj, rel_bias, state, g_out_b, *, batch, seq, w_b):
    first = state is None
    last = dil == 1
    n_heads = w_b // HEAD_DIM
    assert 2 * n_heads <= BLK
    sub = seq // MAX_DIL
    nb = seq // dil // BLK
    rep = MAX_DIL // dil
    runs = BLK // rep
    col_q, col_k, col_v = 0, 1, 2

    if last:
        pos = np.arange(BLK)
        io = _TileIO((slice(None),), (BLK,))
        io_state = _TileIO((0, slice(None), slice(None)), (MAX_DIL, BLK // MAX_DIL))
        blk = (BLK, w_b)
        src = proj
        cur = lambda seg: (lambda b, r, n: (b * nb + n, seg))
        col_q, col_k, col_v = SEG_Q, SEG_K, SEG_V
        st_blk = lambda w: (1, MAX_DIL, BLK // MAX_DIL, w)
        st_idx = lambda b, r, n: (b, 0, n, 0)
        view = lambda a: a
    else:
        rho = np.arange(BLK)
        pos = rep * (rho % runs) + rho // runs
        io = io_state = _TileIO((0, slice(None), 0, slice(None)), (rep, runs))
        blk = (1, rep, 1, runs, w_b)
        src = qkv_rm.reshape(batch, rep, dil, sub, 3 * w_b)
        cur = lambda col: (lambda b, r, n: (b, 0, r, n, col))
        st_blk = lambda w: (1, rep, 1, runs, w)
        st_idx = lambda b, r, n: (b, 0, r, n, 0)
        view = lambda a: a.reshape(batch, rep, dil, sub, a.shape[-1])

    in_cur, bucket = _band_tables(dil, pos)
    diag_bucket = int(_rel_bucket_np(np.array([BLK * dil]))[0])
    const = lambda: pl.BlockSpec((BLK, BLK), lambda b, r, n: (0, 0))
    in_specs = [
        const(), const(), pl.BlockSpec(memory_space=pltpu.SMEM),
        pl.BlockSpec(blk, cur(col_q)), pl.BlockSpec(blk, cur(col_k)), pl.BlockSpec(blk, cur(col_v)),
    ]
    args = [jnp.asarray(in_cur), jnp.asarray(bucket), rel_bias, src, src, src]
    acc_spec = pl.BlockSpec(st_blk(w_b), st_idx)
    ml_spec = pl.BlockSpec(st_blk(BLK), st_idx)
    if not first:
        acc, ml = state
        in_specs += [acc_spec, ml_spec]
        args += [view(acc), view(ml)]
    scratch = [pltpu.VMEM((n_heads, BLK, BLK), F32),
               pltpu.VMEM((BLK, w_b), BF16), pltpu.VMEM((BLK, w_b), BF16)]
    if last:
        in_specs += [pl.BlockSpec(blk, cur(SEG_BZ)), pl.BlockSpec((1, w_b), lambda b, r, n: (0, 0))]
        args += [proj, g_out_b]
        out_specs = pl.BlockSpec(blk, cur(0))
        out_shape = jax.ShapeDtypeStruct((batch * seq, w_b), BF16)
        scratch += [pltpu.VMEM((BLK, w_b), F32), pltpu.VMEM((BLK, HEAD_DIM), F32)]
    else:
        out_specs = [acc_spec, ml_spec]
        out_shape = [jax.ShapeDtypeStruct((batch, rep, dil, sub, w_b), F32),
                     jax.ShapeDtypeStruct((batch, rep, dil, sub, BLK), F32)]
    kern = functools.partial(_attn_kernel, n_heads=n_heads, first=first, last=last,
                             diag_bucket=diag_bucket, io=io, io_state=io_state)
    out = pl.pallas_call(
        kern,
        grid=(batch, dil, nb),
        in_specs=in_specs,
        out_specs=out_specs,
        out_shape=out_shape,
        scratch_shapes=scratch,
        compiler_params=_params(("arbitrary", "arbitrary", "arbitrary")),
        name=f"attn_d{dil}",
    )(*args)
    if last:
        return out
    acc, ml = out
    return (acc.reshape(batch, MAX_DIL, sub, w_b), ml.reshape(batch, MAX_DIL, sub, BLK))


def _out_proj_kernel(x_ref, ya_ref, yb_ref, wa_ref, wb_ref, h_ref):
    def product(c):
        cols = slice(c * MXU_COLS, (c + 1) * MXU_COLS)
        return (jnp.dot(ya_ref[...], wa_ref[:, cols], preferred_element_type=F32)
                + jnp.dot(yb_ref[...], wb_ref[:, cols], preferred_element_type=F32))

    n_sub = h_ref.shape[1] // MXU_COLS
    acc = product(0)
    for c in range(n_sub):
        nxt = product(c + 1) if c + 1 < n_sub else None
        cols = slice(c * MXU_COLS, (c + 1) * MXU_COLS)
        h_ref[:, cols] = x_ref[:, cols] + acc
        acc = nxt


def _out_proj(x2, y_a, y_b, w_out_bf):
    m, d = x2.shape
    w_a = y_a.shape[1]
    w_b = y_b.shape[1]
    assert w_a == w_b
    tm = _pick(m, 1024)
    tn = _pick(d, 512)
    assert tn % MXU_COLS == 0
    return pl.pallas_call(
        _out_proj_kernel,
        grid=(m // tm, d // tn),
        in_specs=[
            pl.BlockSpec((tm, tn), lambda i, j: (i, j)),
            pl.BlockSpec((tm, w_a), lambda i, j: (i, 0)),
            pl.BlockSpec((tm, w_b), lambda i, j: (i, 0)),
            pl.BlockSpec((w_a, tn), lambda i, j: (0, j)),
            pl.BlockSpec((w_b, tn), lambda i, j: (1, j)),
        ],
        out_specs=pl.BlockSpec((tm, tn), lambda i, j: (i, j)),
        out_shape=jax.ShapeDtypeStruct((m, d), F32),
        compiler_params=_params(("parallel", "arbitrary")),
        name="out_proj",
    )(x2, y_a, y_b, w_out_bf, w_out_bf)


def _ple_kernel(h_ref, g_ref, wg_ref, p_ref, wu_ref, o_ref, hn_ref):
    j = pl.program_id(1)

    @pl.when(j == 0)
    def _():
        _normalise_rows(h_ref, g_ref, hn_ref, 16)

    tn = o_ref.shape[1]
    n_sub = tn // MXU_COLS
    p_bf = p_ref[...].astype(BF16)

    def product(c):
        return jnp.dot(hn_ref[...], wg_ref[:, c * MXU_COLS:(c + 1) * MXU_COLS], preferred_element_type=F32)

    acc = product(0)
    for c in range(n_sub):
        nxt = product(c + 1) if c + 1 < n_sub else None
        cols = slice(c * MXU_COLS, (c + 1) * MXU_COLS)
        up = jnp.dot(p_bf, wu_ref[:, cols], preferred_element_type=F32)
        h_cols = pl.ds(pl.multiple_of(j * tn + c * MXU_COLS, MXU_COLS), MXU_COLS)
        o_ref[:, cols] = h_ref[:, h_cols] + jax.nn.sigmoid(acc) * up
        acc = nxt


def _ple(h, g_ple, w_gate_bf, p2, w_up_bf):
    m, d = h.shape
    d_ple = p2.shape[1]
    tm = _pick(m, 512)
    tn = _pick(d, 1024)
    assert tn % MXU_COLS == 0
    return pl.pallas_call(
        _ple_kernel,
        grid=(m // tm, d // tn),
        in_specs=[
            pl.BlockSpec((tm, d), lambda i, j: (i, 0)),
            pl.BlockSpec((1, d), lambda i, j: (0, 0)),
            pl.BlockSpec((d, tn), lambda i, j: (0, j)),
            pl.BlockSpec((tm, d_ple), lambda i, j: (i, 0)),
            pl.BlockSpec((d_ple, tn), lambda i, j: (0, j)),
        ],
        out_specs=pl.BlockSpec((tm, tn), lambda i, j: (i, j)),
        out_shape=jax.ShapeDtypeStruct((m, d), F32),
        scratch_shapes=[pltpu.VMEM((tm, d), BF16)],
        compiler_params=_params(("parallel", "arbitrary")),
        name="ple",
    )(h, g_ple, w_gate_bf, p2, w_up_bf)


def kernel(x, p, g_pre, w_in, w_s, b_s, ln_v_g, ln_v_b, g_q, g_k, rel_bias, g_out_a, g_out_b, w_out, g_ple, w_ple_gate, w_ple_up):
    batch, seq, d = x.shape
    depth = p.shape[0]
    w_a = ln_v_g.shape[-1]
    w_b = g_out_b.shape[-1]
    d_in = w_in.shape[-1]
    assert w_a == w_b and d_in == N_SEG * w_a, "segments of the combined projection must be equally wide"
    assert seq % (MAX_DIL * BLK) == 0 and all(win // dil == BLK for win, dil in DILATED)
    assert sorted(dil for _, dil in DILATED) == [1, 4, MAX_DIL]
    m = batch * seq
    x2 = x.reshape(m, d)
    for i in range(depth):
        proj, qkv_rm = _in_proj(x2, g_pre[i][None], w_in[i].astype(BF16), g_q[i][None], g_k[i][None],
                                batch=batch, seq=seq)
        y_a = _gmlp(proj, w_s[i], b_s[i].T, ln_v_g[i][None], ln_v_b[i][None], g_out_a[i][None])
        state = None
        for dil in sorted((dil for _, dil in DILATED), reverse=True):
            state = _attn_pass(dil, qkv_rm, proj, rel_bias, state, g_out_b[i][None],
                               batch=batch, seq=seq, w_b=w_b)
        y_b = state
        h = _out_proj(x2, y_a, y_b, w_out[i].astype(BF16))
        x2 = _ple(h, g_ple[i][None], w_ple_gate[i].astype(BF16), p[i].reshape(m, -1), w_ple_up[i].astype(BF16))
    return x2.reshape(batch, seq, d)
```

```python
import functools
import math

import numpy as np
import jax
import jax.numpy as jnp
from jax import lax
from jax.experimental import pallas as pl
from jax.experimental.pallas import tpu as pltpu

HEAD_DIM = 128
CHUNK = 128
BLK = 128
QK_LOOKAHEAD = 3
DILATED = ((128, 1), (512, 4), (2048, 16))
MAX_DIL = 16
NUM_BUCKETS = 32
MAX_DISTANCE = 2048
EPS = 1e-6
NEG_INF = -1e30
N_SEG = 7
SEG_Q, SEG_K, SEG_V, SEG_BZ = 3, 4, 5, 6

V7X_VMEM_LIMIT_BYTES = 56 * 1024 * 1024
MXU_COLS = 256
SLAB_PITCH = 24

BF16 = jnp.bfloat16
F32 = jnp.float32


def _pick(n, pref):
    t = min(n, pref)
    while n % t:
        t //= 2
    return t


def _params(sem):
    return pltpu.CompilerParams(dimension_semantics=sem,
                                vmem_limit_bytes=V7X_VMEM_LIMIT_BYTES)


def _gelu(v):
    return 0.5 * v * (1.0 + lax.erf(v * (1.0 / math.sqrt(2.0))))


def _silu(v):
    return v * jax.nn.sigmoid(v)


def _normalise_rows(x_ref, g_ref, hn_ref, rows):
    tm = x_ref.shape[0]

    def body(c, carry):
        r = pl.ds(pl.multiple_of(c * rows, rows), rows)
        xc = x_ref[r, :]
        ms = jnp.mean(xc * xc, axis=-1, keepdims=True)
        hn_ref[r, :] = (xc * lax.rsqrt(ms + EPS) * g_ref[...]).astype(BF16)
        return carry

    lax.fori_loop(0, tm // rows, body, 0, unroll=4)


def _in_proj_kernel(x_ref, g_ref, w_ref, gq_ref, gk_ref, o_ref, op_ref, hn_ref, slab_ref, *, tiles_per_seg):
    j = pl.program_id(1)

    @pl.when(j == 0)
    def _():
        _normalise_rows(x_ref, g_ref, hn_ref, 16)

    seg = j // tiles_per_seg
    tm, tn = o_ref.shape
    n_sub = tn // MXU_COLS

    def sub_cols(c):
        return slice(c * MXU_COLS, (c + 1) * MXU_COLS)

    def product(c):
        return jnp.dot(hn_ref[...], w_ref[:, sub_cols(c)], preferred_element_type=F32)

    def tile(epilogue, residue_major=False):
        acc = product(0)
        for c in range(n_sub):
            nxt = product(c + 1) if c + 1 < n_sub else None
            lanes = MXU_COLS // HEAD_DIM
            y = epilogue(acc)
            o_ref[:, sub_cols(c)] = y.astype(BF16)
            if residue_major:
                for k in range(lanes):
                    for a in range(tm // MAX_DIL):
                        slab_ref[c * lanes + k, a * SLAB_PITCH:a * SLAB_PITCH + MAX_DIL, :] = (
                            y[a * MAX_DIL:(a + 1) * MAX_DIL, k * HEAD_DIM:(k + 1) * HEAD_DIM])
                for r in range(MAX_DIL):
                    rows = [slab_ref[c * lanes + k, pl.ds(r, tm // MAX_DIL, stride=SLAB_PITCH), :]
                            for k in range(lanes)]
                    op_ref[0, r, :, sub_cols(c)] = jnp.concatenate(rows, axis=1).astype(BF16)
            acc = nxt

    def head_norm(g, scale):
        def epilogue(acc):
            out = []
            for c in range(MXU_COLS // HEAD_DIM):
                blk = acc[:, c * HEAD_DIM:(c + 1) * HEAD_DIM]
                ms = jnp.mean(blk * blk, axis=-1, keepdims=True)
                y = blk * lax.rsqrt(ms + EPS) * g
                out.append(y if scale is None else y * scale)
            return jnp.concatenate(out, axis=1)
        return epilogue

    @pl.when(seg <= 1)
    def _():
        tile(_gelu)

    @pl.when((seg == 2) | (seg == SEG_BZ))
    def _():
        tile(_silu)

    @pl.when(seg == SEG_Q)
    def _():
        tile(head_norm(gq_ref[...], HEAD_DIM ** -0.5), residue_major=True)

    @pl.when(seg == SEG_K)
    def _():
        tile(head_norm(gk_ref[...], None), residue_major=True)

    @pl.when(seg == SEG_V)
    def _():
        tile(lambda acc: acc, residue_major=True)


def _in_proj(x2, g_pre, w_in_bf, g_q, g_k, *, batch, seq):
    m, d = x2.shape
    d_in = w_in_bf.shape[1]
    seg_w = d_in // N_SEG
    tm = _pick(seq, 512)
    tn = _pick(seg_w, 1024)
    assert tn % MXU_COLS == 0
    tps = seg_w // tn
    assert tm % (MAX_DIL * 16) == 0, "residue-major runs must cover whole bf16 sublane tiles"
    blocks_per_seq = seq // tm
    sub = seq // MAX_DIL
    runs = tm // MAX_DIL
    kern = functools.partial(_in_proj_kernel, tiles_per_seg=tps)

    def perm_idx(i, j):
        jj = jnp.clip(j - SEG_Q * tps, 0, 3 * tps - 1)
        return (i // blocks_per_seq, 0, i % blocks_per_seq, jj)

    return pl.pallas_call(
        kern,
        grid=(m // tm, d_in // tn),
        in_specs=[
            pl.BlockSpec((tm, d), lambda i, j: (i, 0)),
            pl.BlockSpec((1, d), lambda i, j: (0, 0)),
            pl.BlockSpec((d, tn), lambda i, j: (0, j)),
            pl.BlockSpec((1, HEAD_DIM), lambda i, j: (0, 0)),
            pl.BlockSpec((1, HEAD_DIM), lambda i, j: (0, 0)),
        ],
        out_specs=[
            pl.BlockSpec((tm, tn), lambda i, j: (i, j)),
            pl.BlockSpec((1, MAX_DIL, runs, tn), perm_idx),
        ],
        out_shape=[
            jax.ShapeDtypeStruct((m, d_in), BF16),
            jax.ShapeDtypeStruct((batch, MAX_DIL, sub, 3 * seg_w), BF16),
        ],
        scratch_shapes=[pltpu.VMEM((tm, d), BF16), pltpu.VMEM((tn // HEAD_DIM, runs * SLAB_PITCH, HEAD_DIM), F32)],
        compiler_params=_params(("arbitrary", "arbitrary")),
        name="in_proj",
    )(x2, g_pre, w_in_bf, g_q, g_k)


def _gmlp_kernel(au_ref, av_ref, az_ref, ws_ref, bs_ref, lg_ref, lb_ref, go_ref, o_ref, ya_ref):
    av = av_ref[...].astype(F32)
    mu = jnp.mean(av, axis=-1, keepdims=True)
    xc = av - mu
    var = jnp.mean(xc * xc, axis=-1, keepdims=True)
    avn = xc * lax.rsqrt(var + EPS) * lg_ref[...] + lb_ref[...]

    row = lax.broadcasted_iota(jnp.int32, (CHUNK, CHUNK), 0)
    col = lax.broadcasted_iota(jnp.int32, (CHUNK, CHUNK), 1)
    causal = col <= row
    n_groups = ws_ref.shape[0]
    for g in range(n_groups):
        sl = slice(g * HEAD_DIM, (g + 1) * HEAD_DIM)
        wm = jnp.where(causal, ws_ref[g], 0.0).astype(BF16)
        z = jnp.dot(wm, avn[:, sl].astype(BF16), preferred_element_type=F32)
        z = z + bs_ref[:, g:g + 1]
        ya_ref[:, sl] = au_ref[:, sl].astype(F32) * z
    ya = ya_ref[...]
    ms = jnp.mean(ya * ya, axis=-1, keepdims=True)
    o_ref[...] = (ya * lax.rsqrt(ms + EPS) * go_ref[...] * az_ref[...].astype(F32)).astype(BF16)


def _gmlp(proj, w_s, b_s_t, ln_g, ln_b, g_out_a):
    m, d_in = proj.shape
    w_a = d_in // N_SEG
    n_groups = w_s.shape[0]
    vec = lambda: pl.BlockSpec((1, w_a), lambda i: (0, 0))
    return pl.pallas_call(
        _gmlp_kernel,
        grid=(m // CHUNK,),
        in_specs=[
            pl.BlockSpec((CHUNK, w_a), lambda i: (i, 0)),
            pl.BlockSpec((CHUNK, w_a), lambda i: (i, 1)),
            pl.BlockSpec((CHUNK, w_a), lambda i: (i, 2)),
            pl.BlockSpec((n_groups, CHUNK, CHUNK), lambda i: (0, 0, 0)),
            pl.BlockSpec((CHUNK, n_groups), lambda i: (0, 0)),
            vec(), vec(), vec(),
        ],
        out_specs=pl.BlockSpec((CHUNK, w_a), lambda i: (i, 0)),
        out_shape=jax.ShapeDtypeStruct((m, w_a), BF16),
        scratch_shapes=[pltpu.VMEM((CHUNK, w_a), F32)],
        compiler_params=_params(("parallel",)),
        name="gmlp",
    )(proj, proj, proj, w_s, b_s_t, ln_g, ln_b, g_out_a)


def _rel_bucket_np(dist):
    max_exact = NUM_BUCKETS // 2
    d = np.maximum(dist, 1).astype(np.float32)
    large = max_exact + (np.log(d / np.float32(max_exact)) / np.float32(math.log(MAX_DISTANCE / max_exact))
                         * np.float32(NUM_BUCKETS - max_exact)).astype(np.int32)
    large = np.minimum(large, NUM_BUCKETS - 1)
    return np.where(dist < max_exact, dist, large).astype(np.int32)


def _band_tables(dil, pos):
    i_q = pos[:, None]
    i_k = pos[None, :]
    in_cur = i_k <= i_q
    delta = np.where(in_cur, i_q - i_k, BLK + i_q - i_k)
    return in_cur.astype(np.int32), _rel_bucket_np(delta * dil)


class _TileIO:
    def __init__(self, idx, rows_shape):
        self.idx = idx
        self.rows_shape = rows_shape

    def load(self, ref, sl):
        return ref[self.idx + (sl,)].reshape(BLK, sl.stop - sl.start)

    def store(self, ref, sl, val):
        ref[self.idx + (sl,)] = val.reshape(self.rows_shape + (sl.stop - sl.start,))


def _attn_kernel(*refs, n_heads, first, last, diag_bucket, io, io_state):
    it = iter(refs)
    cur_ref, bucket_ref, relb_ref = next(it), next(it), next(it)
    q_ref, kc_ref, vc_ref = next(it), next(it), next(it)
    if not first:
        o_in_ref, lse_in_ref = next(it), next(it)
    if last:
        bz_ref, go_ref = next(it), next(it)
        y_ref = next(it)
    else:
        o_out_ref, lse_out_ref = next(it), next(it)
    bias_ref, kp_ref, vp_ref = next(it), next(it), next(it)
    if last:
        yb_ref, st_ref = next(it), next(it)
    else:
        lse_ref = next(it)

    n = pl.program_id(2)
    first_step = (pl.program_id(0) == 0) & (pl.program_id(1) == 0) & (n == 0)
    prev_io = _TileIO((slice(None),), (BLK,))

    @pl.when(first_step)
    def _():
        bk = bucket_ref[...]
        for h in range(n_heads):
            tab = jnp.zeros((BLK, BLK), F32)
            for b in range(NUM_BUCKETS):
                tab = jnp.where(bk == b, relb_ref[b, h], tab)
            bias_ref[h] = tab
        if not last:
            lse_ref[...] = jnp.zeros((BLK, BLK), F32)

    def natural_rows(tile):
        st_ref[...] = tile
        per = BLK // MAX_DIL
        rows = [st_ref[pl.ds((MAX_DIL // 2) * per * (v % 2) + v // 2, 8, stride=per), :]
                for v in range(BLK // 8)]
        return jnp.concatenate(rows, axis=0)

    nt = (((1,), (1,)), ((), ()))

    def heads(has_prev):
        in_cur = cur_ref[...] != 0
        ones = jnp.ones((BLK, HEAD_DIM), BF16)
        if has_prev:
            eye = (lax.broadcasted_iota(jnp.int32, (BLK, BLK), 0)
                   == lax.broadcasted_iota(jnp.int32, (BLK, BLK), 1))
        if not first:
            lse_in = io_state.load(lse_in_ref, slice(0, BLK))
            if last:
                lse_in = natural_rows(lse_in)

        def logits(h):
            sl = slice(h * HEAD_DIM, (h + 1) * HEAD_DIM)
            keys = io.load(kc_ref, sl)
            if has_prev:
                keys = jnp.concatenate([prev_io.load(kp_ref, sl), keys], axis=0)
            return lax.dot_general(io.load(q_ref, sl), keys, nt, preferred_element_type=F32)

        def softmax(h, s2):
            if has_prev:
                s_p, s_c = s2[:, :BLK], s2[:, BLK:]
                s = jnp.where(in_cur, s_c, s_p) + bias_ref[h]
                far = jnp.where(eye, s_p + relb_ref[diag_bucket, h], NEG_INF)
                mx = jnp.max(jnp.maximum(s, far), axis=-1, keepdims=True)
            else:
                s = jnp.where(in_cur, s2 + bias_ref[h], NEG_INF)
                mx = jnp.max(s, axis=-1, keepdims=True)
            lse_old = None
            if not first:
                lse_old = jnp.broadcast_to(lse_in[:, h:h + 1], (BLK, HEAD_DIM))
                mx = jnp.maximum(mx, lse_old)
            e = jnp.exp(s - mx)
            if has_prev:
                e_far = jnp.exp(far - mx)
                probs = jnp.concatenate([jnp.where(in_cur, e_far, e), jnp.where(in_cur, e, 0.0)], axis=1)
            else:
                probs = e
            return probs.astype(BF16), mx, lse_old

        def values(h, probs, mx, lse_old):
            sl = slice(h * HEAD_DIM, (h + 1) * HEAD_DIM)
            vals = jnp.concatenate([io.load(vc_ref, sl), ones], axis=1)
            if has_prev:
                vals = jnp.concatenate([jnp.concatenate([prev_io.load(vp_ref, sl), ones], axis=1), vals], axis=0)
            both = jnp.dot(probs, vals, preferred_element_type=F32)
            num, den = both[:, :HEAD_DIM], both[:, HEAD_DIM:]
            if first:
                total = den
                o = num / total
            else:
                o_old = io_state.load(o_in_ref, sl)
                if last:
                    o_old = natural_rows(o_old)
                w_old = jnp.exp(lse_old - mx)
                total = w_old + den
                o = (o_old * w_old + num) / total
            if last:
                yb_ref[:, sl] = o
            else:
                io_state.store(o_out_ref, sl, o)
                lse = mx + jnp.log(total)
                lse_ref[:, h:h + 1] = lse[:, h:h + 1]

        ahead = {h: logits(h) for h in range(min(QK_LOOKAHEAD, n_heads))}
        soft = softmax(0, ahead.pop(0))
        for h in range(n_heads):
            if h + QK_LOOKAHEAD < n_heads:
                ahead[h + QK_LOOKAHEAD] = logits(h + QK_LOOKAHEAD)
            nxt = softmax(h + 1, ahead.pop(h + 1)) if h + 1 < n_heads else None
            values(h, *soft)
            soft = nxt

    @pl.when(n == 0)
    def _():
        heads(False)

    @pl.when(n > 0)
    def _():
        heads(True)

    whole = slice(0, kp_ref.shape[1])
    kp_ref[...] = io.load(kc_ref, whole)
    vp_ref[...] = io.load(vc_ref, whole)
    if last:
        yb = yb_ref[...]
        ms = jnp.mean(yb * yb, axis=-1, keepdims=True)
        y_ref[...] = (yb * lax.rsqrt(ms + EPS) * go_ref[...] * bz_ref[...].astype(F32)).astype(BF16)
    else:
        io_state.store(lse_out_ref, slice(0, BLK), lse_ref[...])


def _attn_pass(dil, qkv_rm, proj, rel_bias, state, g_out_b, *, batch, seq, w_b):
    first = state is None
    last = dil == 1
    n_heads = w_b // HEAD_DIM
    assert 2 * n_heads <= BLK
    sub = seq // MAX_DIL
    nb = seq // dil // BLK
    rep = MAX_DIL // dil
    runs = BLK // rep
    col_q, col_k, col_v = 0, 1, 2

    if last:
        pos = np.arange(BLK)
        io = _TileIO((slice(None),), (BLK,))
        io_state = _TileIO((0, slice(None), slice(None)), (MAX_DIL, BLK // MAX_DIL))
        blk = (BLK, w_b)
        src = proj
        cur = lambda seg: (lambda b, r, n: (b * nb + n, seg))
        col_q, col_k, col_v = SEG_Q, SEG_K, SEG_V
        st_blk = lambda w: (1, MAX_DIL, BLK // MAX_DIL, w)
        st_idx = lambda b, r, n: (b, 0, n, 0)
        view = lambda a: a
    else:
        rho = np.arange(BLK)
        pos = rep * (rho % runs) + rho // runs
        io = io_state = _TileIO((0, slice(None), 0, slice(None)), (rep, runs))
        blk = (1, rep, 1, runs, w_b)
        src = qkv_rm.reshape(batch, rep, dil, sub, 3 * w_b)
        cur = lambda col: (lambda b, r, n: (b, 0, r, n, col))
        st_blk = lambda w: (1, rep, 1, runs, w)
        st_idx = lambda b, r, n: (b, 0, r, n, 0)
        view = lambda a: a.reshape(batch, rep, dil, sub, a.shape[-1])

    in_cur, bucket = _band_tables(dil, pos)
    diag_bucket = int(_rel_bucket_np(np.array([BLK * dil]))[0])
    const = lambda: pl.BlockSpec((BLK, BLK), lambda b, r, n: (0, 0))
    in_specs = [
        const(), const(), pl.BlockSpec(memory_space=pltpu.SMEM),
        pl.BlockSpec(blk, cur(col_q)), pl.BlockSpec(blk, cur(col_k)), pl.BlockSpec(blk, cur(col_v)),
    ]
    args = [jnp.asarray(in_cur), jnp.asarray(bucket), rel_bias, src, src, src]
    acc_spec = pl.BlockSpec(st_blk(w_b), st_idx)
    ml_spec = pl.BlockSpec(st_blk(BLK), st_idx)
    if not first:
        acc, ml = state
        in_specs += [acc_spec, ml_spec]
        args += [view(acc), view(ml)]
    scratch = [pltpu.VMEM((n_heads, BLK, BLK), F32),
               pltpu.VMEM((BLK, w_b), BF16), pltpu.VMEM((BLK, w_b), BF16)]
    if last:
        in_specs += [pl.BlockSpec(blk, cur(SEG_BZ)), pl.BlockSpec((1, w_b), lambda b, r, n: (0, 0))]
        args += [proj, g_out_b]
        out_specs = pl.BlockSpec(blk, cur(0))
        out_shape = jax.ShapeDtypeStruct((batch * seq, w_b), BF16)
        scratch += [pltpu.VMEM((BLK, w_b), F32), pltpu.VMEM((BLK, HEAD_DIM), F32)]
    else:
        out_specs = [acc_spec, ml_spec]
        scratch.append(pltpu.VMEM((BLK, BLK), F32))
        out_shape = [jax.ShapeDtypeStruct((batch, rep, dil, sub, w_b), F32),
                     jax.ShapeDtypeStruct((batch, rep, dil, sub, BLK), F32)]
    kern = functools.partial(_attn_kernel, n_heads=n_heads, first=first, last=last,
                             diag_bucket=diag_bucket, io=io, io_state=io_state)
    out = pl.pallas_call(
        kern,
        grid=(batch, dil, nb),
        in_specs=in_specs,
        out_specs=out_specs,
        out_shape=out_shape,
        scratch_shapes=scratch,
        compiler_params=_params(("arbitrary", "arbitrary", "arbitrary")),
        name=f"attn_d{dil}",
    )(*args)
    if last:
        return out
    acc, ml = out
    return (acc.reshape(batch, MAX_DIL, sub, w_b), ml.reshape(batch, MAX_DIL, sub, BLK))


def _out_proj_kernel(x_ref, ya_ref, yb_ref, wa_ref, wb_ref, h_ref):
    def product(c):
        cols = slice(c * MXU_COLS, (c + 1) * MXU_COLS)
        return (jnp.dot(ya_ref[...], wa_ref[:, cols], preferred_element_type=F32)
                + jnp.dot(yb_ref[...], wb_ref[:, cols], preferred_element_type=F32))

    n_sub = h_ref.shape[1] // MXU_COLS
    acc = product(0)
    for c in range(n_sub):
        nxt = product(c + 1) if c + 1 < n_sub else None
        cols = slice(c * MXU_COLS, (c + 1) * MXU_COLS)
        h_ref[:, cols] = x_ref[:, cols] + acc
        acc = nxt


def _out_proj(x2, y_a, y_b, w_out_bf):
    m, d = x2.shape
    w_a = y_a.shape[1]
    w_b = y_b.shape[1]
    assert w_a == w_b
    tm = _pick(m, 1024)
    tn = _pick(d, 512)
    assert tn % MXU_COLS == 0
    return pl.pallas_call(
        _out_proj_kernel,
        grid=(m // tm, d // tn),
        in_specs=[
            pl.BlockSpec((tm, tn), lambda i, j: (i, j)),
            pl.BlockSpec((tm, w_a), lambda i, j: (i, 0)),
            pl.BlockSpec((tm, w_b), lambda i, j: (i, 0)),
            pl.BlockSpec((w_a, tn), lambda i, j: (0, j)),
            pl.BlockSpec((w_b, tn), lambda i, j: (1, j)),
        ],
        out_specs=pl.BlockSpec((tm, tn), lambda i, j: (i, j)),
        out_shape=jax.ShapeDtypeStruct((m, d), F32),
        compiler_params=_params(("parallel", "arbitrary")),
        name="out_proj",
    )(x2, y_a, y_b, w_out_bf, w_out_bf)


def _ple_kernel(h_ref, g_ref, wg_ref, p_ref, wu_ref, o_ref, hn_ref):
    j = pl.program_id(1)

    @pl.when(j == 0)
    def _():
        _normalise_rows(h_ref, g_ref, hn_ref, 16)

    tn = o_ref.shape[1]
    n_sub = tn // MXU_COLS
    p_bf = p_ref[...].astype(BF16)

    def product(c):
        return jnp.dot(hn_ref[...], wg_ref[:, c * MXU_COLS:(c + 1) * MXU_COLS], preferred_element_type=F32)

    acc = product(0)
    for c in range(n_sub):
        nxt = product(c + 1) if c + 1 < n_sub else None
        cols = slice(c * MXU_COLS, (c + 1) * MXU_COLS)
        up = jnp.dot(p_bf, wu_ref[:, cols], preferred_element_type=F32)
        h_cols = pl.ds(pl.multiple_of(j * tn + c * MXU_COLS, MXU_COLS), MXU_COLS)
        o_ref[:, cols] = h_ref[:, h_cols] + jax.nn.sigmoid(acc) * up
        acc = nxt


def _ple(h, g_ple, w_gate_bf, p2, w_up_bf):
    m, d = h.shape
    d_ple = p2.shape[1]
    tm = _pick(m, 512)
    tn = _pick(d, 1024)
    assert tn % MXU_COLS == 0
    return pl.pallas_call(
        _ple_kernel,
        grid=(m // tm, d // tn),
        in_specs=[
            pl.BlockSpec((tm, d), lambda i, j: (i, 0)),
            pl.BlockSpec((1, d), lambda i, j: (0, 0)),
            pl.BlockSpec((d, tn), lambda i, j: (0, j)),
            pl.BlockSpec((tm, d_ple), lambda i, j: (i, 0)),
            pl.BlockSpec((d_ple, tn), lambda i, j: (0, j)),
        ],
        out_specs=pl.BlockSpec((tm, tn), lambda i, j: (i, j)),
        out_shape=jax.ShapeDtypeStruct((m, d), F32),
        scratch_shapes=[pltpu.VMEM((tm, d), BF16)],
        compiler_params=_params(("parallel", "arbitrary")),
        name="ple",
    )(h, g_ple, w_gate_bf, p2, w_up_bf)


def kernel(x, p, g_pre, w_in, w_s, b_s, ln_v_g, ln_v_b, g_q, g_k, rel_bias, g_out_a, g_out_b, w_out, g_ple, w_ple_gate, w_ple_up):
    batch, seq, d = x.shape
    depth = p.shape[0]
    w_a = ln_v_g.shape[-1]
    w_b = g_out_b.shape[-1]
    d_in = w_in.shape[-1]
    assert w_a == w_b and d_in == N_SEG * w_a, "segments of the combined projection must be equally wide"
    assert seq % (MAX_DIL * BLK) == 0 and all(win // dil == BLK for win, dil in DILATED)
    assert sorted(dil for _, dil in DILATED) == [1, 4, MAX_DIL]
    m = batch * seq
    x2 = x.reshape(m, d)
    for i in range(depth):
        proj, qkv_rm = _in_proj(x2, g_pre[i][None], w_in[i].astype(BF16), g_q[i][None], g_k[i][None],
                                batch=batch, seq=seq)
        y_a = _gmlp(proj, w_s[i], b_s[i].T, ln_v_g[i][None], ln_v_b[i][None], g_out_a[i][None])
        state = None
        for dil in sorted((dil for _, dil in DILATED), reverse=True):
            state = _attn_pass(dil, qkv_rm, proj, rel_bias, state, g_out_b[i][None],
                               batch=batch, seq=seq, w_b=w_b)
        y_b = state
        h = _out_proj(x2, y_a, y_b, w_out[i].astype(BF16))
        x2 = _ple(h, g_ple[i][None], w_ple_gate[i].astype(BF16), p[i].reshape(m, -1), w_ple_up[i].astype(BF16))
    return x2.reshape(batch, seq, d)
```

```python
import functools
import math

import numpy as np
import jax
import jax.numpy as jnp
from jax import lax
from jax.experimental import pallas as pl
from jax.experimental.pallas import tpu as pltpu

HEAD_DIM = 128
CHUNK = 128
BLK = 128
QK_LOOKAHEAD = 3
ATTN_BLOCKS_PER_STEP = 2
DILATED = ((128, 1), (512, 4), (2048, 16))
MAX_DIL = 16
NUM_BUCKETS = 32
MAX_DISTANCE = 2048
EPS = 1e-6
NEG_INF = -1e30
N_SEG = 7
SEG_Q, SEG_K, SEG_V, SEG_BZ = 3, 4, 5, 6

V7X_VMEM_LIMIT_BYTES = 56 * 1024 * 1024
MXU_COLS = 256
SLAB_PITCH = 24

BF16 = jnp.bfloat16
F32 = jnp.float32


def _pick(n, pref):
    t = min(n, pref)
    while n % t:
        t //= 2
    return t


def _params(sem):
    return pltpu.CompilerParams(dimension_semantics=sem,
                                vmem_limit_bytes=V7X_VMEM_LIMIT_BYTES)


def _gelu(v):
    return 0.5 * v * (1.0 + lax.erf(v * (1.0 / math.sqrt(2.0))))


def _silu(v):
    return v * jax.nn.sigmoid(v)


def _normalise_rows(x_ref, g_ref, hn_ref, rows):
    tm = x_ref.shape[0]

    def body(c, carry):
        r = pl.ds(pl.multiple_of(c * rows, rows), rows)
        xc = x_ref[r, :]
        ms = jnp.mean(xc * xc, axis=-1, keepdims=True)
        hn_ref[r, :] = (xc * lax.rsqrt(ms + EPS) * g_ref[...]).astype(BF16)
        return carry

    lax.fori_loop(0, tm // rows, body, 0, unroll=4)


def _in_proj_kernel(x_ref, g_ref, w_ref, gq_ref, gk_ref, o_ref, op_ref, hn_ref, slab_ref, *, tiles_per_seg):
    j = pl.program_id(1)

    @pl.when(j == 0)
    def _():
        _normalise_rows(x_ref, g_ref, hn_ref, 16)

    seg = j // tiles_per_seg
    tm, tn = o_ref.shape
    n_sub = tn // MXU_COLS

    def sub_cols(c):
        return slice(c * MXU_COLS, (c + 1) * MXU_COLS)

    def product(c):
        return jnp.dot(hn_ref[...], w_ref[:, sub_cols(c)], preferred_element_type=F32)

    def tile(epilogue, residue_major=False):
        acc = product(0)
        for c in range(n_sub):
            nxt = product(c + 1) if c + 1 < n_sub else None
            lanes = MXU_COLS // HEAD_DIM
            y = epilogue(acc)
            o_ref[:, sub_cols(c)] = y.astype(BF16)
            if residue_major:
                for k in range(lanes):
                    for a in range(tm // MAX_DIL):
                        slab_ref[c * lanes + k, a * SLAB_PITCH:a * SLAB_PITCH + MAX_DIL, :] = (
                            y[a * MAX_DIL:(a + 1) * MAX_DIL, k * HEAD_DIM:(k + 1) * HEAD_DIM])
                for r in range(MAX_DIL):
                    rows = [slab_ref[c * lanes + k, pl.ds(r, tm // MAX_DIL, stride=SLAB_PITCH), :]
                            for k in range(lanes)]
                    op_ref[0, r, :, sub_cols(c)] = jnp.concatenate(rows, axis=1).astype(BF16)
            acc = nxt

    def head_norm(g, scale):
        def epilogue(acc):
            out = []
            for c in range(MXU_COLS // HEAD_DIM):
                blk = acc[:, c * HEAD_DIM:(c + 1) * HEAD_DIM]
                ms = jnp.mean(blk * blk, axis=-1, keepdims=True)
                y = blk * lax.rsqrt(ms + EPS) * g
                out.append(y if scale is None else y * scale)
            return jnp.concatenate(out, axis=1)
        return epilogue

    @pl.when(seg <= 1)
    def _():
        tile(_gelu)

    @pl.when((seg == 2) | (seg == SEG_BZ))
    def _():
        tile(_silu)

    @pl.when(seg == SEG_Q)
    def _():
        tile(head_norm(gq_ref[...], HEAD_DIM ** -0.5), residue_major=True)

    @pl.when(seg == SEG_K)
    def _():
        tile(head_norm(gk_ref[...], None), residue_major=True)

    @pl.when(seg == SEG_V)
    def _():
        tile(lambda acc: acc, residue_major=True)


def _in_proj(x2, g_pre, w_in_bf, g_q, g_k, *, batch, seq):
    m, d = x2.shape
    d_in = w_in_bf.shape[1]
    seg_w = d_in // N_SEG
    tm = _pick(seq, 512)
    tn = _pick(seg_w, 1024)
    assert tn % MXU_COLS == 0
    tps = seg_w // tn
    assert tm % (MAX_DIL * 16) == 0, "residue-major runs must cover whole bf16 sublane tiles"
    blocks_per_seq = seq // tm
    sub = seq // MAX_DIL
    runs = tm // MAX_DIL
    kern = functools.partial(_in_proj_kernel, tiles_per_seg=tps)

    def perm_idx(i, j):
        jj = jnp.clip(j - SEG_Q * tps, 0, 3 * tps - 1)
        return (i // blocks_per_seq, 0, i % blocks_per_seq, jj)

    return pl.pallas_call(
        kern,
        grid=(m // tm, d_in // tn),
        in_specs=[
            pl.BlockSpec((tm, d), lambda i, j: (i, 0)),
            pl.BlockSpec((1, d), lambda i, j: (0, 0)),
            pl.BlockSpec((d, tn), lambda i, j: (0, j)),
            pl.BlockSpec((1, HEAD_DIM), lambda i, j: (0, 0)),
            pl.BlockSpec((1, HEAD_DIM), lambda i, j: (0, 0)),
        ],
        out_specs=[
            pl.BlockSpec((tm, tn), lambda i, j: (i, j)),
            pl.BlockSpec((1, MAX_DIL, runs, tn), perm_idx),
        ],
        out_shape=[
            jax.ShapeDtypeStruct((m, d_in), BF16),
            jax.ShapeDtypeStruct((batch, MAX_DIL, sub, 3 * seg_w), BF16),
        ],
        scratch_shapes=[pltpu.VMEM((tm, d), BF16), pltpu.VMEM((tn // HEAD_DIM, runs * SLAB_PITCH, HEAD_DIM), F32)],
        compiler_params=_params(("arbitrary", "arbitrary")),
        name="in_proj",
    )(x2, g_pre, w_in_bf, g_q, g_k)


def _gmlp_kernel(au_ref, av_ref, az_ref, ws_ref, bs_ref, lg_ref, lb_ref, go_ref, o_ref, ya_ref):
    av = av_ref[...].astype(F32)
    mu = jnp.mean(av, axis=-1, keepdims=True)
    xc = av - mu
    var = jnp.mean(xc * xc, axis=-1, keepdims=True)
    avn = xc * lax.rsqrt(var + EPS) * lg_ref[...] + lb_ref[...]

    row = lax.broadcasted_iota(jnp.int32, (CHUNK, CHUNK), 0)
    col = lax.broadcasted_iota(jnp.int32, (CHUNK, CHUNK), 1)
    causal = col <= row
    n_groups = ws_ref.shape[0]
    for g in range(n_groups):
        sl = slice(g * HEAD_DIM, (g + 1) * HEAD_DIM)
        wm = jnp.where(causal, ws_ref[g], 0.0).astype(BF16)
        z = jnp.dot(wm, avn[:, sl].astype(BF16), preferred_element_type=F32)
        z = z + bs_ref[:, g:g + 1]
        ya_ref[:, sl] = au_ref[:, sl].astype(F32) * z
    ya = ya_ref[...]
    ms = jnp.mean(ya * ya, axis=-1, keepdims=True)
    o_ref[...] = (ya * lax.rsqrt(ms + EPS) * go_ref[...] * az_ref[...].astype(F32)).astype(BF16)


def _gmlp(proj, w_s, b_s_t, ln_g, ln_b, g_out_a):
    m, d_in = proj.shape
    w_a = d_in // N_SEG
    n_groups = w_s.shape[0]
    vec = lambda: pl.BlockSpec((1, w_a), lambda i: (0, 0))
    return pl.pallas_call(
        _gmlp_kernel,
        grid=(m // CHUNK,),
        in_specs=[
            pl.BlockSpec((CHUNK, w_a), lambda i: (i, 0)),
            pl.BlockSpec((CHUNK, w_a), lambda i: (i, 1)),
            pl.BlockSpec((CHUNK, w_a), lambda i: (i, 2)),
            pl.BlockSpec((n_groups, CHUNK, CHUNK), lambda i: (0, 0, 0)),
            pl.BlockSpec((CHUNK, n_groups), lambda i: (0, 0)),
            vec(), vec(), vec(),
        ],
        out_specs=pl.BlockSpec((CHUNK, w_a), lambda i: (i, 0)),
        out_shape=jax.ShapeDtypeStruct((m, w_a), BF16),
        scratch_shapes=[pltpu.VMEM((CHUNK, w_a), F32)],
        compiler_params=_params(("parallel",)),
        name="gmlp",
    )(proj, proj, proj, w_s, b_s_t, ln_g, ln_b, g_out_a)


def _rel_bucket_np(dist):
    max_exact = NUM_BUCKETS // 2
    d = np.maximum(dist, 1).astype(np.float32)
    large = max_exact + (np.log(d / np.float32(max_exact)) / np.float32(math.log(MAX_DISTANCE / max_exact))
                         * np.float32(NUM_BUCKETS - max_exact)).astype(np.int32)
    large = np.minimum(large, NUM_BUCKETS - 1)
    return np.where(dist < max_exact, dist, large).astype(np.int32)


def _band_tables(dil, pos):
    i_q = pos[:, None]
    i_k = pos[None, :]
    in_cur = i_k <= i_q
    delta = np.where(in_cur, i_q - i_k, BLK + i_q - i_k)
    return in_cur.astype(np.int32), _rel_bucket_np(delta * dil)


class _TileIO:
    def __init__(self, prefix, run, rows_shape, g=0):
        self.prefix = prefix
        self.run = run
        self.rows_shape = rows_shape
        self.idx = prefix + (slice(g * run, (g + 1) * run),)

    def sub(self, g):
        return _TileIO(self.prefix, self.run, self.rows_shape, g)

    def load(self, ref, sl):
        return ref[self.idx + (sl,)].reshape(BLK, sl.stop - sl.start)

    def store(self, ref, sl, val):
        ref[self.idx + (sl,)] = val.reshape(self.rows_shape + (sl.stop - sl.start,))


def _attn_kernel(*refs, n_heads, n_sub, first, last, diag_bucket, io, io_state):
    it = iter(refs)
    cur_ref, bucket_ref, relb_ref = next(it), next(it), next(it)
    q_ref, kc_ref, vc_ref = next(it), next(it), next(it)
    if not first:
        o_in_ref, lse_in_ref = next(it), next(it)
    if last:
        bz_ref, go_ref = next(it), next(it)
        y_ref = next(it)
    else:
        o_out_ref, lse_out_ref = next(it), next(it)
    bias_ref, kp_ref, vp_ref = next(it), next(it), next(it)
    if last:
        yb_ref, st_ref = next(it), next(it)
    else:
        lse_ref = next(it)

    n = pl.program_id(2)
    first_step = (pl.program_id(0) == 0) & (pl.program_id(1) == 0) & (n == 0)
    prev_io = _TileIO((), BLK, (BLK,))

    @pl.when(first_step)
    def _():
        bk = bucket_ref[...]
        for h in range(n_heads):
            tab = jnp.zeros((BLK, BLK), F32)
            for b in range(NUM_BUCKETS):
                tab = jnp.where(bk == b, relb_ref[b, h], tab)
            bias_ref[h] = tab
        if not last:
            lse_ref[...] = jnp.zeros(lse_ref.shape, F32)

    def natural_rows(tile):
        st_ref[...] = tile
        per = BLK // MAX_DIL
        rows = [st_ref[pl.ds((MAX_DIL // 2) * per * (v % 2) + v // 2, 8, stride=per), :]
                for v in range(BLK // 8)]
        return jnp.concatenate(rows, axis=0)

    nt = (((1,), (1,)), ((), ()))

    def heads(g, has_prev):
        cur_io, st_io = io.sub(g), io_state.sub(g)
        kp_io, kp_src, vp_src = (io.sub(g - 1), kc_ref, vc_ref) if g else (prev_io, kp_ref, vp_ref)
        in_cur = cur_ref[...] != 0
        ones = jnp.ones((BLK, HEAD_DIM), BF16)
        if has_prev:
            eye = (lax.broadcasted_iota(jnp.int32, (BLK, BLK), 0)
                   == lax.broadcasted_iota(jnp.int32, (BLK, BLK), 1))
        if not first:
            lse_in = st_io.load(lse_in_ref, slice(0, BLK))
            if last:
                lse_in = natural_rows(lse_in)

        def logits(h):
            sl = slice(h * HEAD_DIM, (h + 1) * HEAD_DIM)
            keys = cur_io.load(kc_ref, sl)
            if has_prev:
                keys = jnp.concatenate([kp_io.load(kp_src, sl), keys], axis=0)
            return lax.dot_general(cur_io.load(q_ref, sl), keys, nt, preferred_element_type=F32)

        def softmax(h, s2):
            if has_prev:
                s_p, s_c = s2[:, :BLK], s2[:, BLK:]
                s = jnp.where(in_cur, s_c, s_p) + bias_ref[h]
                far = jnp.where(eye, s_p + relb_ref[diag_bucket, h], NEG_INF)
                mx = jnp.max(jnp.maximum(s, far), axis=-1, keepdims=True)
            else:
                s = jnp.where(in_cur, s2 + bias_ref[h], NEG_INF)
                mx = jnp.max(s, axis=-1, keepdims=True)
            lse_old = None
            if not first:
                lse_old = jnp.broadcast_to(lse_in[:, h:h + 1], (BLK, HEAD_DIM))
                mx = jnp.maximum(mx, lse_old)
            e = jnp.exp(s - mx)
            if has_prev:
                e_far = jnp.exp(far - mx)
                probs = jnp.concatenate([jnp.where(in_cur, e_far, e), jnp.where(in_cur, e, 0.0)], axis=1)
            else:
                probs = e
            return probs.astype(BF16), mx, lse_old

        def values(h, probs, mx, lse_old):
            sl = slice(h * HEAD_DIM, (h + 1) * HEAD_DIM)
            vals = jnp.concatenate([cur_io.load(vc_ref, sl), ones], axis=1)
            if has_prev:
                vals = jnp.concatenate([jnp.concatenate([kp_io.load(vp_src, sl), ones], axis=1), vals], axis=0)
            both = jnp.dot(probs, vals, preferred_element_type=F32)
            num, den = both[:, :HEAD_DIM], both[:, HEAD_DIM:]
            if first:
                total = den
                o = num / total
            else:
                o_old = st_io.load(o_in_ref, sl)
                if last:
                    o_old = natural_rows(o_old)
                w_old = jnp.exp(lse_old - mx)
                total = w_old + den
                o = (o_old * w_old + num) / total
            if last:
                yb_ref[g * BLK:(g + 1) * BLK, sl] = o
            else:
                st_io.store(o_out_ref, sl, o)
                lse = mx + jnp.log(total)
                lse_ref[g, :, h:h + 1] = lse[:, h:h + 1]

        ahead = {h: logits(h) for h in range(min(QK_LOOKAHEAD, n_heads))}
        soft = softmax(0, ahead.pop(0))
        for h in range(n_heads):
            if h + QK_LOOKAHEAD < n_heads:
                ahead[h + QK_LOOKAHEAD] = logits(h + QK_LOOKAHEAD)
            nxt = softmax(h + 1, ahead.pop(h + 1)) if h + 1 < n_heads else None
            values(h, *soft)
            soft = nxt

    @pl.when(n == 0)
    def _():
        heads(0, False)

    @pl.when(n > 0)
    def _():
        heads(0, True)

    for g in range(1, n_sub):
        heads(g, True)

    whole = slice(0, kp_ref.shape[1])
    kp_ref[...] = io.sub(n_sub - 1).load(kc_ref, whole)
    vp_ref[...] = io.sub(n_sub - 1).load(vc_ref, whole)
    if last:
        yb = yb_ref[...]
        ms = jnp.mean(yb * yb, axis=-1, keepdims=True)
        y_ref[...] = (yb * lax.rsqrt(ms + EPS) * go_ref[...] * bz_ref[...].astype(F32)).astype(BF16)
    else:
        for g in range(n_sub):
            io_state.sub(g).store(lse_out_ref, slice(0, BLK), lse_ref[g])


def _attn_pass(dil, qkv_rm, proj, rel_bias, state, g_out_b, *, batch, seq, w_b):
    first = state is None
    last = dil == 1
    n_heads = w_b // HEAD_DIM
    assert 2 * n_heads <= BLK
    sub = seq // MAX_DIL
    nb = seq // dil // BLK
    rep = MAX_DIL // dil
    runs = BLK // rep
    n_sub = ATTN_BLOCKS_PER_STEP
    assert nb % n_sub == 0
    col_q, col_k, col_v = 0, 1, 2

    if last:
        pos = np.arange(BLK)
        io = _TileIO((), BLK, (BLK,))
        io_state = _TileIO((0, slice(None)), BLK // MAX_DIL, (MAX_DIL, BLK // MAX_DIL))
        blk = (n_sub * BLK, w_b)
        src = proj
        cur = lambda seg: (lambda b, r, n: (b * (nb // n_sub) + n, seg))
        col_q, col_k, col_v = SEG_Q, SEG_K, SEG_V
        st_blk = lambda w: (1, MAX_DIL, n_sub * BLK // MAX_DIL, w)
        st_idx = lambda b, r, n: (b, 0, n, 0)
        view = lambda a: a
    else:
        rho = np.arange(BLK)
        pos = rep * (rho % runs) + rho // runs
        io = io_state = _TileIO((0, slice(None), 0), runs, (rep, runs))
        blk = (1, rep, 1, n_sub * runs, w_b)
        src = qkv_rm.reshape(batch, rep, dil, sub, 3 * w_b)
        cur = lambda col: (lambda b, r, n: (b, 0, r, n, col))
        st_blk = lambda w: (1, rep, 1, n_sub * runs, w)
        st_idx = lambda b, r, n: (b, 0, r, n, 0)
        view = lambda a: a.reshape(batch, rep, dil, sub, a.shape[-1])

    in_cur, bucket = _band_tables(dil, pos)
    diag_bucket = int(_rel_bucket_np(np.array([BLK * dil]))[0])
    const = lambda: pl.BlockSpec((BLK, BLK), lambda b, r, n: (0, 0))
    in_specs = [
        const(), const(), pl.BlockSpec(memory_space=pltpu.SMEM),
        pl.BlockSpec(blk, cur(col_q)), pl.BlockSpec(blk, cur(col_k)), pl.BlockSpec(blk, cur(col_v)),
    ]
    args = [jnp.asarray(in_cur), jnp.asarray(bucket), rel_bias, src, src, src]
    acc_spec = pl.BlockSpec(st_blk(w_b), st_idx)
    ml_spec = pl.BlockSpec(st_blk(BLK), st_idx)
    if not first:
        acc, ml = state
        in_specs += [acc_spec, ml_spec]
        args += [view(acc), view(ml)]
    scratch = [pltpu.VMEM((n_heads, BLK, BLK), F32),
               pltpu.VMEM((BLK, w_b), BF16), pltpu.VMEM((BLK, w_b), BF16)]
    if last:
        in_specs += [pl.BlockSpec(blk, cur(SEG_BZ)), pl.BlockSpec((1, w_b), lambda b, r, n: (0, 0))]
        args += [proj, g_out_b]
        out_specs = pl.BlockSpec(blk, cur(0))
        out_shape = jax.ShapeDtypeStruct((batch * seq, w_b), BF16)
        scratch += [pltpu.VMEM((n_sub * BLK, w_b), F32), pltpu.VMEM((BLK, HEAD_DIM), F32)]
    else:
        out_specs = [acc_spec, ml_spec]
        scratch.append(pltpu.VMEM((n_sub, BLK, BLK), F32))
        out_shape = [jax.ShapeDtypeStruct((batch, rep, dil, sub, w_b), F32),
                     jax.ShapeDtypeStruct((batch, rep, dil, sub, BLK), F32)]
    kern = functools.partial(_attn_kernel, n_heads=n_heads, n_sub=n_sub, first=first, last=last,
                             diag_bucket=diag_bucket, io=io, io_state=io_state)
    out = pl.pallas_call(
        kern,
        grid=(batch, dil, nb // n_sub),
        in_specs=in_specs,
        out_specs=out_specs,
        out_shape=out_shape,
        scratch_shapes=scratch,
        compiler_params=_params(("arbitrary", "arbitrary", "arbitrary")),
        name=f"attn_d{dil}",
    )(*args)
    if last:
        return out
    acc, ml = out
    return (acc.reshape(batch, MAX_DIL, sub, w_b), ml.reshape(batch, MAX_DIL, sub, BLK))


def _out_proj_kernel(x_ref, ya_ref, yb_ref, wa_ref, wb_ref, h_ref):
    def product(c):
        cols = slice(c * MXU_COLS, (c + 1) * MXU_COLS)
        return (jnp.dot(ya_ref[...], wa_ref[:, cols], preferred_element_type=F32)
                + jnp.dot(yb_ref[...], wb_ref[:, cols], preferred_element_type=F32))

    n_sub = h_ref.shape[1] // MXU_COLS
    acc = product(0)
    for c in range(n_sub):
        nxt = product(c + 1) if c + 1 < n_sub else None
        cols = slice(c * MXU_COLS, (c + 1) * MXU_COLS)
        h_ref[:, cols] = x_ref[:, cols] + acc
        acc = nxt


def _out_proj(x2, y_a, y_b, w_out_bf):
    m, d = x2.shape
    w_a = y_a.shape[1]
    w_b = y_b.shape[1]
    assert w_a == w_b
    tm = _pick(m, 1024)
    tn = _pick(d, 512)
    assert tn % MXU_COLS == 0
    return pl.pallas_call(
        _out_proj_kernel,
        grid=(m // tm, d // tn),
        in_specs=[
            pl.BlockSpec((tm, tn), lambda i, j: (i, j)),
            pl.BlockSpec((tm, w_a), lambda i, j: (i, 0)),
            pl.BlockSpec((tm, w_b), lambda i, j: (i, 0)),
            pl.BlockSpec((w_a, tn), lambda i, j: (0, j)),
            pl.BlockSpec((w_b, tn), lambda i, j: (1, j)),
        ],
        out_specs=pl.BlockSpec((tm, tn), lambda i, j: (i, j)),
        out_shape=jax.ShapeDtypeStruct((m, d), F32),
        compiler_params=_params(("parallel", "arbitrary")),
        name="out_proj",
    )(x2, y_a, y_b, w_out_bf, w_out_bf)


def _ple_kernel(h_ref, g_ref, wg_ref, p_ref, wu_ref, o_ref, hn_ref):
    j = pl.program_id(1)

    @pl.when(j == 0)
    def _():
        _normalise_rows(h_ref, g_ref, hn_ref, 16)

    tn = o_ref.shape[1]
    n_sub = tn // MXU_COLS
    p_bf = p_ref[...].astype(BF16)

    def product(c):
        return jnp.dot(hn_ref[...], wg_ref[:, c * MXU_COLS:(c + 1) * MXU_COLS], preferred_element_type=F32)

    acc = product(0)
    for c in range(n_sub):
        nxt = product(c + 1) if c + 1 < n_sub else None
        cols = slice(c * MXU_COLS, (c + 1) * MXU_COLS)
        up = jnp.dot(p_bf, wu_ref[:, cols], preferred_element_type=F32)
        h_cols = pl.ds(pl.multiple_of(j * tn + c * MXU_COLS, MXU_COLS), MXU_COLS)
        o_ref[:, cols] = h_ref[:, h_cols] + jax.nn.sigmoid(acc) * up
        acc = nxt


def _ple(h, g_ple, w_gate_bf, p2, w_up_bf):
    m, d = h.shape
    d_ple = p2.shape[1]
    tm = _pick(m, 512)
    tn = _pick(d, 1024)
    assert tn % MXU_COLS == 0
    return pl.pallas_call(
        _ple_kernel,
        grid=(m // tm, d // tn),
        in_specs=[
            pl.BlockSpec((tm, d), lambda i, j: (i, 0)),
            pl.BlockSpec((1, d), lambda i, j: (0, 0)),
            pl.BlockSpec((d, tn), lambda i, j: (0, j)),
            pl.BlockSpec((tm, d_ple), lambda i, j: (i, 0)),
            pl.BlockSpec((d_ple, tn), lambda i, j: (0, j)),
        ],
        out_specs=pl.BlockSpec((tm, tn), lambda i, j: (i, j)),
        out_shape=jax.ShapeDtypeStruct((m, d), F32),
        scratch_shapes=[pltpu.VMEM((tm, d), BF16)],
        compiler_params=_params(("parallel", "arbitrary")),
        name="ple",
    )(h, g_ple, w_gate_bf, p2, w_up_bf)


def kernel(x, p, g_pre, w_in, w_s, b_s, ln_v_g, ln_v_b, g_q, g_k, rel_bias, g_out_a, g_out_b, w_out, g_ple, w_ple_gate, w_ple_up):
    batch, seq, d = x.shape
    depth = p.shape[0]
    w_a = ln_v_g.shape[-1]
    w_b = g_out_b.shape[-1]
    d_in = w_in.shape[-1]
    assert w_a == w_b and d_in == N_SEG * w_a, "segments of the combined projection must be equally wide"
    assert seq % (MAX_DIL * BLK) == 0 and all(win // dil == BLK for win, dil in DILATED)
    assert sorted(dil for _, dil in DILATED) == [1, 4, MAX_DIL]
    m = batch * seq
    x2 = x.reshape(m, d)
    for i in range(depth):
        proj, qkv_rm = _in_proj(x2, g_pre[i][None], w_in[i].astype(BF16), g_q[i][None], g_k[i][None],
                                batch=batch, seq=seq)
        y_a = _gmlp(proj, w_s[i], b_s[i].T, ln_v_g[i][None], ln_v_b[i][None], g_out_a[i][None])
        state = None
        for dil in sorted((dil for _, dil in DILATED), reverse=True):
            state = _attn_pass(dil, qkv_rm, proj, rel_bias, state, g_out_b[i][None],
                               batch=batch, seq=seq, w_b=w_b)
        y_b = state
        h = _out_proj(x2, y_a, y_b, w_out[i].astype(BF16))
        x2 = _ple(h, g_ple[i][None], w_ple_gate[i].astype(BF16), p[i].reshape(m, -1), w_ple_up[i].astype(BF16))
    return x2.reshape(batch, seq, d)
```

```python
import functools
import math

import numpy as np
import jax
import jax.numpy as jnp
from jax import lax
from jax.experimental import pallas as pl
from jax.experimental.pallas import tpu as pltpu

HEAD_DIM = 128
CHUNK = 128
BLK = 128
QK_LOOKAHEAD = 3
ATTN_BLOCKS_PER_STEP = 4
GMLP_CHUNKS_PER_STEP = 2
DILATED = ((128, 1), (512, 4), (2048, 16))
MAX_DIL = 16
NUM_BUCKETS = 32
MAX_DISTANCE = 2048
EPS = 1e-6
NEG_INF = -1e30
N_SEG = 7
SEG_Q, SEG_K, SEG_V, SEG_BZ = 3, 4, 5, 6

V7X_VMEM_LIMIT_BYTES = 56 * 1024 * 1024
MXU_COLS = 256
SLAB_PITCH = 24

BF16 = jnp.bfloat16
F32 = jnp.float32


def _pick(n, pref):
    t = min(n, pref)
    while n % t:
        t //= 2
    return t


def _params(sem):
    return pltpu.CompilerParams(dimension_semantics=sem,
                                vmem_limit_bytes=V7X_VMEM_LIMIT_BYTES)


def _gelu(v):
    return 0.5 * v * (1.0 + lax.erf(v * (1.0 / math.sqrt(2.0))))


def _silu(v):
    return v * jax.nn.sigmoid(v)


def _normalise_rows(x_ref, g_ref, hn_ref, rows):
    tm = x_ref.shape[0]

    def body(c, carry):
        r = pl.ds(pl.multiple_of(c * rows, rows), rows)
        xc = x_ref[r, :]
        ms = jnp.mean(xc * xc, axis=-1, keepdims=True)
        hn_ref[r, :] = (xc * lax.rsqrt(ms + EPS) * g_ref[...]).astype(BF16)
        return carry

    lax.fori_loop(0, tm // rows, body, 0, unroll=4)


def _in_proj_kernel(x_ref, g_ref, w_ref, gq_ref, gk_ref, o_ref, op_ref, hn_ref, slab_ref, *, tiles_per_seg):
    j = pl.program_id(1)

    @pl.when(j == 0)
    def _():
        _normalise_rows(x_ref, g_ref, hn_ref, 16)

    seg = j // tiles_per_seg
    tm, tn = o_ref.shape
    n_sub = tn // MXU_COLS

    def sub_cols(c):
        return slice(c * MXU_COLS, (c + 1) * MXU_COLS)

    def product(c):
        return jnp.dot(hn_ref[...], w_ref[:, sub_cols(c)], preferred_element_type=F32)

    def tile(epilogue, residue_major=False):
        acc = product(0)
        for c in range(n_sub):
            nxt = product(c + 1) if c + 1 < n_sub else None
            lanes = MXU_COLS // HEAD_DIM
            y = epilogue(acc)
            o_ref[:, sub_cols(c)] = y.astype(BF16)
            if residue_major:
                for k in range(lanes):
                    for a in range(tm // MAX_DIL):
                        slab_ref[c * lanes + k, a * SLAB_PITCH:a * SLAB_PITCH + MAX_DIL, :] = (
                            y[a * MAX_DIL:(a + 1) * MAX_DIL, k * HEAD_DIM:(k + 1) * HEAD_DIM])
                for r in range(MAX_DIL):
                    rows = [slab_ref[c * lanes + k, pl.ds(r, tm // MAX_DIL, stride=SLAB_PITCH), :]
                            for k in range(lanes)]
                    op_ref[0, r, :, sub_cols(c)] = jnp.concatenate(rows, axis=1).astype(BF16)
            acc = nxt

    def head_norm(g, scale):
        def epilogue(acc):
            out = []
            for c in range(MXU_COLS // HEAD_DIM):
                blk = acc[:, c * HEAD_DIM:(c + 1) * HEAD_DIM]
                ms = jnp.mean(blk * blk, axis=-1, keepdims=True)
                y = blk * lax.rsqrt(ms + EPS) * g
                out.append(y if scale is None else y * scale)
            return jnp.concatenate(out, axis=1)
        return epilogue

    @pl.when(seg <= 1)
    def _():
        tile(_gelu)

    @pl.when((seg == 2) | (seg == SEG_BZ))
    def _():
        tile(_silu)

    @pl.when(seg == SEG_Q)
    def _():
        tile(head_norm(gq_ref[...], HEAD_DIM ** -0.5), residue_major=True)

    @pl.when(seg == SEG_K)
    def _():
        tile(head_norm(gk_ref[...], None), residue_major=True)

    @pl.when(seg == SEG_V)
    def _():
        tile(lambda acc: acc, residue_major=True)


def _in_proj(x2, g_pre, w_in_bf, g_q, g_k, *, batch, seq):
    m, d = x2.shape
    d_in = w_in_bf.shape[1]
    seg_w = d_in // N_SEG
    tm = _pick(seq, 512)
    tn = _pick(seg_w, 1024)
    assert tn % MXU_COLS == 0
    tps = seg_w // tn
    assert tm % (MAX_DIL * 16) == 0, "residue-major runs must cover whole bf16 sublane tiles"
    blocks_per_seq = seq // tm
    sub = seq // MAX_DIL
    runs = tm // MAX_DIL
    kern = functools.partial(_in_proj_kernel, tiles_per_seg=tps)

    def perm_idx(i, j):
        jj = jnp.clip(j - SEG_Q * tps, 0, 3 * tps - 1)
        return (i // blocks_per_seq, 0, i % blocks_per_seq, jj)

    return pl.pallas_call(
        kern,
        grid=(m // tm, d_in // tn),
        in_specs=[
            pl.BlockSpec((tm, d), lambda i, j: (i, 0)),
            pl.BlockSpec((1, d), lambda i, j: (0, 0)),
            pl.BlockSpec((d, tn), lambda i, j: (0, j)),
            pl.BlockSpec((1, HEAD_DIM), lambda i, j: (0, 0)),
            pl.BlockSpec((1, HEAD_DIM), lambda i, j: (0, 0)),
        ],
        out_specs=[
            pl.BlockSpec((tm, tn), lambda i, j: (i, j)),
            pl.BlockSpec((1, MAX_DIL, runs, tn), perm_idx),
        ],
        out_shape=[
            jax.ShapeDtypeStruct((m, d_in), BF16),
            jax.ShapeDtypeStruct((batch, MAX_DIL, sub, 3 * seg_w), BF16),
        ],
        scratch_shapes=[pltpu.VMEM((tm, d), BF16), pltpu.VMEM((tn // HEAD_DIM, runs * SLAB_PITCH, HEAD_DIM), F32)],
        compiler_params=_params(("arbitrary", "arbitrary")),
        name="in_proj",
    )(x2, g_pre, w_in_bf, g_q, g_k)


def _gmlp_kernel(au_ref, av_ref, az_ref, ws_ref, bs_ref, lg_ref, lb_ref, go_ref, o_ref, ya_ref):
    n_chunks = av_ref.shape[0] // CHUNK
    av = av_ref[...].astype(F32)
    mu = jnp.mean(av, axis=-1, keepdims=True)
    xc = av - mu
    var = jnp.mean(xc * xc, axis=-1, keepdims=True)
    avn = xc * lax.rsqrt(var + EPS) * lg_ref[...] + lb_ref[...]

    row = lax.broadcasted_iota(jnp.int32, (CHUNK, CHUNK), 0)
    col = lax.broadcasted_iota(jnp.int32, (CHUNK, CHUNK), 1)
    causal = col <= row
    n_groups = ws_ref.shape[0]
    for g in range(n_groups):
        sl = slice(g * HEAD_DIM, (g + 1) * HEAD_DIM)
        wm = jnp.where(causal, ws_ref[g], 0.0).astype(BF16)
        rhs = jnp.concatenate([avn[c * CHUNK:(c + 1) * CHUNK, sl] for c in range(n_chunks)], axis=1)
        z = jnp.dot(wm, rhs.astype(BF16), preferred_element_type=F32)
        for c in range(n_chunks):
            rows = slice(c * CHUNK, (c + 1) * CHUNK)
            zc = z[:, c * HEAD_DIM:(c + 1) * HEAD_DIM] + bs_ref[:, g:g + 1]
            ya_ref[rows, sl] = au_ref[rows, sl].astype(F32) * zc
    ya = ya_ref[...]
    ms = jnp.mean(ya * ya, axis=-1, keepdims=True)
    o_ref[...] = (ya * lax.rsqrt(ms + EPS) * go_ref[...] * az_ref[...].astype(F32)).astype(BF16)


def _gmlp(proj, w_s, b_s_t, ln_g, ln_b, g_out_a):
    m, d_in = proj.shape
    w_a = d_in // N_SEG
    n_groups = w_s.shape[0]
    rows = GMLP_CHUNKS_PER_STEP * CHUNK
    assert m % rows == 0
    vec = lambda: pl.BlockSpec((1, w_a), lambda i: (0, 0))
    return pl.pallas_call(
        _gmlp_kernel,
        grid=(m // rows,),
        in_specs=[
            pl.BlockSpec((rows, w_a), lambda i: (i, 0)),
            pl.BlockSpec((rows, w_a), lambda i: (i, 1)),
            pl.BlockSpec((rows, w_a), lambda i: (i, 2)),
            pl.BlockSpec((n_groups, CHUNK, CHUNK), lambda i: (0, 0, 0)),
            pl.BlockSpec((CHUNK, n_groups), lambda i: (0, 0)),
            vec(), vec(), vec(),
        ],
        out_specs=pl.BlockSpec((rows, w_a), lambda i: (i, 0)),
        out_shape=jax.ShapeDtypeStruct((m, w_a), BF16),
        scratch_shapes=[pltpu.VMEM((rows, w_a), F32)],
        compiler_params=_params(("parallel",)),
        name="gmlp",
    )(proj, proj, proj, w_s, b_s_t, ln_g, ln_b, g_out_a)


def _rel_bucket_np(dist):
    max_exact = NUM_BUCKETS // 2
    d = np.maximum(dist, 1).astype(np.float32)
    large = max_exact + (np.log(d / np.float32(max_exact)) / np.float32(math.log(MAX_DISTANCE / max_exact))
                         * np.float32(NUM_BUCKETS - max_exact)).astype(np.int32)
    large = np.minimum(large, NUM_BUCKETS - 1)
    return np.where(dist < max_exact, dist, large).astype(np.int32)


def _band_tables(dil, pos):
    i_q = pos[:, None]
    i_k = pos[None, :]
    in_cur = i_k <= i_q
    delta = np.where(in_cur, i_q - i_k, BLK + i_q - i_k)
    return in_cur.astype(np.int32), _rel_bucket_np(delta * dil)


class _TileIO:
    def __init__(self, prefix, run, rows_shape, g=0):
        self.prefix = prefix
        self.run = run
        self.rows_shape = rows_shape
        self.idx = prefix + (slice(g * run, (g + 1) * run),)

    def sub(self, g):
        return _TileIO(self.prefix, self.run, self.rows_shape, g)

    def load(self, ref, sl):
        return ref[self.idx + (sl,)].reshape(BLK, sl.stop - sl.start)

    def store(self, ref, sl, val):
        ref[self.idx + (sl,)] = val.reshape(self.rows_shape + (sl.stop - sl.start,))


def _attn_kernel(*refs, n_heads, n_sub, first, last, diag_bucket, io, io_state):
    it = iter(refs)
    cur_ref, bucket_ref, relb_ref = next(it), next(it), next(it)
    q_ref, kc_ref, vc_ref = next(it), next(it), next(it)
    if not first:
        o_in_ref, lse_in_ref = next(it), next(it)
    if last:
        bz_ref, go_ref = next(it), next(it)
        y_ref = next(it)
    else:
        o_out_ref, lse_out_ref = next(it), next(it)
    bias_ref, kp_ref, vp_ref = next(it), next(it), next(it)
    if last:
        yb_ref, st_ref = next(it), next(it)
    else:
        lse_ref = next(it)

    n = pl.program_id(2)
    first_step = (pl.program_id(0) == 0) & (pl.program_id(1) == 0) & (n == 0)
    prev_io = _TileIO((), BLK, (BLK,))

    @pl.when(first_step)
    def _():
        bk = bucket_ref[...]
        for h in range(n_heads):
            tab = jnp.zeros((BLK, BLK), F32)
            for b in range(NUM_BUCKETS):
                tab = jnp.where(bk == b, relb_ref[b, h], tab)
            bias_ref[h] = tab
        if not last:
            lse_ref[...] = jnp.zeros(lse_ref.shape, F32)

    def natural_rows(tile):
        st_ref[...] = tile
        per = BLK // MAX_DIL
        rows = [st_ref[pl.ds((MAX_DIL // 2) * per * (v % 2) + v // 2, 8, stride=per), :]
                for v in range(BLK // 8)]
        return jnp.concatenate(rows, axis=0)

    nt = (((1,), (1,)), ((), ()))

    def heads(g, has_prev):
        cur_io, st_io = io.sub(g), io_state.sub(g)
        kp_io, kp_src, vp_src = (io.sub(g - 1), kc_ref, vc_ref) if g else (prev_io, kp_ref, vp_ref)
        in_cur = cur_ref[...] != 0
        ones = jnp.ones((BLK, HEAD_DIM), BF16)
        if has_prev:
            eye = (lax.broadcasted_iota(jnp.int32, (BLK, BLK), 0)
                   == lax.broadcasted_iota(jnp.int32, (BLK, BLK), 1))
        if not first:
            lse_in = st_io.load(lse_in_ref, slice(0, BLK))
            if last:
                lse_in = natural_rows(lse_in)

        def logits(h):
            sl = slice(h * HEAD_DIM, (h + 1) * HEAD_DIM)
            keys = cur_io.load(kc_ref, sl)
            if has_prev:
                keys = jnp.concatenate([kp_io.load(kp_src, sl), keys], axis=0)
            return lax.dot_general(cur_io.load(q_ref, sl), keys, nt, preferred_element_type=F32)

        def softmax(h, s2):
            if has_prev:
                s_p, s_c = s2[:, :BLK], s2[:, BLK:]
                s = jnp.where(in_cur, s_c, s_p) + bias_ref[h]
                far = jnp.where(eye, s_p + relb_ref[diag_bucket, h], NEG_INF)
                mx = jnp.max(jnp.maximum(s, far), axis=-1, keepdims=True)
            else:
                s = jnp.where(in_cur, s2 + bias_ref[h], NEG_INF)
                mx = jnp.max(s, axis=-1, keepdims=True)
            lse_old = None
            if not first:
                lse_old = jnp.broadcast_to(lse_in[:, h:h + 1], (BLK, HEAD_DIM))
                mx = jnp.maximum(mx, lse_old)
            e = jnp.exp(s - mx)
            if has_prev:
                e_far = jnp.exp(far - mx)
                probs = jnp.concatenate([jnp.where(in_cur, e_far, e), jnp.where(in_cur, e, 0.0)], axis=1)
            else:
                probs = e
            return probs.astype(BF16), mx, lse_old

        def values(h, probs, mx, lse_old):
            sl = slice(h * HEAD_DIM, (h + 1) * HEAD_DIM)
            vals = jnp.concatenate([cur_io.load(vc_ref, sl), ones], axis=1)
            if has_prev:
                vals = jnp.concatenate([jnp.concatenate([kp_io.load(vp_src, sl), ones], axis=1), vals], axis=0)
            both = jnp.dot(probs, vals, preferred_element_type=F32)
            num, den = both[:, :HEAD_DIM], both[:, HEAD_DIM:]
            if first:
                total = den
                o = num / total
            else:
                o_old = st_io.load(o_in_ref, sl)
                if last:
                    o_old = natural_rows(o_old)
                w_old = jnp.exp(lse_old - mx)
                total = w_old + den
                o = (o_old * w_old + num) / total
            if last:
                yb_ref[g * BLK:(g + 1) * BLK, sl] = o
            else:
                st_io.store(o_out_ref, sl, o)
                lse = mx + jnp.log(total)
                lse_ref[g, :, h:h + 1] = lse[:, h:h + 1]

        ahead = {h: logits(h) for h in range(min(QK_LOOKAHEAD, n_heads))}
        soft = softmax(0, ahead.pop(0))
        for h in range(n_heads):
            if h + QK_LOOKAHEAD < n_heads:
                ahead[h + QK_LOOKAHEAD] = logits(h + QK_LOOKAHEAD)
            nxt = softmax(h + 1, ahead.pop(h + 1)) if h + 1 < n_heads else None
            values(h, *soft)
            soft = nxt

    @pl.when(n == 0)
    def _():
        heads(0, False)

    @pl.when(n > 0)
    def _():
        heads(0, True)

    for g in range(1, n_sub):
        heads(g, True)

    whole = slice(0, kp_ref.shape[1])
    kp_ref[...] = io.sub(n_sub - 1).load(kc_ref, whole)
    vp_ref[...] = io.sub(n_sub - 1).load(vc_ref, whole)
    if last:
        yb = yb_ref[...]
        ms = jnp.mean(yb * yb, axis=-1, keepdims=True)
        y_ref[...] = (yb * lax.rsqrt(ms + EPS) * go_ref[...] * bz_ref[...].astype(F32)).astype(BF16)
    else:
        for g in range(n_sub):
            io_state.sub(g).store(lse_out_ref, slice(0, BLK), lse_ref[g])


def _attn_pass(dil, qkv_rm, proj, rel_bias, state, g_out_b, *, batch, seq, w_b):
    first = state is None
    last = dil == 1
    n_heads = w_b // HEAD_DIM
    assert 2 * n_heads <= BLK
    sub = seq // MAX_DIL
    nb = seq // dil // BLK
    rep = MAX_DIL // dil
    runs = BLK // rep
    n_sub = min(ATTN_BLOCKS_PER_STEP, nb)
    assert nb % n_sub == 0
    col_q, col_k, col_v = 0, 1, 2

    if last:
        pos = np.arange(BLK)
        io = _TileIO((), BLK, (BLK,))
        io_state = _TileIO((0, slice(None)), BLK // MAX_DIL, (MAX_DIL, BLK // MAX_DIL))
        blk = (n_sub * BLK, w_b)
        src = proj
        cur = lambda seg: (lambda b, r, n: (b * (nb // n_sub) + n, seg))
        col_q, col_k, col_v = SEG_Q, SEG_K, SEG_V
        st_blk = lambda w: (1, MAX_DIL, n_sub * BLK // MAX_DIL, w)
        st_idx = lambda b, r, n: (b, 0, n, 0)
        view = lambda a: a
    else:
        rho = np.arange(BLK)
        pos = rep * (rho % runs) + rho // runs
        io = io_state = _TileIO((0, slice(None), 0), runs, (rep, runs))
        blk = (1, rep, 1, n_sub * runs, w_b)
        src = qkv_rm.reshape(batch, rep, dil, sub, 3 * w_b)
        cur = lambda col: (lambda b, r, n: (b, 0, r, n, col))
        st_blk = lambda w: (1, rep, 1, n_sub * runs, w)
        st_idx = lambda b, r, n: (b, 0, r, n, 0)
        view = lambda a: a.reshape(batch, rep, dil, sub, a.shape[-1])

    in_cur, bucket = _band_tables(dil, pos)
    diag_bucket = int(_rel_bucket_np(np.array([BLK * dil]))[0])
    const = lambda: pl.BlockSpec((BLK, BLK), lambda b, r, n: (0, 0))
    in_specs = [
        const(), const(), pl.BlockSpec(memory_space=pltpu.SMEM),
        pl.BlockSpec(blk, cur(col_q)), pl.BlockSpec(blk, cur(col_k)), pl.BlockSpec(blk, cur(col_v)),
    ]
    args = [jnp.asarray(in_cur), jnp.asarray(bucket), rel_bias, src, src, src]
    acc_spec = pl.BlockSpec(st_blk(w_b), st_idx)
    ml_spec = pl.BlockSpec(st_blk(BLK), st_idx)
    if not first:
        acc, ml = state
        in_specs += [acc_spec, ml_spec]
        args += [view(acc), view(ml)]
    scratch = [pltpu.VMEM((n_heads, BLK, BLK), F32),
               pltpu.VMEM((BLK, w_b), BF16), pltpu.VMEM((BLK, w_b), BF16)]
    if last:
        in_specs += [pl.BlockSpec(blk, cur(SEG_BZ)), pl.BlockSpec((1, w_b), lambda b, r, n: (0, 0))]
        args += [proj, g_out_b]
        out_specs = pl.BlockSpec(blk, cur(0))
        out_shape = jax.ShapeDtypeStruct((batch * seq, w_b), BF16)
        scratch += [pltpu.VMEM((n_sub * BLK, w_b), F32), pltpu.VMEM((BLK, HEAD_DIM), F32)]
    else:
        out_specs = [acc_spec, ml_spec]
        scratch.append(pltpu.VMEM((n_sub, BLK, BLK), F32))
        out_shape = [jax.ShapeDtypeStruct((batch, rep, dil, sub, w_b), F32),
                     jax.ShapeDtypeStruct((batch, rep, dil, sub, BLK), F32)]
    kern = functools.partial(_attn_kernel, n_heads=n_heads, n_sub=n_sub, first=first, last=last,
                             diag_bucket=diag_bucket, io=io, io_state=io_state)
    out = pl.pallas_call(
        kern,
        grid=(batch, dil, nb // n_sub),
        in_specs=in_specs,
        out_specs=out_specs,
        out_shape=out_shape,
        scratch_shapes=scratch,
        compiler_params=_params(("arbitrary", "arbitrary", "arbitrary")),
        name=f"attn_d{dil}",
    )(*args)
    if last:
        return out
    acc, ml = out
    return (acc.reshape(batch, MAX_DIL, sub, w_b), ml.reshape(batch, MAX_DIL, sub, BLK))


def _out_proj_kernel(x_ref, ya_ref, yb_ref, wa_ref, wb_ref, h_ref):
    def product(c):
        cols = slice(c * MXU_COLS, (c + 1) * MXU_COLS)
        return (jnp.dot(ya_ref[...], wa_ref[:, cols], preferred_element_type=F32)
                + jnp.dot(yb_ref[...], wb_ref[:, cols], preferred_element_type=F32))

    n_sub = h_ref.shape[1] // MXU_COLS
    acc = product(0)
    for c in range(n_sub):
        nxt = product(c + 1) if c + 1 < n_sub else None
        cols = slice(c * MXU_COLS, (c + 1) * MXU_COLS)
        h_ref[:, cols] = x_ref[:, cols] + acc
        acc = nxt


def _out_proj(x2, y_a, y_b, w_out_bf):
    m, d = x2.shape
    w_a = y_a.shape[1]
    w_b = y_b.shape[1]
    assert w_a == w_b
    tm = _pick(m, 1024)
    tn = _pick(d, 512)
    assert tn % MXU_COLS == 0
    return pl.pallas_call(
        _out_proj_kernel,
        grid=(m // tm, d // tn),
        in_specs=[
            pl.BlockSpec((tm, tn), lambda i, j: (i, j)),
            pl.BlockSpec((tm, w_a), lambda i, j: (i, 0)),
            pl.BlockSpec((tm, w_b), lambda i, j: (i, 0)),
            pl.BlockSpec((w_a, tn), lambda i, j: (0, j)),
            pl.BlockSpec((w_b, tn), lambda i, j: (1, j)),
        ],
        out_specs=pl.BlockSpec((tm, tn), lambda i, j: (i, j)),
        out_shape=jax.ShapeDtypeStruct((m, d), F32),
        compiler_params=_params(("parallel", "arbitrary")),
        name="out_proj",
    )(x2, y_a, y_b, w_out_bf, w_out_bf)


def _ple_kernel(h_ref, g_ref, wg_ref, p_ref, wu_ref, o_ref, hn_ref):
    j = pl.program_id(1)

    @pl.when(j == 0)
    def _():
        _normalise_rows(h_ref, g_ref, hn_ref, 16)

    tn = o_ref.shape[1]
    n_sub = tn // MXU_COLS
    p_bf = p_ref[...].astype(BF16)

    def product(c):
        return jnp.dot(hn_ref[...], wg_ref[:, c * MXU_COLS:(c + 1) * MXU_COLS], preferred_element_type=F32)

    acc = product(0)
    for c in range(n_sub):
        nxt = product(c + 1) if c + 1 < n_sub else None
        cols = slice(c * MXU_COLS, (c + 1) * MXU_COLS)
        up = jnp.dot(p_bf, wu_ref[:, cols], preferred_element_type=F32)
        h_cols = pl.ds(pl.multiple_of(j * tn + c * MXU_COLS, MXU_COLS), MXU_COLS)
        o_ref[:, cols] = h_ref[:, h_cols] + jax.nn.sigmoid(acc) * up
        acc = nxt


def _ple(h, g_ple, w_gate_bf, p2, w_up_bf):
    m, d = h.shape
    d_ple = p2.shape[1]
    tm = _pick(m, 512)
    tn = _pick(d, 1024)
    assert tn % MXU_COLS == 0
    return pl.pallas_call(
        _ple_kernel,
        grid=(m // tm, d // tn),
        in_specs=[
            pl.BlockSpec((tm, d), lambda i, j: (i, 0)),
            pl.BlockSpec((1, d), lambda i, j: (0, 0)),
            pl.BlockSpec((d, tn), lambda i, j: (0, j)),
            pl.BlockSpec((tm, d_ple), lambda i, j: (i, 0)),
            pl.BlockSpec((d_ple, tn), lambda i, j: (0, j)),
        ],
        out_specs=pl.BlockSpec((tm, tn), lambda i, j: (i, j)),
        out_shape=jax.ShapeDtypeStruct((m, d), F32),
        scratch_shapes=[pltpu.VMEM((tm, d), BF16)],
        compiler_params=_params(("parallel", "arbitrary")),
        name="ple",
    )(h, g_ple, w_gate_bf, p2, w_up_bf)


def kernel(x, p, g_pre, w_in, w_s, b_s, ln_v_g, ln_v_b, g_q, g_k, rel_bias, g_out_a, g_out_b, w_out, g_ple, w_ple_gate, w_ple_up):
    batch, seq, d = x.shape
    depth = p.shape[0]
    w_a = ln_v_g.shape[-1]
    w_b = g_out_b.shape[-1]
    d_in = w_in.shape[-1]
    assert w_a == w_b and d_in == N_SEG * w_a, "segments of the combined projection must be equally wide"
    assert seq % (MAX_DIL * BLK) == 0 and all(win // dil == BLK for win, dil in DILATED)
    assert sorted(dil for _, dil in DILATED) == [1, 4, MAX_DIL]
    m = batch * seq
    x2 = x.reshape(m, d)
    for i in range(depth):
        proj, qkv_rm = _in_proj(x2, g_pre[i][None], w_in[i].astype(BF16), g_q[i][None], g_k[i][None],
                                batch=batch, seq=seq)
        y_a = _gmlp(proj, w_s[i], b_s[i].T, ln_v_g[i][None], ln_v_b[i][None], g_out_a[i][None])
        state = None
        for dil in sorted((dil for _, dil in DILATED), reverse=True):
            state = _attn_pass(dil, qkv_rm, proj, rel_bias, state, g_out_b[i][None],
                               batch=batch, seq=seq, w_b=w_b)
        y_b = state
        h = _out_proj(x2, y_a, y_b, w_out[i].astype(BF16))
        x2 = _ple(h, g_ple[i][None], w_ple_gate[i].astype(BF16), p[i].reshape(m, -1), w_ple_up[i].astype(BF16))
    return x2.reshape(batch, seq, d)
```

```python
import functools
import math

import numpy as np
import jax
import jax.numpy as jnp
from jax import lax
from jax.experimental import pallas as pl
from jax.experimental.pallas import tpu as pltpu

HEAD_DIM = 128
CHUNK = 128
BLK = 128
QK_LOOKAHEAD = 3
ATTN_BLOCKS_PER_STEP = 4
GMLP_CHUNKS_PER_STEP = 2
DILATED = ((128, 1), (512, 4), (2048, 16))
MAX_DIL = 16
NUM_BUCKETS = 32
MAX_DISTANCE = 2048
EPS = 1e-6
NEG_INF = -1e30
N_SEG = 7
SEG_Q, SEG_K, SEG_V, SEG_BZ = 3, 4, 5, 6

V7X_VMEM_LIMIT_BYTES = 56 * 1024 * 1024
MXU_COLS = 256
SLAB_PITCH = 24

BF16 = jnp.bfloat16
F32 = jnp.float32


def _pick(n, pref):
    t = min(n, pref)
    while n % t:
        t //= 2
    return t


def _params(sem):
    return pltpu.CompilerParams(dimension_semantics=sem,
                                vmem_limit_bytes=V7X_VMEM_LIMIT_BYTES)


def _gelu(v):
    return 0.5 * v * (1.0 + lax.erf(v * (1.0 / math.sqrt(2.0))))


def _silu(v):
    return v * jax.nn.sigmoid(v)


def _normalise_rows(x_ref, g_ref, hn_ref, rows):
    tm = x_ref.shape[0]

    def body(c, carry):
        r = pl.ds(pl.multiple_of(c * rows, rows), rows)
        xc = x_ref[r, :]
        ms = jnp.mean(xc * xc, axis=-1, keepdims=True)
        hn_ref[r, :] = (xc * lax.rsqrt(ms + EPS) * g_ref[...]).astype(BF16)
        return carry

    lax.fori_loop(0, tm // rows, body, 0, unroll=4)


def _in_proj_kernel(x_ref, g_ref, w_ref, gq_ref, gk_ref, *rest, tiles_per_seg, n_cast):
    cast_in, (o_ref, op_ref), cast_out = rest[:n_cast], rest[n_cast:n_cast + 2], rest[n_cast + 2:2 * n_cast + 2]
    hn_ref, slab_ref = rest[2 * n_cast + 2:]
    j = pl.program_id(1)

    @pl.when(j == 0)
    def _():
        _normalise_rows(x_ref, g_ref, hn_ref, 16)

    seg = j // tiles_per_seg
    tm, tn = o_ref.shape
    n_sub = tn // MXU_COLS

    def sub_cols(c):
        return slice(c * MXU_COLS, (c + 1) * MXU_COLS)

    def product(c):
        return jnp.dot(hn_ref[...], w_ref[:, sub_cols(c)], preferred_element_type=F32)

    def tile(epilogue, residue_major=False):
        acc = product(0)
        for src_ref, dst_ref in zip(cast_in, cast_out):
            dst_ref[...] = src_ref[...].astype(BF16)
        for c in range(n_sub):
            nxt = product(c + 1) if c + 1 < n_sub else None
            lanes = MXU_COLS // HEAD_DIM
            y = epilogue(acc)
            o_ref[:, sub_cols(c)] = y.astype(BF16)
            if residue_major:
                for k in range(lanes):
                    for a in range(tm // MAX_DIL):
                        slab_ref[c * lanes + k, a * SLAB_PITCH:a * SLAB_PITCH + MAX_DIL, :] = (
                            y[a * MAX_DIL:(a + 1) * MAX_DIL, k * HEAD_DIM:(k + 1) * HEAD_DIM])
                for r in range(MAX_DIL):
                    rows = [slab_ref[c * lanes + k, pl.ds(r, tm // MAX_DIL, stride=SLAB_PITCH), :]
                            for k in range(lanes)]
                    op_ref[0, r, :, sub_cols(c)] = jnp.concatenate(rows, axis=1).astype(BF16)
            acc = nxt

    def head_norm(g, scale):
        def epilogue(acc):
            out = []
            for c in range(MXU_COLS // HEAD_DIM):
                blk = acc[:, c * HEAD_DIM:(c + 1) * HEAD_DIM]
                ms = jnp.mean(blk * blk, axis=-1, keepdims=True)
                y = blk * lax.rsqrt(ms + EPS) * g
                out.append(y if scale is None else y * scale)
            return jnp.concatenate(out, axis=1)
        return epilogue

    @pl.when(seg <= 1)
    def _():
        tile(_gelu)

    @pl.when((seg == 2) | (seg == SEG_BZ))
    def _():
        tile(_silu)

    @pl.when(seg == SEG_Q)
    def _():
        tile(head_norm(gq_ref[...], HEAD_DIM ** -0.5), residue_major=True)

    @pl.when(seg == SEG_K)
    def _():
        tile(head_norm(gk_ref[...], None), residue_major=True)

    @pl.when(seg == SEG_V)
    def _():
        tile(lambda acc: acc, residue_major=True)


def _in_proj(x2, g_pre, w_in_bf, g_q, g_k, later_weights, *, batch, seq):
    m, d = x2.shape
    d_in = w_in_bf.shape[1]
    seg_w = d_in // N_SEG
    tm = _pick(seq, 512)
    tn = _pick(seg_w, 1024)
    assert tn % MXU_COLS == 0
    tps = seg_w // tn
    assert tm % (MAX_DIL * 16) == 0, "residue-major runs must cover whole bf16 sublane tiles"
    blocks_per_seq = seq // tm
    sub = seq // MAX_DIL
    runs = tm // MAX_DIL
    kern = functools.partial(_in_proj_kernel, tiles_per_seg=tps, n_cast=len(later_weights))
    n_j = d_in // tn
    steps = (m // tm) * n_j

    def perm_idx(i, j):
        jj = jnp.clip(j - SEG_Q * tps, 0, 3 * tps - 1)
        return (i // blocks_per_seq, 0, i % blocks_per_seq, jj)

    def cast_spec(w):
        rows = 16 * pl.cdiv(w.shape[0], 16 * steps)
        assert w.shape[0] % rows == 0
        last = w.shape[0] // rows - 1
        return pl.BlockSpec((rows, w.shape[1]), lambda i, j: (jnp.minimum(i * n_j + j, last), 0))

    out = pl.pallas_call(
        kern,
        grid=(m // tm, n_j),
        in_specs=[
            pl.BlockSpec((tm, d), lambda i, j: (i, 0)),
            pl.BlockSpec((1, d), lambda i, j: (0, 0)),
            pl.BlockSpec((d, tn), lambda i, j: (0, j)),
            pl.BlockSpec((1, HEAD_DIM), lambda i, j: (0, 0)),
            pl.BlockSpec((1, HEAD_DIM), lambda i, j: (0, 0)),
        ] + [cast_spec(w) for w in later_weights],
        out_specs=[
            pl.BlockSpec((tm, tn), lambda i, j: (i, j)),
            pl.BlockSpec((1, MAX_DIL, runs, tn), perm_idx),
        ] + [cast_spec(w) for w in later_weights],
        out_shape=[
            jax.ShapeDtypeStruct((m, d_in), BF16),
            jax.ShapeDtypeStruct((batch, MAX_DIL, sub, 3 * seg_w), BF16),
        ] + [jax.ShapeDtypeStruct(w.shape, BF16) for w in later_weights],
        scratch_shapes=[pltpu.VMEM((tm, d), BF16), pltpu.VMEM((tn // HEAD_DIM, runs * SLAB_PITCH, HEAD_DIM), F32)],
        compiler_params=_params(("arbitrary", "arbitrary")),
        name="in_proj",
    )(x2, g_pre, w_in_bf, g_q, g_k, *later_weights)
    return out[0], out[1], out[2:]


def _gmlp_kernel(au_ref, av_ref, az_ref, ws_ref, bs_ref, lg_ref, lb_ref, go_ref, o_ref, ya_ref):
    n_chunks = av_ref.shape[0] // CHUNK
    av = av_ref[...].astype(F32)
    mu = jnp.mean(av, axis=-1, keepdims=True)
    xc = av - mu
    var = jnp.mean(xc * xc, axis=-1, keepdims=True)
    avn = xc * lax.rsqrt(var + EPS) * lg_ref[...] + lb_ref[...]

    row = lax.broadcasted_iota(jnp.int32, (CHUNK, CHUNK), 0)
    col = lax.broadcasted_iota(jnp.int32, (CHUNK, CHUNK), 1)
    causal = col <= row
    n_groups = ws_ref.shape[0]
    for g in range(n_groups):
        sl = slice(g * HEAD_DIM, (g + 1) * HEAD_DIM)
        wm = jnp.where(causal, ws_ref[g], 0.0).astype(BF16)
        rhs = jnp.concatenate([avn[c * CHUNK:(c + 1) * CHUNK, sl] for c in range(n_chunks)], axis=1)
        z = jnp.dot(wm, rhs.astype(BF16), preferred_element_type=F32)
        for c in range(n_chunks):
            rows = slice(c * CHUNK, (c + 1) * CHUNK)
            zc = z[:, c * HEAD_DIM:(c + 1) * HEAD_DIM] + bs_ref[:, g:g + 1]
            ya_ref[rows, sl] = au_ref[rows, sl].astype(F32) * zc
    ya = ya_ref[...]
    ms = jnp.mean(ya * ya, axis=-1, keepdims=True)
    o_ref[...] = (ya * lax.rsqrt(ms + EPS) * go_ref[...] * az_ref[...].astype(F32)).astype(BF16)


def _gmlp(proj, w_s, b_s_t, ln_g, ln_b, g_out_a):
    m, d_in = proj.shape
    w_a = d_in // N_SEG
    n_groups = w_s.shape[0]
    rows = GMLP_CHUNKS_PER_STEP * CHUNK
    assert m % rows == 0
    vec = lambda: pl.BlockSpec((1, w_a), lambda i: (0, 0))
    return pl.pallas_call(
        _gmlp_kernel,
        grid=(m // rows,),
        in_specs=[
            pl.BlockSpec((rows, w_a), lambda i: (i, 0)),
            pl.BlockSpec((rows, w_a), lambda i: (i, 1)),
            pl.BlockSpec((rows, w_a), lambda i: (i, 2)),
            pl.BlockSpec((n_groups, CHUNK, CHUNK), lambda i: (0, 0, 0)),
            pl.BlockSpec((CHUNK, n_groups), lambda i: (0, 0)),
            vec(), vec(), vec(),
        ],
        out_specs=pl.BlockSpec((rows, w_a), lambda i: (i, 0)),
        out_shape=jax.ShapeDtypeStruct((m, w_a), BF16),
        scratch_shapes=[pltpu.VMEM((rows, w_a), F32)],
        compiler_params=_params(("parallel",)),
        name="gmlp",
    )(proj, proj, proj, w_s, b_s_t, ln_g, ln_b, g_out_a)


def _rel_bucket_np(dist):
    max_exact = NUM_BUCKETS // 2
    d = np.maximum(dist, 1).astype(np.float32)
    large = max_exact + (np.log(d / np.float32(max_exact)) / np.float32(math.log(MAX_DISTANCE / max_exact))
                         * np.float32(NUM_BUCKETS - max_exact)).astype(np.int32)
    large = np.minimum(large, NUM_BUCKETS - 1)
    return np.where(dist < max_exact, dist, large).astype(np.int32)


def _band_tables(dil, pos):
    i_q = pos[:, None]
    i_k = pos[None, :]
    in_cur = i_k <= i_q
    delta = np.where(in_cur, i_q - i_k, BLK + i_q - i_k)
    return in_cur.astype(np.int32), _rel_bucket_np(delta * dil)


class _TileIO:
    def __init__(self, prefix, run, rows_shape, g=0):
        self.prefix = prefix
        self.run = run
        self.rows_shape = rows_shape
        self.idx = prefix + (slice(g * run, (g + 1) * run),)

    def sub(self, g):
        return _TileIO(self.prefix, self.run, self.rows_shape, g)

    def load(self, ref, sl):
        return ref[self.idx + (sl,)].reshape(BLK, sl.stop - sl.start)

    def store(self, ref, sl, val):
        ref[self.idx + (sl,)] = val.reshape(self.rows_shape + (sl.stop - sl.start,))


def _attn_kernel(*refs, n_heads, n_sub, first, last, diag_bucket, io, io_state):
    it = iter(refs)
    cur_ref, bucket_ref, relb_ref = next(it), next(it), next(it)
    q_ref, kc_ref, vc_ref = next(it), next(it), next(it)
    if not first:
        o_in_ref, lse_in_ref = next(it), next(it)
    if last:
        bz_ref, go_ref = next(it), next(it)
        y_ref = next(it)
    else:
        o_out_ref, lse_out_ref = next(it), next(it)
    bias_ref, kp_ref, vp_ref = next(it), next(it), next(it)
    if last:
        yb_ref, st_ref = next(it), next(it)
    else:
        lse_ref = next(it)

    n = pl.program_id(2)
    first_step = (pl.program_id(0) == 0) & (pl.program_id(1) == 0) & (n == 0)
    prev_io = _TileIO((), BLK, (BLK,))

    @pl.when(first_step)
    def _():
        bk = bucket_ref[...]
        for h in range(n_heads):
            tab = jnp.zeros((BLK, BLK), F32)
            for b in range(NUM_BUCKETS):
                tab = jnp.where(bk == b, relb_ref[b, h], tab)
            bias_ref[h] = tab
        if not last:
            lse_ref[...] = jnp.zeros(lse_ref.shape, F32)

    def natural_rows(tile):
        st_ref[...] = tile
        per = BLK // MAX_DIL
        rows = [st_ref[pl.ds((MAX_DIL // 2) * per * (v % 2) + v // 2, 8, stride=per), :]
                for v in range(BLK // 8)]
        return jnp.concatenate(rows, axis=0)

    nt = (((1,), (1,)), ((), ()))

    def heads(g, has_prev):
        cur_io, st_io = io.sub(g), io_state.sub(g)
        kp_io, kp_src, vp_src = (io.sub(g - 1), kc_ref, vc_ref) if g else (prev_io, kp_ref, vp_ref)
        in_cur = cur_ref[...] != 0
        ones = jnp.ones((BLK, HEAD_DIM), BF16)
        if has_prev:
            eye = (lax.broadcasted_iota(jnp.int32, (BLK, BLK), 0)
                   == lax.broadcasted_iota(jnp.int32, (BLK, BLK), 1))
        if not first:
            lse_in = st_io.load(lse_in_ref, slice(0, BLK))
            if last:
                lse_in = natural_rows(lse_in)

        def logits(h):
            sl = slice(h * HEAD_DIM, (h + 1) * HEAD_DIM)
            keys = cur_io.load(kc_ref, sl)
            if has_prev:
                keys = jnp.concatenate([kp_io.load(kp_src, sl), keys], axis=0)
            return lax.dot_general(cur_io.load(q_ref, sl), keys, nt, preferred_element_type=F32)

        def softmax(h, s2):
            if has_prev:
                s_p, s_c = s2[:, :BLK], s2[:, BLK:]
                s = jnp.where(in_cur, s_c, s_p) + bias_ref[h]
                far = jnp.where(eye, s_p + relb_ref[diag_bucket, h], NEG_INF)
                mx = jnp.max(jnp.maximum(s, far), axis=-1, keepdims=True)
            else:
                s = jnp.where(in_cur, s2 + bias_ref[h], NEG_INF)
                mx = jnp.max(s, axis=-1, keepdims=True)
            lse_old = None
            if not first:
                lse_old = jnp.broadcast_to(lse_in[:, h:h + 1], (BLK, HEAD_DIM))
                mx = jnp.maximum(mx, lse_old)
            e = jnp.exp(s - mx)
            if has_prev:
                e_far = jnp.exp(far - mx)
                probs = jnp.concatenate([jnp.where(in_cur, e_far, e), jnp.where(in_cur, e, 0.0)], axis=1)
            else:
                probs = e
            return probs.astype(BF16), mx, lse_old

        def values(h, probs, mx, lse_old):
            sl = slice(h * HEAD_DIM, (h + 1) * HEAD_DIM)
            vals = jnp.concatenate([cur_io.load(vc_ref, sl), ones], axis=1)
            if has_prev:
                vals = jnp.concatenate([jnp.concatenate([kp_io.load(vp_src, sl), ones], axis=1), vals], axis=0)
            both = jnp.dot(probs, vals, preferred_element_type=F32)
            num, den = both[:, :HEAD_DIM], both[:, HEAD_DIM:]
            if first:
                total = den
                o = num / total
            else:
                o_old = st_io.load(o_in_ref, sl)
                if last:
                    o_old = natural_rows(o_old)
                w_old = jnp.exp(lse_old - mx)
                total = w_old + den
                o = (o_old * w_old + num) / total
            if last:
                yb_ref[g * BLK:(g + 1) * BLK, sl] = o
            else:
                st_io.store(o_out_ref, sl, o)
                lse = mx + jnp.log(total)
                lse_ref[g, :, h:h + 1] = lse[:, h:h + 1]

        ahead = {h: logits(h) for h in range(min(QK_LOOKAHEAD, n_heads))}
        soft = softmax(0, ahead.pop(0))
        for h in range(n_heads):
            if h + QK_LOOKAHEAD < n_heads:
                ahead[h + QK_LOOKAHEAD] = logits(h + QK_LOOKAHEAD)
            nxt = softmax(h + 1, ahead.pop(h + 1)) if h + 1 < n_heads else None
            values(h, *soft)
            soft = nxt

    @pl.when(n == 0)
    def _():
        heads(0, False)

    @pl.when(n > 0)
    def _():
        heads(0, True)

    for g in range(1, n_sub):
        heads(g, True)

    whole = slice(0, kp_ref.shape[1])
    kp_ref[...] = io.sub(n_sub - 1).load(kc_ref, whole)
    vp_ref[...] = io.sub(n_sub - 1).load(vc_ref, whole)
    if last:
        yb = yb_ref[...]
        ms = jnp.mean(yb * yb, axis=-1, keepdims=True)
        y_ref[...] = (yb * lax.rsqrt(ms + EPS) * go_ref[...] * bz_ref[...].astype(F32)).astype(BF16)
    else:
        for g in range(n_sub):
            io_state.sub(g).store(lse_out_ref, slice(0, BLK), lse_ref[g])


def _attn_pass(dil, qkv_rm, proj, rel_bias, state, g_out_b, *, batch, seq, w_b):
    first = state is None
    last = dil == 1
    n_heads = w_b // HEAD_DIM
    assert 2 * n_heads <= BLK
    sub = seq // MAX_DIL
    nb = seq // dil // BLK
    rep = MAX_DIL // dil
    runs = BLK // rep
    n_sub = min(ATTN_BLOCKS_PER_STEP, nb)
    assert nb % n_sub == 0
    col_q, col_k, col_v = 0, 1, 2

    if last:
        pos = np.arange(BLK)
        io = _TileIO((), BLK, (BLK,))
        io_state = _TileIO((0, slice(None)), BLK // MAX_DIL, (MAX_DIL, BLK // MAX_DIL))
        blk = (n_sub * BLK, w_b)
        src = proj
        cur = lambda seg: (lambda b, r, n: (b * (nb // n_sub) + n, seg))
        col_q, col_k, col_v = SEG_Q, SEG_K, SEG_V
        st_blk = lambda w: (1, MAX_DIL, n_sub * BLK // MAX_DIL, w)
        st_idx = lambda b, r, n: (b, 0, n, 0)
        view = lambda a: a
    else:
        rho = np.arange(BLK)
        pos = rep * (rho % runs) + rho // runs
        io = io_state = _TileIO((0, slice(None), 0), runs, (rep, runs))
        blk = (1, rep, 1, n_sub * runs, w_b)
        src = qkv_rm.reshape(batch, rep, dil, sub, 3 * w_b)
        cur = lambda col: (lambda b, r, n: (b, 0, r, n, col))
        st_blk = lambda w: (1, rep, 1, n_sub * runs, w)
        st_idx = lambda b, r, n: (b, 0, r, n, 0)
        view = lambda a: a.reshape(batch, rep, dil, sub, a.shape[-1])

    in_cur, bucket = _band_tables(dil, pos)
    diag_bucket = int(_rel_bucket_np(np.array([BLK * dil]))[0])
    const = lambda: pl.BlockSpec((BLK, BLK), lambda b, r, n: (0, 0))
    in_specs = [
        const(), const(), pl.BlockSpec(memory_space=pltpu.SMEM),
        pl.BlockSpec(blk, cur(col_q)), pl.BlockSpec(blk, cur(col_k)), pl.BlockSpec(blk, cur(col_v)),
    ]
    args = [jnp.asarray(in_cur), jnp.asarray(bucket), rel_bias, src, src, src]
    acc_spec = pl.BlockSpec(st_blk(w_b), st_idx)
    ml_spec = pl.BlockSpec(st_blk(BLK), st_idx)
    if not first:
        acc, ml = state
        in_specs += [acc_spec, ml_spec]
        args += [view(acc), view(ml)]
    scratch = [pltpu.VMEM((n_heads, BLK, BLK), F32),
               pltpu.VMEM((BLK, w_b), BF16), pltpu.VMEM((BLK, w_b), BF16)]
    if last:
        in_specs += [pl.BlockSpec(blk, cur(SEG_BZ)), pl.BlockSpec((1, w_b), lambda b, r, n: (0, 0))]
        args += [proj, g_out_b]
        out_specs = pl.BlockSpec(blk, cur(0))
        out_shape = jax.ShapeDtypeStruct((batch * seq, w_b), BF16)
        scratch += [pltpu.VMEM((n_sub * BLK, w_b), F32), pltpu.VMEM((BLK, HEAD_DIM), F32)]
    else:
        out_specs = [acc_spec, ml_spec]
        scratch.append(pltpu.VMEM((n_sub, BLK, BLK), F32))
        out_shape = [jax.ShapeDtypeStruct((batch, rep, dil, sub, w_b), F32),
                     jax.ShapeDtypeStruct((batch, rep, dil, sub, BLK), F32)]
    kern = functools.partial(_attn_kernel, n_heads=n_heads, n_sub=n_sub, first=first, last=last,
                             diag_bucket=diag_bucket, io=io, io_state=io_state)
    out = pl.pallas_call(
        kern,
        grid=(batch, dil, nb // n_sub),
        in_specs=in_specs,
        out_specs=out_specs,
        out_shape=out_shape,
        scratch_shapes=scratch,
        compiler_params=_params(("arbitrary", "arbitrary", "arbitrary")),
        name=f"attn_d{dil}",
    )(*args)
    if last:
        return out
    acc, ml = out
    return (acc.reshape(batch, MAX_DIL, sub, w_b), ml.reshape(batch, MAX_DIL, sub, BLK))


def _out_proj_kernel(x_ref, ya_ref, yb_ref, wa_ref, wb_ref, h_ref):
    def product(c):
        cols = slice(c * MXU_COLS, (c + 1) * MXU_COLS)
        return (jnp.dot(ya_ref[...], wa_ref[:, cols], preferred_element_type=F32)
                + jnp.dot(yb_ref[...], wb_ref[:, cols], preferred_element_type=F32))

    n_sub = h_ref.shape[1] // MXU_COLS
    acc = product(0)
    for c in range(n_sub):
        nxt = product(c + 1) if c + 1 < n_sub else None
        cols = slice(c * MXU_COLS, (c + 1) * MXU_COLS)
        h_ref[:, cols] = x_ref[:, cols] + acc
        acc = nxt


def _out_proj(x2, y_a, y_b, w_out_bf):
    m, d = x2.shape
    w_a = y_a.shape[1]
    w_b = y_b.shape[1]
    assert w_a == w_b
    tm = _pick(m, 1024)
    tn = _pick(d, 512)
    assert tn % MXU_COLS == 0
    return pl.pallas_call(
        _out_proj_kernel,
        grid=(m // tm, d // tn),
        in_specs=[
            pl.BlockSpec((tm, tn), lambda i, j: (i, j)),
            pl.BlockSpec((tm, w_a), lambda i, j: (i, 0)),
            pl.BlockSpec((tm, w_b), lambda i, j: (i, 0)),
            pl.BlockSpec((w_a, tn), lambda i, j: (0, j)),
            pl.BlockSpec((w_b, tn), lambda i, j: (1, j)),
        ],
        out_specs=pl.BlockSpec((tm, tn), lambda i, j: (i, j)),
        out_shape=jax.ShapeDtypeStruct((m, d), F32),
        compiler_params=_params(("parallel", "arbitrary")),
        name="out_proj",
    )(x2, y_a, y_b, w_out_bf, w_out_bf)


def _ple_kernel(h_ref, g_ref, wg_ref, p_ref, wu_ref, o_ref, hn_ref):
    j = pl.program_id(1)

    @pl.when(j == 0)
    def _():
        _normalise_rows(h_ref, g_ref, hn_ref, 16)

    tn = o_ref.shape[1]
    n_sub = tn // MXU_COLS
    p_bf = p_ref[...].astype(BF16)

    def product(c):
        return jnp.dot(hn_ref[...], wg_ref[:, c * MXU_COLS:(c + 1) * MXU_COLS], preferred_element_type=F32)

    acc = product(0)
    for c in range(n_sub):
        nxt = product(c + 1) if c + 1 < n_sub else None
        cols = slice(c * MXU_COLS, (c + 1) * MXU_COLS)
        up = jnp.dot(p_bf, wu_ref[:, cols], preferred_element_type=F32)
        h_cols = pl.ds(pl.multiple_of(j * tn + c * MXU_COLS, MXU_COLS), MXU_COLS)
        o_ref[:, cols] = h_ref[:, h_cols] + jax.nn.sigmoid(acc) * up
        acc = nxt


def _ple(h, g_ple, w_gate_bf, p2, w_up_bf):
    m, d = h.shape
    d_ple = p2.shape[1]
    tm = _pick(m, 512)
    tn = _pick(d, 1024)
    assert tn % MXU_COLS == 0
    return pl.pallas_call(
        _ple_kernel,
        grid=(m // tm, d // tn),
        in_specs=[
            pl.BlockSpec((tm, d), lambda i, j: (i, 0)),
            pl.BlockSpec((1, d), lambda i, j: (0, 0)),
            pl.BlockSpec((d, tn), lambda i, j: (0, j)),
            pl.BlockSpec((tm, d_ple), lambda i, j: (i, 0)),
            pl.BlockSpec((d_ple, tn), lambda i, j: (0, j)),
        ],
        out_specs=pl.BlockSpec((tm, tn), lambda i, j: (i, j)),
        out_shape=jax.ShapeDtypeStruct((m, d), F32),
        scratch_shapes=[pltpu.VMEM((tm, d), BF16)],
        compiler_params=_params(("parallel", "arbitrary")),
        name="ple",
    )(h, g_ple, w_gate_bf, p2, w_up_bf)


def kernel(x, p, g_pre, w_in, w_s, b_s, ln_v_g, ln_v_b, g_q, g_k, rel_bias, g_out_a, g_out_b, w_out, g_ple, w_ple_gate, w_ple_up):
    batch, seq, d = x.shape
    depth = p.shape[0]
    w_a = ln_v_g.shape[-1]
    w_b = g_out_b.shape[-1]
    d_in = w_in.shape[-1]
    assert w_a == w_b and d_in == N_SEG * w_a, "segments of the combined projection must be equally wide"
    assert seq % (MAX_DIL * BLK) == 0 and all(win // dil == BLK for win, dil in DILATED)
    assert sorted(dil for _, dil in DILATED) == [1, 4, MAX_DIL]
    m = batch * seq
    x2 = x.reshape(m, d)
    for i in range(depth):
        proj, qkv_rm, (w_out_bf, w_gate_bf) = _in_proj(
            x2, g_pre[i][None], w_in[i].astype(BF16), g_q[i][None], g_k[i][None],
            [w_out[i], w_ple_gate[i]], batch=batch, seq=seq)
        y_a = _gmlp(proj, w_s[i], b_s[i].T, ln_v_g[i][None], ln_v_b[i][None], g_out_a[i][None])
        state = None
        for dil in sorted((dil for _, dil in DILATED), reverse=True):
            state = _attn_pass(dil, qkv_rm, proj, rel_bias, state, g_out_b[i][None],
                               batch=batch, seq=seq, w_b=w_b)
        y_b = state
        h = _out_proj(x2, y_a, y_b, w_out_bf)
        x2 = _ple(h, g_ple[i][None], w_gate_bf, p[i].reshape(m, -1), w_ple_up[i].astype(BF16))
    return x2.reshape(batch, seq, d)
```

```python
import functools
import math

import numpy as np
import jax
import jax.numpy as jnp
from jax import lax
from jax.experimental import pallas as pl
from jax.experimental.pallas import tpu as pltpu

HEAD_DIM = 128
CHUNK = 128
BLK = 128
QK_LOOKAHEAD = 3
ATTN_BLOCKS_PER_STEP = 4
GMLP_CHUNKS_PER_STEP = 2
DILATED = ((128, 1), (512, 4), (2048, 16))
MAX_DIL = 16
NUM_BUCKETS = 32
MAX_DISTANCE = 2048
EPS = 1e-6
NEG_INF = -1e30
N_SEG = 7
SEG_Q, SEG_K, SEG_V, SEG_BZ = 3, 4, 5, 6

V7X_VMEM_LIMIT_BYTES = 56 * 1024 * 1024
MXU_COLS = 256
SLAB_PITCH = 24

BF16 = jnp.bfloat16
F32 = jnp.float32


def _pick(n, pref):
    t = min(n, pref)
    while n % t:
        t //= 2
    return t


def _params(sem):
    return pltpu.CompilerParams(dimension_semantics=sem,
                                vmem_limit_bytes=V7X_VMEM_LIMIT_BYTES)


def _gelu(v):
    return 0.5 * v * (1.0 + lax.erf(v * (1.0 / math.sqrt(2.0))))


def _silu(v):
    return v * jax.nn.sigmoid(v)


def _normalise_rows(x_ref, g_ref, hn_ref, rows):
    tm = x_ref.shape[0]

    def body(c, carry):
        r = pl.ds(pl.multiple_of(c * rows, rows), rows)
        xc = x_ref[r, :]
        ms = jnp.mean(xc * xc, axis=-1, keepdims=True)
        hn_ref[r, :] = (xc * lax.rsqrt(ms + EPS) * g_ref[...]).astype(BF16)
        return carry

    lax.fori_loop(0, tm // rows, body, 0, unroll=4)


def _in_proj_kernel(x_ref, g_ref, w_ref, gq_ref, gk_ref, *rest, tiles_per_seg, n_cast):
    cast_in, (o_ref, op_ref), cast_out = rest[:n_cast], rest[n_cast:n_cast + 2], rest[n_cast + 2:2 * n_cast + 2]
    hn_ref, slab_ref = rest[2 * n_cast + 2:]
    j = pl.program_id(1)

    @pl.when(j == 0)
    def _():
        _normalise_rows(x_ref, g_ref, hn_ref, 16)

    seg = j // tiles_per_seg
    tm, tn = o_ref.shape
    n_sub = tn // MXU_COLS

    def sub_cols(c):
        return slice(c * MXU_COLS, (c + 1) * MXU_COLS)

    def product(c):
        return jnp.dot(hn_ref[...], w_ref[:, sub_cols(c)], preferred_element_type=F32)

    def tile(epilogue, residue_major=False):
        acc = product(0)
        for src_ref, dst_ref in zip(cast_in, cast_out):
            dst_ref[...] = src_ref[...].astype(BF16)
        for c in range(n_sub):
            nxt = product(c + 1) if c + 1 < n_sub else None
            lanes = MXU_COLS // HEAD_DIM
            y = epilogue(acc)
            o_ref[:, sub_cols(c)] = y.astype(BF16)
            if residue_major:
                for k in range(lanes):
                    for a in range(tm // MAX_DIL):
                        slab_ref[c * lanes + k, a * SLAB_PITCH:a * SLAB_PITCH + MAX_DIL, :] = (
                            y[a * MAX_DIL:(a + 1) * MAX_DIL, k * HEAD_DIM:(k + 1) * HEAD_DIM])
                for r in range(MAX_DIL):
                    rows = [slab_ref[c * lanes + k, pl.ds(r, tm // MAX_DIL, stride=SLAB_PITCH), :]
                            for k in range(lanes)]
                    op_ref[0, r, :, sub_cols(c)] = jnp.concatenate(rows, axis=1).astype(BF16)
            acc = nxt

    def head_norm(g, scale):
        def epilogue(acc):
            out = []
            for c in range(MXU_COLS // HEAD_DIM):
                blk = acc[:, c * HEAD_DIM:(c + 1) * HEAD_DIM]
                ms = jnp.mean(blk * blk, axis=-1, keepdims=True)
                y = blk * lax.rsqrt(ms + EPS) * g
                out.append(y if scale is None else y * scale)
            return jnp.concatenate(out, axis=1)
        return epilogue

    @pl.when(seg <= 1)
    def _():
        tile(_gelu)

    @pl.when((seg == 2) | (seg == SEG_BZ))
    def _():
        tile(_silu)

    @pl.when(seg == SEG_Q)
    def _():
        tile(head_norm(gq_ref[...], HEAD_DIM ** -0.5), residue_major=True)

    @pl.when(seg == SEG_K)
    def _():
        tile(head_norm(gk_ref[...], None), residue_major=True)

    @pl.when(seg == SEG_V)
    def _():
        tile(lambda acc: acc, residue_major=True)


def _in_proj(x2, g_pre, w_in_bf, g_q, g_k, later_weights, *, batch, seq):
    m, d = x2.shape
    d_in = w_in_bf.shape[1]
    seg_w = d_in // N_SEG
    tm = _pick(seq, 512)
    tn = _pick(seg_w, 1024)
    assert tn % MXU_COLS == 0
    tps = seg_w // tn
    assert tm % (MAX_DIL * 16) == 0, "residue-major runs must cover whole bf16 sublane tiles"
    blocks_per_seq = seq // tm
    sub = seq // MAX_DIL
    runs = tm // MAX_DIL
    kern = functools.partial(_in_proj_kernel, tiles_per_seg=tps, n_cast=len(later_weights))
    n_j = d_in // tn
    steps = (m // tm) * n_j

    def perm_idx(i, j):
        jj = jnp.clip(j - SEG_Q * tps, 0, 3 * tps - 1)
        return (i // blocks_per_seq, 0, i % blocks_per_seq, jj)

    def cast_spec(w):
        rows = 16
        while w.shape[0] % rows or w.shape[0] // rows > steps:
            rows += 16
        last = w.shape[0] // rows - 1
        return pl.BlockSpec((rows, w.shape[1]), lambda i, j: (jnp.minimum(i * n_j + j, last), 0))

    out = pl.pallas_call(
        kern,
        grid=(m // tm, n_j),
        in_specs=[
            pl.BlockSpec((tm, d), lambda i, j: (i, 0)),
            pl.BlockSpec((1, d), lambda i, j: (0, 0)),
            pl.BlockSpec((d, tn), lambda i, j: (0, j)),
            pl.BlockSpec((1, HEAD_DIM), lambda i, j: (0, 0)),
            pl.BlockSpec((1, HEAD_DIM), lambda i, j: (0, 0)),
        ] + [cast_spec(w) for w in later_weights],
        out_specs=[
            pl.BlockSpec((tm, tn), lambda i, j: (i, j)),
            pl.BlockSpec((1, MAX_DIL, runs, tn), perm_idx),
        ] + [cast_spec(w) for w in later_weights],
        out_shape=[
            jax.ShapeDtypeStruct((m, d_in), BF16),
            jax.ShapeDtypeStruct((batch, MAX_DIL, sub, 3 * seg_w), BF16),
        ] + [jax.ShapeDtypeStruct(w.shape, BF16) for w in later_weights],
        scratch_shapes=[pltpu.VMEM((tm, d), BF16), pltpu.VMEM((tn // HEAD_DIM, runs * SLAB_PITCH, HEAD_DIM), F32)],
        compiler_params=_params(("arbitrary", "arbitrary")),
        name="in_proj",
    )(x2, g_pre, w_in_bf, g_q, g_k, *later_weights)
    return out[0], out[1], out[2:]


def _gmlp_rows(au_ref, av_ref, az_ref, ws_ref, bs_ref, lg_ref, lb_ref, go_ref, ya_ref):
    n_chunks = av_ref.shape[0] // CHUNK
    av = av_ref[...].astype(F32)
    mu = jnp.mean(av, axis=-1, keepdims=True)
    xc = av - mu
    var = jnp.mean(xc * xc, axis=-1, keepdims=True)
    avn = xc * lax.rsqrt(var + EPS) * lg_ref[...] + lb_ref[...]

    row = lax.broadcasted_iota(jnp.int32, (CHUNK, CHUNK), 0)
    col = lax.broadcasted_iota(jnp.int32, (CHUNK, CHUNK), 1)
    causal = col <= row
    n_groups = ws_ref.shape[0]
    for g in range(n_groups):
        sl = slice(g * HEAD_DIM, (g + 1) * HEAD_DIM)
        wm = jnp.where(causal, ws_ref[g], 0.0).astype(BF16)
        rhs = jnp.concatenate([avn[c * CHUNK:(c + 1) * CHUNK, sl] for c in range(n_chunks)], axis=1)
        z = jnp.dot(wm, rhs.astype(BF16), preferred_element_type=F32)
        for c in range(n_chunks):
            rows = slice(c * CHUNK, (c + 1) * CHUNK)
            zc = z[:, c * HEAD_DIM:(c + 1) * HEAD_DIM] + bs_ref[:, g:g + 1]
            ya_ref[rows, sl] = au_ref[rows, sl].astype(F32) * zc
    ya = ya_ref[...]
    ms = jnp.mean(ya * ya, axis=-1, keepdims=True)
    return (ya * lax.rsqrt(ms + EPS) * go_ref[...] * az_ref[...].astype(F32)).astype(BF16)


def _rel_bucket_np(dist):
    max_exact = NUM_BUCKETS // 2
    d = np.maximum(dist, 1).astype(np.float32)
    large = max_exact + (np.log(d / np.float32(max_exact)) / np.float32(math.log(MAX_DISTANCE / max_exact))
                         * np.float32(NUM_BUCKETS - max_exact)).astype(np.int32)
    large = np.minimum(large, NUM_BUCKETS - 1)
    return np.where(dist < max_exact, dist, large).astype(np.int32)


def _band_tables(dil, pos):
    i_q = pos[:, None]
    i_k = pos[None, :]
    in_cur = i_k <= i_q
    delta = np.where(in_cur, i_q - i_k, BLK + i_q - i_k)
    return in_cur.astype(np.int32), _rel_bucket_np(delta * dil)


class _TileIO:
    def __init__(self, prefix, run, rows_shape, g=0):
        self.prefix = prefix
        self.run = run
        self.rows_shape = rows_shape
        self.idx = prefix + (slice(g * run, (g + 1) * run),)

    def sub(self, g):
        return _TileIO(self.prefix, self.run, self.rows_shape, g)

    def load(self, ref, sl):
        return ref[self.idx + (sl,)].reshape(BLK, sl.stop - sl.start)

    def store(self, ref, sl, val):
        ref[self.idx + (sl,)] = val.reshape(self.rows_shape + (sl.stop - sl.start,))


def _attn_kernel(*refs, n_heads, n_sub, first, last, diag_bucket, io, io_state):
    it = iter(refs)
    cur_ref, bucket_ref, relb_ref = next(it), next(it), next(it)
    q_ref, kc_ref, vc_ref = next(it), next(it), next(it)
    if not first:
        o_in_ref, lse_in_ref = next(it), next(it)
    if last:
        bz_ref, go_ref = next(it), next(it)
        y_ref = next(it)
    else:
        o_out_ref, lse_out_ref = next(it), next(it)
    bias_ref, kp_ref, vp_ref = next(it), next(it), next(it)
    if last:
        yb_ref, st_ref = next(it), next(it)
    else:
        lse_ref = next(it)

    n = pl.program_id(2)
    first_step = (pl.program_id(0) == 0) & (pl.program_id(1) == 0) & (n == 0)
    prev_io = _TileIO((), BLK, (BLK,))

    @pl.when(first_step)
    def _():
        bk = bucket_ref[...]
        for h in range(n_heads):
            tab = jnp.zeros((BLK, BLK), F32)
            for b in range(NUM_BUCKETS):
                tab = jnp.where(bk == b, relb_ref[b, h], tab)
            bias_ref[h] = tab
        if not last:
            lse_ref[...] = jnp.zeros(lse_ref.shape, F32)

    def natural_rows(tile):
        st_ref[...] = tile
        per = BLK // MAX_DIL
        rows = [st_ref[pl.ds((MAX_DIL // 2) * per * (v % 2) + v // 2, 8, stride=per), :]
                for v in range(BLK // 8)]
        return jnp.concatenate(rows, axis=0)

    nt = (((1,), (1,)), ((), ()))

    def heads(g, has_prev):
        cur_io, st_io = io.sub(g), io_state.sub(g)
        kp_io, kp_src, vp_src = (io.sub(g - 1), kc_ref, vc_ref) if g else (prev_io, kp_ref, vp_ref)
        in_cur = cur_ref[...] != 0
        ones = jnp.ones((BLK, HEAD_DIM), BF16)
        if has_prev:
            eye = (lax.broadcasted_iota(jnp.int32, (BLK, BLK), 0)
                   == lax.broadcasted_iota(jnp.int32, (BLK, BLK), 1))
        if not first:
            lse_in = st_io.load(lse_in_ref, slice(0, BLK))
            if last:
                lse_in = natural_rows(lse_in)

        def logits(h):
            sl = slice(h * HEAD_DIM, (h + 1) * HEAD_DIM)
            keys = cur_io.load(kc_ref, sl)
            if has_prev:
                keys = jnp.concatenate([kp_io.load(kp_src, sl), keys], axis=0)
            return lax.dot_general(cur_io.load(q_ref, sl), keys, nt, preferred_element_type=F32)

        def softmax(h, s2):
            if has_prev:
                s_p, s_c = s2[:, :BLK], s2[:, BLK:]
                s = jnp.where(in_cur, s_c, s_p) + bias_ref[h]
                far = jnp.where(eye, s_p + relb_ref[diag_bucket, h], NEG_INF)
                mx = jnp.max(jnp.maximum(s, far), axis=-1, keepdims=True)
            else:
                s = jnp.where(in_cur, s2 + bias_ref[h], NEG_INF)
                mx = jnp.max(s, axis=-1, keepdims=True)
            lse_old = None
            if not first:
                lse_old = jnp.broadcast_to(lse_in[:, h:h + 1], (BLK, HEAD_DIM))
                mx = jnp.maximum(mx, lse_old)
            e = jnp.exp(s - mx)
            if has_prev:
                e_far = jnp.exp(far - mx)
                probs = jnp.concatenate([jnp.where(in_cur, e_far, e), jnp.where(in_cur, e, 0.0)], axis=1)
            else:
                probs = e
            return probs.astype(BF16), mx, lse_old

        def values(h, probs, mx, lse_old):
            sl = slice(h * HEAD_DIM, (h + 1) * HEAD_DIM)
            vals = jnp.concatenate([cur_io.load(vc_ref, sl), ones], axis=1)
            if has_prev:
                vals = jnp.concatenate([jnp.concatenate([kp_io.load(vp_src, sl), ones], axis=1), vals], axis=0)
            both = jnp.dot(probs, vals, preferred_element_type=F32)
            num, den = both[:, :HEAD_DIM], both[:, HEAD_DIM:]
            if first:
                total = den
                o = num / total
            else:
                o_old = st_io.load(o_in_ref, sl)
                if last:
                    o_old = natural_rows(o_old)
                w_old = jnp.exp(lse_old - mx)
                total = w_old + den
                o = (o_old * w_old + num) / total
            if last:
                yb_ref[g * BLK:(g + 1) * BLK, sl] = o
            else:
                st_io.store(o_out_ref, sl, o)
                lse = mx + jnp.log(total)
                lse_ref[g, :, h:h + 1] = lse[:, h:h + 1]

        ahead = {h: logits(h) for h in range(min(QK_LOOKAHEAD, n_heads))}
        soft = softmax(0, ahead.pop(0))
        for h in range(n_heads):
            if h + QK_LOOKAHEAD < n_heads:
                ahead[h + QK_LOOKAHEAD] = logits(h + QK_LOOKAHEAD)
            nxt = softmax(h + 1, ahead.pop(h + 1)) if h + 1 < n_heads else None
            values(h, *soft)
            soft = nxt

    @pl.when(n == 0)
    def _():
        heads(0, False)

    @pl.when(n > 0)
    def _():
        heads(0, True)

    for g in range(1, n_sub):
        heads(g, True)

    whole = slice(0, kp_ref.shape[1])
    kp_ref[...] = io.sub(n_sub - 1).load(kc_ref, whole)
    vp_ref[...] = io.sub(n_sub - 1).load(vc_ref, whole)
    if last:
        yb = yb_ref[...]
        ms = jnp.mean(yb * yb, axis=-1, keepdims=True)
        y_ref[...] = (yb * lax.rsqrt(ms + EPS) * go_ref[...] * bz_ref[...].astype(F32)).astype(BF16)
    else:
        for g in range(n_sub):
            io_state.sub(g).store(lse_out_ref, slice(0, BLK), lse_ref[g])


def _attn_pass(dil, qkv_rm, proj, rel_bias, state, g_out_b, *, batch, seq, w_b):
    first = state is None
    last = dil == 1
    n_heads = w_b // HEAD_DIM
    assert 2 * n_heads <= BLK
    sub = seq // MAX_DIL
    nb = seq // dil // BLK
    rep = MAX_DIL // dil
    runs = BLK // rep
    n_sub = min(ATTN_BLOCKS_PER_STEP, nb)
    assert nb % n_sub == 0
    col_q, col_k, col_v = 0, 1, 2

    if last:
        pos = np.arange(BLK)
        io = _TileIO((), BLK, (BLK,))
        io_state = _TileIO((0, slice(None)), BLK // MAX_DIL, (MAX_DIL, BLK // MAX_DIL))
        blk = (n_sub * BLK, w_b)
        src = proj
        cur = lambda seg: (lambda b, r, n: (b * (nb // n_sub) + n, seg))
        col_q, col_k, col_v = SEG_Q, SEG_K, SEG_V
        st_blk = lambda w: (1, MAX_DIL, n_sub * BLK // MAX_DIL, w)
        st_idx = lambda b, r, n: (b, 0, n, 0)
        view = lambda a: a
    else:
        rho = np.arange(BLK)
        pos = rep * (rho % runs) + rho // runs
        io = io_state = _TileIO((0, slice(None), 0), runs, (rep, runs))
        blk = (1, rep, 1, n_sub * runs, w_b)
        src = qkv_rm.reshape(batch, rep, dil, sub, 3 * w_b)
        cur = lambda col: (lambda b, r, n: (b, 0, r, n, col))
        st_blk = lambda w: (1, rep, 1, n_sub * runs, w)
        st_idx = lambda b, r, n: (b, 0, r, n, 0)
        view = lambda a: a.reshape(batch, rep, dil, sub, a.shape[-1])

    in_cur, bucket = _band_tables(dil, pos)
    diag_bucket = int(_rel_bucket_np(np.array([BLK * dil]))[0])
    const = lambda: pl.BlockSpec((BLK, BLK), lambda b, r, n: (0, 0))
    in_specs = [
        const(), const(), pl.BlockSpec(memory_space=pltpu.SMEM),
        pl.BlockSpec(blk, cur(col_q)), pl.BlockSpec(blk, cur(col_k)), pl.BlockSpec(blk, cur(col_v)),
    ]
    args = [jnp.asarray(in_cur), jnp.asarray(bucket), rel_bias, src, src, src]
    acc_spec = pl.BlockSpec(st_blk(w_b), st_idx)
    ml_spec = pl.BlockSpec(st_blk(BLK), st_idx)
    if not first:
        acc, ml = state
        in_specs += [acc_spec, ml_spec]
        args += [view(acc), view(ml)]
    scratch = [pltpu.VMEM((n_heads, BLK, BLK), F32),
               pltpu.VMEM((BLK, w_b), BF16), pltpu.VMEM((BLK, w_b), BF16)]
    if last:
        in_specs += [pl.BlockSpec(blk, cur(SEG_BZ)), pl.BlockSpec((1, w_b), lambda b, r, n: (0, 0))]
        args += [proj, g_out_b]
        out_specs = pl.BlockSpec(blk, cur(0))
        out_shape = jax.ShapeDtypeStruct((batch * seq, w_b), BF16)
        scratch += [pltpu.VMEM((n_sub * BLK, w_b), F32), pltpu.VMEM((BLK, HEAD_DIM), F32)]
    else:
        out_specs = [acc_spec, ml_spec]
        scratch.append(pltpu.VMEM((n_sub, BLK, BLK), F32))
        out_shape = [jax.ShapeDtypeStruct((batch, rep, dil, sub, w_b), F32),
                     jax.ShapeDtypeStruct((batch, rep, dil, sub, BLK), F32)]
    kern = functools.partial(_attn_kernel, n_heads=n_heads, n_sub=n_sub, first=first, last=last,
                             diag_bucket=diag_bucket, io=io, io_state=io_state)
    out = pl.pallas_call(
        kern,
        grid=(batch, dil, nb // n_sub),
        in_specs=in_specs,
        out_specs=out_specs,
        out_shape=out_shape,
        scratch_shapes=scratch,
        compiler_params=_params(("arbitrary", "arbitrary", "arbitrary")),
        name=f"attn_d{dil}",
    )(*args)
    if last:
        return out
    acc, ml = out
    return (acc.reshape(batch, MAX_DIL, sub, w_b), ml.reshape(batch, MAX_DIL, sub, BLK))


def _gmlp_kernel(au_ref, av_ref, az_ref, ws_ref, bs_ref, lg_ref, lb_ref, go_ref, o_ref, tmp_ref):
    o_ref[...] = _gmlp_rows(au_ref, av_ref, az_ref, ws_ref, bs_ref, lg_ref, lb_ref, go_ref, tmp_ref)


def _out_proj_kernel(x_ref, ya0_ref, yb_ref, wa_ref, wb_ref, au_ref, av_ref, az_ref, ws_ref, bs_ref, lg_ref,
                     lb_ref, go_ref, h_ref, ya_ref, tmp_ref, *, mix_steps):
    i, j = pl.program_id(0), pl.program_id(1)
    rows = au_ref.shape[0]
    n_sub = h_ref.shape[1] // MXU_COLS

    @pl.when((i == 0) & (j == 0))
    def _():
        ya_ref[0] = ya0_ref[...]

    def mix():
        y = _gmlp_rows(au_ref, av_ref, az_ref, ws_ref, bs_ref, lg_ref, lb_ref, go_ref, tmp_ref)
        ya_ref[(i + 1) % 2, pl.ds(pl.multiple_of(j * rows, rows), rows), :] = y

    def product(c):
        cols = slice(c * MXU_COLS, (c + 1) * MXU_COLS)
        return (jnp.dot(ya_ref[i % 2], wa_ref[:, cols], preferred_element_type=F32)
                + jnp.dot(yb_ref[...], wb_ref[:, cols], preferred_element_type=F32))

    def project(with_mix):
        acc = product(0)
        for c in range(n_sub):
            nxt = product(c + 1) if c + 1 < n_sub else None
            cols = slice(c * MXU_COLS, (c + 1) * MXU_COLS)
            h_ref[:, cols] = x_ref[:, cols] + acc
            acc = nxt
        if with_mix:
            mix()

    has_mix = (i + 1 < pl.num_programs(0)) & (j < mix_steps)

    @pl.when(has_mix)
    def _():
        project(True)

    @pl.when(jnp.logical_not(has_mix))
    def _():
        project(False)


def _out_proj(x2, proj, y_b, w_out_bf, w_s, b_s_t, ln_g, ln_b, g_out_a):
    m, d = x2.shape
    w_a = proj.shape[1] // N_SEG
    w_b = y_b.shape[1]
    assert w_a == w_b
    n_groups = w_s.shape[0]
    rows = GMLP_CHUNKS_PER_STEP * CHUNK
    tn = _pick(d, 512)
    assert tn % MXU_COLS == 0
    n_j = d // tn
    tm = min(_pick(m, 1024), rows * n_j)
    mix_steps = tm // rows
    assert m % tm == 0 and tm % rows == 0
    n_rb = m // tm
    mix_params = (w_s, b_s_t, ln_g, ln_b, g_out_a)

    def mix_specs(idx):
        const = lambda *_: (0, 0)
        return [
            pl.BlockSpec((rows, w_a), idx(0)), pl.BlockSpec((rows, w_a), idx(1)), pl.BlockSpec((rows, w_a), idx(2)),
            pl.BlockSpec((n_groups, CHUNK, CHUNK), lambda *_: (0, 0, 0)),
            pl.BlockSpec((CHUNK, n_groups), const),
            pl.BlockSpec((1, w_a), const), pl.BlockSpec((1, w_a), const), pl.BlockSpec((1, w_a), const),
        ]

    ya0 = pl.pallas_call(
        _gmlp_kernel,
        grid=(mix_steps,),
        in_specs=mix_specs(lambda seg: (lambda s: (s, seg))),
        out_specs=pl.BlockSpec((rows, w_a), lambda s: (s, 0)),
        out_shape=jax.ShapeDtypeStruct((tm, w_a), BF16),
        scratch_shapes=[pltpu.VMEM((rows, w_a), F32)],
        compiler_params=_params(("arbitrary",)),
        name="gmlp_first",
    )(proj, proj, proj, *mix_params)

    nxt = lambda seg: (lambda i, j: (jnp.minimum(i + 1, n_rb - 1) * mix_steps + jnp.minimum(j, mix_steps - 1), seg))
    return pl.pallas_call(
        functools.partial(_out_proj_kernel, mix_steps=mix_steps),
        grid=(n_rb, n_j),
        in_specs=[
            pl.BlockSpec((tm, tn), lambda i, j: (i, j)),
            pl.BlockSpec((tm, w_a), lambda i, j: (0, 0)),
            pl.BlockSpec((tm, w_b), lambda i, j: (i, 0)),
            pl.BlockSpec((w_a, tn), lambda i, j: (0, j)),
            pl.BlockSpec((w_b, tn), lambda i, j: (1, j)),
        ] + mix_specs(nxt),
        out_specs=pl.BlockSpec((tm, tn), lambda i, j: (i, j)),
        out_shape=jax.ShapeDtypeStruct((m, d), F32),
        scratch_shapes=[pltpu.VMEM((2, tm, w_a), BF16), pltpu.VMEM((rows, w_a), F32)],
        compiler_params=_params(("arbitrary", "arbitrary")),
        name="out_proj",
    )(x2, ya0, y_b, w_out_bf, w_out_bf, proj, proj, proj, *mix_params)


def _ple_kernel(h_ref, g_ref, wg_ref, p_ref, wu_ref, o_ref, hn_ref):
    j = pl.program_id(1)

    @pl.when(j == 0)
    def _():
        _normalise_rows(h_ref, g_ref, hn_ref, 16)

    tn = o_ref.shape[1]
    n_sub = tn // MXU_COLS
    p_bf = p_ref[...].astype(BF16)

    def product(c):
        return jnp.dot(hn_ref[...], wg_ref[:, c * MXU_COLS:(c + 1) * MXU_COLS], preferred_element_type=F32)

    acc = product(0)
    for c in range(n_sub):
        nxt = product(c + 1) if c + 1 < n_sub else None
        cols = slice(c * MXU_COLS, (c + 1) * MXU_COLS)
        up = jnp.dot(p_bf, wu_ref[:, cols], preferred_element_type=F32)
        h_cols = pl.ds(pl.multiple_of(j * tn + c * MXU_COLS, MXU_COLS), MXU_COLS)
        o_ref[:, cols] = h_ref[:, h_cols] + jax.nn.sigmoid(acc) * up
        acc = nxt


def _ple(h, g_ple, w_gate_bf, p2, w_up_bf):
    m, d = h.shape
    d_ple = p2.shape[1]
    tm = _pick(m, 512)
    tn = _pick(d, 1024)
    assert tn % MXU_COLS == 0
    return pl.pallas_call(
        _ple_kernel,
        grid=(m // tm, d // tn),
        in_specs=[
            pl.BlockSpec((tm, d), lambda i, j: (i, 0)),
            pl.BlockSpec((1, d), lambda i, j: (0, 0)),
            pl.BlockSpec((d, tn), lambda i, j: (0, j)),
            pl.BlockSpec((tm, d_ple), lambda i, j: (i, 0)),
            pl.BlockSpec((d_ple, tn), lambda i, j: (0, j)),
        ],
        out_specs=pl.BlockSpec((tm, tn), lambda i, j: (i, j)),
        out_shape=jax.ShapeDtypeStruct((m, d), F32),
        scratch_shapes=[pltpu.VMEM((tm, d), BF16)],
        compiler_params=_params(("parallel", "arbitrary")),
        name="ple",
    )(h, g_ple, w_gate_bf, p2, w_up_bf)


def kernel(x, p, g_pre, w_in, w_s, b_s, ln_v_g, ln_v_b, g_q, g_k, rel_bias, g_out_a, g_out_b, w_out, g_ple, w_ple_gate, w_ple_up):
    batch, seq, d = x.shape
    depth = p.shape[0]
    w_a = ln_v_g.shape[-1]
    w_b = g_out_b.shape[-1]
    d_in = w_in.shape[-1]
    assert w_a == w_b and d_in == N_SEG * w_a, "segments of the combined projection must be equally wide"
    assert seq % (MAX_DIL * BLK) == 0 and all(win // dil == BLK for win, dil in DILATED)
    assert sorted(dil for _, dil in DILATED) == [1, 4, MAX_DIL]
    m = batch * seq
    x2 = x.reshape(m, d)
    for i in range(depth):
        proj, qkv_rm, (w_out_bf, w_gate_bf) = _in_proj(
            x2, g_pre[i][None], w_in[i].astype(BF16), g_q[i][None], g_k[i][None],
            [w_out[i], w_ple_gate[i]], batch=batch, seq=seq)
        state = None
        for dil in sorted((dil for _, dil in DILATED), reverse=True):
            state = _attn_pass(dil, qkv_rm, proj, rel_bias, state, g_out_b[i][None],
                               batch=batch, seq=seq, w_b=w_b)
        y_b = state
        h = _out_proj(x2, proj, y_b, w_out_bf, w_s[i], b_s[i].T, ln_v_g[i][None], ln_v_b[i][None],
                      g_out_a[i][None])
        x2 = _ple(h, g_ple[i][None], w_gate_bf, p[i].reshape(m, -1), w_ple_up[i].astype(BF16))
    return x2.reshape(batch, seq, d)
```

```python
import functools
import math

import numpy as np
import jax
import jax.numpy as jnp
from jax import lax
from jax.experimental import pallas as pl
from jax.experimental.pallas import tpu as pltpu

HEAD_DIM = 128
CHUNK = 128
BLK = 128
QK_LOOKAHEAD = 3
ATTN_BLOCKS_PER_STEP = 4
GMLP_CHUNKS_PER_STEP = 2
DILATED = ((128, 1), (512, 4), (2048, 16))
MAX_DIL = 16
NUM_BUCKETS = 32
MAX_DISTANCE = 2048
EPS = 1e-6
NEG_INF = -1e30
N_SEG = 7
SEG_Q, SEG_K, SEG_V, SEG_BZ = 3, 4, 5, 6

V7X_VMEM_LIMIT_BYTES = 56 * 1024 * 1024
MXU_COLS = 256
NORM_ROWS = 16
SUMSQ_ROWS = 64
SLAB_PITCH = 24

BF16 = jnp.bfloat16
F32 = jnp.float32


def _pick(n, pref):
    t = min(n, pref)
    while n % t:
        t //= 2
    return t


def _params(sem):
    return pltpu.CompilerParams(dimension_semantics=sem,
                                vmem_limit_bytes=V7X_VMEM_LIMIT_BYTES)


def _gelu(v):
    return 0.5 * v * (1.0 + lax.erf(v * (1.0 / math.sqrt(2.0))))


def _silu(v):
    return v * jax.nn.sigmoid(v)


def _scale_rows(x_ref, inv, g_ref, hn_ref):
    for r0 in range(0, x_ref.shape[0], NORM_ROWS):
        r = slice(r0, r0 + NORM_ROWS)
        hn_ref[r, :] = (x_ref[r, :] * inv[r] * g_ref[...]).astype(BF16)


def _normalise_rows(x_ref, g_ref, hn_ref):
    tm, d = x_ref.shape
    sums = []
    for r0 in range(0, tm, SUMSQ_ROWS):
        acc = jnp.zeros((SUMSQ_ROWS, HEAD_DIM), F32)
        for k in range(d // HEAD_DIM):
            xk = x_ref[r0:r0 + SUMSQ_ROWS, k * HEAD_DIM:(k + 1) * HEAD_DIM]
            acc = acc + xk * xk
        sums.append(jnp.sum(acc, axis=-1, keepdims=True))
    ms = jnp.concatenate(sums, axis=0) * (1.0 / d)
    _scale_rows(x_ref, lax.rsqrt(ms + EPS), g_ref, hn_ref)


def _in_proj_kernel(x_ref, g_ref, w_ref, gq_ref, gk_ref, *rest, tiles_per_seg, n_cast):
    cast_in, (o_ref, op_ref), cast_out = rest[:n_cast], rest[n_cast:n_cast + 2], rest[n_cast + 2:2 * n_cast + 2]
    hn_ref, slab_ref = rest[2 * n_cast + 2:]
    j = pl.program_id(1)

    @pl.when(j == 0)
    def _():
        _normalise_rows(x_ref, g_ref, hn_ref)

    seg = j // tiles_per_seg
    tm, tn = o_ref.shape
    n_sub = tn // MXU_COLS

    def sub_cols(c):
        return slice(c * MXU_COLS, (c + 1) * MXU_COLS)

    def product(c):
        return jnp.dot(hn_ref[...], w_ref[:, sub_cols(c)], preferred_element_type=F32)

    def tile(epilogue, residue_major=False):
        acc = product(0)
        for src_ref, dst_ref in zip(cast_in, cast_out):
            dst_ref[...] = src_ref[...].astype(BF16)
        for c in range(n_sub):
            nxt = product(c + 1) if c + 1 < n_sub else None
            lanes = MXU_COLS // HEAD_DIM
            y = epilogue(acc)
            o_ref[:, sub_cols(c)] = y.astype(BF16)
            if residue_major:
                for k in range(lanes):
                    for a in range(tm // MAX_DIL):
                        slab_ref[c * lanes + k, a * SLAB_PITCH:a * SLAB_PITCH + MAX_DIL, :] = (
                            y[a * MAX_DIL:(a + 1) * MAX_DIL, k * HEAD_DIM:(k + 1) * HEAD_DIM])
                for r in range(MAX_DIL):
                    rows = [slab_ref[c * lanes + k, pl.ds(r, tm // MAX_DIL, stride=SLAB_PITCH), :]
                            for k in range(lanes)]
                    op_ref[0, r, :, sub_cols(c)] = jnp.concatenate(rows, axis=1).astype(BF16)
            acc = nxt

    def head_norm(g, scale):
        def epilogue(acc):
            out = []
            for c in range(MXU_COLS // HEAD_DIM):
                blk = acc[:, c * HEAD_DIM:(c + 1) * HEAD_DIM]
                ms = jnp.mean(blk * blk, axis=-1, keepdims=True)
                y = blk * lax.rsqrt(ms + EPS) * g
                out.append(y if scale is None else y * scale)
            return jnp.concatenate(out, axis=1)
        return epilogue

    @pl.when(seg <= 1)
    def _():
        tile(_gelu)

    @pl.when((seg == 2) | (seg == SEG_BZ))
    def _():
        tile(_silu)

    @pl.when(seg == SEG_Q)
    def _():
        tile(head_norm(gq_ref[...], HEAD_DIM ** -0.5), residue_major=True)

    @pl.when(seg == SEG_K)
    def _():
        tile(head_norm(gk_ref[...], None), residue_major=True)

    @pl.when(seg == SEG_V)
    def _():
        tile(lambda acc: acc, residue_major=True)


def _in_proj(x2, g_pre, w_in_bf, g_q, g_k, later_weights, *, batch, seq):
    m, d = x2.shape
    d_in = w_in_bf.shape[1]
    seg_w = d_in // N_SEG
    tm = _pick(seq, 512)
    tn = _pick(seg_w, 1024)
    assert tn % MXU_COLS == 0
    tps = seg_w // tn
    assert tm % (MAX_DIL * 16) == 0, "residue-major runs must cover whole bf16 sublane tiles"
    blocks_per_seq = seq // tm
    sub = seq // MAX_DIL
    runs = tm // MAX_DIL
    kern = functools.partial(_in_proj_kernel, tiles_per_seg=tps, n_cast=len(later_weights))
    n_j = d_in // tn
    steps = (m // tm) * n_j

    def perm_idx(i, j):
        jj = jnp.clip(j - SEG_Q * tps, 0, 3 * tps - 1)
        return (i // blocks_per_seq, 0, i % blocks_per_seq, jj)

    def cast_spec(w):
        rows = 16
        while w.shape[0] % rows or w.shape[0] // rows > steps:
            rows += 16
        last = w.shape[0] // rows - 1
        return pl.BlockSpec((rows, w.shape[1]), lambda i, j: (jnp.minimum(i * n_j + j, last), 0))

    out = pl.pallas_call(
        kern,
        grid=(m // tm, n_j),
        in_specs=[
            pl.BlockSpec((tm, d), lambda i, j: (i, 0)),
            pl.BlockSpec((1, d), lambda i, j: (0, 0)),
            pl.BlockSpec((d, tn), lambda i, j: (0, j)),
            pl.BlockSpec((1, HEAD_DIM), lambda i, j: (0, 0)),
            pl.BlockSpec((1, HEAD_DIM), lambda i, j: (0, 0)),
        ] + [cast_spec(w) for w in later_weights],
        out_specs=[
            pl.BlockSpec((tm, tn), lambda i, j: (i, j)),
            pl.BlockSpec((1, MAX_DIL, runs, tn), perm_idx),
        ] + [cast_spec(w) for w in later_weights],
        out_shape=[
            jax.ShapeDtypeStruct((m, d_in), BF16),
            jax.ShapeDtypeStruct((batch, MAX_DIL, sub, 3 * seg_w), BF16),
        ] + [jax.ShapeDtypeStruct(w.shape, BF16) for w in later_weights],
        scratch_shapes=[pltpu.VMEM((tm, d), BF16), pltpu.VMEM((tn // HEAD_DIM, runs * SLAB_PITCH, HEAD_DIM), F32)],
        compiler_params=_params(("arbitrary", "arbitrary")),
        name="in_proj",
    )(x2, g_pre, w_in_bf, g_q, g_k, *later_weights)
    return out[0], out[1], out[2:]


def _gmlp_rows(au_ref, av_ref, az_ref, ws_ref, bs_ref, lg_ref, lb_ref, go_ref, ya_ref):
    n_chunks = av_ref.shape[0] // CHUNK
    av = av_ref[...].astype(F32)
    mu = jnp.mean(av, axis=-1, keepdims=True)
    xc = av - mu
    var = jnp.mean(xc * xc, axis=-1, keepdims=True)
    avn = xc * lax.rsqrt(var + EPS) * lg_ref[...] + lb_ref[...]

    row = lax.broadcasted_iota(jnp.int32, (CHUNK, CHUNK), 0)
    col = lax.broadcasted_iota(jnp.int32, (CHUNK, CHUNK), 1)
    causal = col <= row
    n_groups = ws_ref.shape[0]
    for g in range(n_groups):
        sl = slice(g * HEAD_DIM, (g + 1) * HEAD_DIM)
        wm = jnp.where(causal, ws_ref[g], 0.0).astype(BF16)
        rhs = jnp.concatenate([avn[c * CHUNK:(c + 1) * CHUNK, sl] for c in range(n_chunks)], axis=1)
        z = jnp.dot(wm, rhs.astype(BF16), preferred_element_type=F32)
        for c in range(n_chunks):
            rows = slice(c * CHUNK, (c + 1) * CHUNK)
            zc = z[:, c * HEAD_DIM:(c + 1) * HEAD_DIM] + bs_ref[:, g:g + 1]
            ya_ref[rows, sl] = au_ref[rows, sl].astype(F32) * zc
    ya = ya_ref[...]
    ms = jnp.mean(ya * ya, axis=-1, keepdims=True)
    return (ya * lax.rsqrt(ms + EPS) * go_ref[...] * az_ref[...].astype(F32)).astype(BF16)


def _rel_bucket_np(dist):
    max_exact = NUM_BUCKETS // 2
    d = np.maximum(dist, 1).astype(np.float32)
    large = max_exact + (np.log(d / np.float32(max_exact)) / np.float32(math.log(MAX_DISTANCE / max_exact))
                         * np.float32(NUM_BUCKETS - max_exact)).astype(np.int32)
    large = np.minimum(large, NUM_BUCKETS - 1)
    return np.where(dist < max_exact, dist, large).astype(np.int32)


def _band_tables(dil, pos):
    i_q = pos[:, None]
    i_k = pos[None, :]
    in_cur = i_k <= i_q
    delta = np.where(in_cur, i_q - i_k, BLK + i_q - i_k)
    return in_cur.astype(np.int32), _rel_bucket_np(delta * dil)


class _TileIO:
    def __init__(self, prefix, run, rows_shape, g=0):
        self.prefix = prefix
        self.run = run
        self.rows_shape = rows_shape
        self.idx = prefix + (slice(g * run, (g + 1) * run),)

    def sub(self, g):
        return _TileIO(self.prefix, self.run, self.rows_shape, g)

    def load(self, ref, sl):
        return ref[self.idx + (sl,)].reshape(BLK, sl.stop - sl.start)

    def store(self, ref, sl, val):
        ref[self.idx + (sl,)] = val.reshape(self.rows_shape + (sl.stop - sl.start,))


def _attn_kernel(*refs, n_heads, n_sub, first, last, diag_bucket, io, io_state):
    it = iter(refs)
    cur_ref, bucket_ref, relb_ref = next(it), next(it), next(it)
    q_ref, kc_ref, vc_ref = next(it), next(it), next(it)
    if not first:
        o_in_ref, lse_in_ref = next(it), next(it)
    if last:
        bz_ref, go_ref = next(it), next(it)
        y_ref = next(it)
    else:
        o_out_ref, lse_out_ref = next(it), next(it)
    bias_ref, kp_ref, vp_ref = next(it), next(it), next(it)
    if last:
        yb_ref, st_ref = next(it), next(it)
    else:
        lse_ref = next(it)

    n = pl.program_id(2)
    first_step = (pl.program_id(0) == 0) & (pl.program_id(1) == 0) & (n == 0)
    prev_io = _TileIO((), BLK, (BLK,))

    @pl.when(first_step)
    def _():
        bk = bucket_ref[...]
        for h in range(n_heads):
            tab = jnp.zeros((BLK, BLK), F32)
            for b in range(NUM_BUCKETS):
                tab = jnp.where(bk == b, relb_ref[b, h], tab)
            bias_ref[h] = tab
        if not last:
            lse_ref[...] = jnp.zeros(lse_ref.shape, F32)

    def natural_rows(tile):
        st_ref[...] = tile
        per = BLK // MAX_DIL
        rows = [st_ref[pl.ds((MAX_DIL // 2) * per * (v % 2) + v // 2, 8, stride=per), :]
                for v in range(BLK // 8)]
        return jnp.concatenate(rows, axis=0)

    nt = (((1,), (1,)), ((), ()))

    def heads(g, has_prev):
        cur_io, st_io = io.sub(g), io_state.sub(g)
        kp_io, kp_src, vp_src = (io.sub(g - 1), kc_ref, vc_ref) if g else (prev_io, kp_ref, vp_ref)
        in_cur = cur_ref[...] != 0
        ones = jnp.ones((BLK, HEAD_DIM), BF16)
        if has_prev:
            eye = (lax.broadcasted_iota(jnp.int32, (BLK, BLK), 0)
                   == lax.broadcasted_iota(jnp.int32, (BLK, BLK), 1))
        if not first:
            lse_in = st_io.load(lse_in_ref, slice(0, BLK))
            if last:
                lse_in = natural_rows(lse_in)

        def logits(h):
            sl = slice(h * HEAD_DIM, (h + 1) * HEAD_DIM)
            keys = cur_io.load(kc_ref, sl)
            if has_prev:
                keys = jnp.concatenate([kp_io.load(kp_src, sl), keys], axis=0)
            return lax.dot_general(cur_io.load(q_ref, sl), keys, nt, preferred_element_type=F32)

        def softmax(h, s2):
            if has_prev:
                s_p, s_c = s2[:, :BLK], s2[:, BLK:]
                s = jnp.where(in_cur, s_c, s_p) + bias_ref[h]
                far = jnp.where(eye, s_p + relb_ref[diag_bucket, h], NEG_INF)
                mx = jnp.max(jnp.maximum(s, far), axis=-1, keepdims=True)
            else:
                s = jnp.where(in_cur, s2 + bias_ref[h], NEG_INF)
                mx = jnp.max(s, axis=-1, keepdims=True)
            lse_old = None
            if not first:
                lse_old = jnp.broadcast_to(lse_in[:, h:h + 1], (BLK, HEAD_DIM))
                mx = jnp.maximum(mx, lse_old)
            e = jnp.exp(s - mx)
            if has_prev:
                e_far = jnp.exp(far - mx)
                probs = jnp.concatenate([jnp.where(in_cur, e_far, e), jnp.where(in_cur, e, 0.0)], axis=1)
            else:
                probs = e
            return probs.astype(BF16), mx, lse_old

        def values(h, probs, mx, lse_old):
            sl = slice(h * HEAD_DIM, (h + 1) * HEAD_DIM)
            vals = jnp.concatenate([cur_io.load(vc_ref, sl), ones], axis=1)
            if has_prev:
                vals = jnp.concatenate([jnp.concatenate([kp_io.load(vp_src, sl), ones], axis=1), vals], axis=0)
            both = jnp.dot(probs, vals, preferred_element_type=F32)
            num, den = both[:, :HEAD_DIM], both[:, HEAD_DIM:]
            if first:
                total = den
                o = num / total
            else:
                o_old = st_io.load(o_in_ref, sl)
                if last:
                    o_old = natural_rows(o_old)
                w_old = jnp.exp(lse_old - mx)
                total = w_old + den
                o = (o_old * w_old + num) / total
            if last:
                yb_ref[g * BLK:(g + 1) * BLK, sl] = o
            else:
                st_io.store(o_out_ref, sl, o)
                lse = mx + jnp.log(total)
                lse_ref[g, :, h:h + 1] = lse[:, h:h + 1]

        ahead = {h: logits(h) for h in range(min(QK_LOOKAHEAD, n_heads))}
        soft = softmax(0, ahead.pop(0))
        for h in range(n_heads):
            if h + QK_LOOKAHEAD < n_heads:
                ahead[h + QK_LOOKAHEAD] = logits(h + QK_LOOKAHEAD)
            nxt = softmax(h + 1, ahead.pop(h + 1)) if h + 1 < n_heads else None
            values(h, *soft)
            soft = nxt

    @pl.when(n == 0)
    def _():
        heads(0, False)

    @pl.when(n > 0)
    def _():
        heads(0, True)

    for g in range(1, n_sub):
        heads(g, True)

    whole = slice(0, kp_ref.shape[1])
    kp_ref[...] = io.sub(n_sub - 1).load(kc_ref, whole)
    vp_ref[...] = io.sub(n_sub - 1).load(vc_ref, whole)
    if last:
        yb = yb_ref[...]
        ms = jnp.mean(yb * yb, axis=-1, keepdims=True)
        y_ref[...] = (yb * lax.rsqrt(ms + EPS) * go_ref[...] * bz_ref[...].astype(F32)).astype(BF16)
    else:
        for g in range(n_sub):
            io_state.sub(g).store(lse_out_ref, slice(0, BLK), lse_ref[g])


def _attn_pass(dil, qkv_rm, proj, rel_bias, state, g_out_b, *, batch, seq, w_b):
    first = state is None
    last = dil == 1
    n_heads = w_b // HEAD_DIM
    assert 2 * n_heads <= BLK
    sub = seq // MAX_DIL
    nb = seq // dil // BLK
    rep = MAX_DIL // dil
    runs = BLK // rep
    n_sub = min(ATTN_BLOCKS_PER_STEP, nb)
    assert nb % n_sub == 0
    col_q, col_k, col_v = 0, 1, 2

    if last:
        pos = np.arange(BLK)
        io = _TileIO((), BLK, (BLK,))
        io_state = _TileIO((0, slice(None)), BLK // MAX_DIL, (MAX_DIL, BLK // MAX_DIL))
        blk = (n_sub * BLK, w_b)
        src = proj
        cur = lambda seg: (lambda b, r, n: (b * (nb // n_sub) + n, seg))
        col_q, col_k, col_v = SEG_Q, SEG_K, SEG_V
        st_blk = lambda w: (1, MAX_DIL, n_sub * BLK // MAX_DIL, w)
        st_idx = lambda b, r, n: (b, 0, n, 0)
        view = lambda a: a
    else:
        rho = np.arange(BLK)
        pos = rep * (rho % runs) + rho // runs
        io = io_state = _TileIO((0, slice(None), 0), runs, (rep, runs))
        blk = (1, rep, 1, n_sub * runs, w_b)
        src = qkv_rm.reshape(batch, rep, dil, sub, 3 * w_b)
        cur = lambda col: (lambda b, r, n: (b, 0, r, n, col))
        st_blk = lambda w: (1, rep, 1, n_sub * runs, w)
        st_idx = lambda b, r, n: (b, 0, r, n, 0)
        view = lambda a: a.reshape(batch, rep, dil, sub, a.shape[-1])

    in_cur, bucket = _band_tables(dil, pos)
    diag_bucket = int(_rel_bucket_np(np.array([BLK * dil]))[0])
    const = lambda: pl.BlockSpec((BLK, BLK), lambda b, r, n: (0, 0))
    in_specs = [
        const(), const(), pl.BlockSpec(memory_space=pltpu.SMEM),
        pl.BlockSpec(blk, cur(col_q)), pl.BlockSpec(blk, cur(col_k)), pl.BlockSpec(blk, cur(col_v)),
    ]
    args = [jnp.asarray(in_cur), jnp.asarray(bucket), rel_bias, src, src, src]
    acc_spec = pl.BlockSpec(st_blk(w_b), st_idx)
    ml_spec = pl.BlockSpec(st_blk(BLK), st_idx)
    if not first:
        acc, ml = state
        in_specs += [acc_spec, ml_spec]
        args += [view(acc), view(ml)]
    scratch = [pltpu.VMEM((n_heads, BLK, BLK), F32),
               pltpu.VMEM((BLK, w_b), BF16), pltpu.VMEM((BLK, w_b), BF16)]
    if last:
        in_specs += [pl.BlockSpec(blk, cur(SEG_BZ)), pl.BlockSpec((1, w_b), lambda b, r, n: (0, 0))]
        args += [proj, g_out_b]
        out_specs = pl.BlockSpec(blk, cur(0))
        out_shape = jax.ShapeDtypeStruct((batch * seq, w_b), BF16)
        scratch += [pltpu.VMEM((n_sub * BLK, w_b), F32), pltpu.VMEM((BLK, HEAD_DIM), F32)]
    else:
        out_specs = [acc_spec, ml_spec]
        scratch.append(pltpu.VMEM((n_sub, BLK, BLK), F32))
        out_shape = [jax.ShapeDtypeStruct((batch, rep, dil, sub, w_b), F32),
                     jax.ShapeDtypeStruct((batch, rep, dil, sub, BLK), F32)]
    kern = functools.partial(_attn_kernel, n_heads=n_heads, n_sub=n_sub, first=first, last=last,
                             diag_bucket=diag_bucket, io=io, io_state=io_state)
    out = pl.pallas_call(
        kern,
        grid=(batch, dil, nb // n_sub),
        in_specs=in_specs,
        out_specs=out_specs,
        out_shape=out_shape,
        scratch_shapes=scratch,
        compiler_params=_params(("arbitrary", "arbitrary", "arbitrary")),
        name=f"attn_d{dil}",
    )(*args)
    if last:
        return out
    acc, ml = out
    return (acc.reshape(batch, MAX_DIL, sub, w_b), ml.reshape(batch, MAX_DIL, sub, BLK))


def _gmlp_kernel(au_ref, av_ref, az_ref, ws_ref, bs_ref, lg_ref, lb_ref, go_ref, o_ref, tmp_ref):
    o_ref[...] = _gmlp_rows(au_ref, av_ref, az_ref, ws_ref, bs_ref, lg_ref, lb_ref, go_ref, tmp_ref)


def _out_proj_kernel(x_ref, ya0_ref, yb_ref, wa_ref, wb_ref, au_ref, av_ref, az_ref, ws_ref, bs_ref, lg_ref,
                     lb_ref, go_ref, h_ref, hsq_ref, ya_ref, tmp_ref, *, mix_steps):
    i, j = pl.program_id(0), pl.program_id(1)
    rows = au_ref.shape[0]
    n_sub = h_ref.shape[1] // MXU_COLS

    @pl.when((i == 0) & (j == 0))
    def _():
        ya_ref[0] = ya0_ref[...]

    @pl.when(j == 0)
    def _():
        hsq_ref[...] = jnp.zeros(hsq_ref.shape, F32)

    def mix():
        y = _gmlp_rows(au_ref, av_ref, az_ref, ws_ref, bs_ref, lg_ref, lb_ref, go_ref, tmp_ref)
        ya_ref[(i + 1) % 2, pl.ds(pl.multiple_of(j * rows, rows), rows), :] = y

    def product(c):
        cols = slice(c * MXU_COLS, (c + 1) * MXU_COLS)
        return (jnp.dot(ya_ref[i % 2], wa_ref[:, cols], preferred_element_type=F32)
                + jnp.dot(yb_ref[...], wb_ref[:, cols], preferred_element_type=F32))

    def project(with_mix):
        acc = product(0)
        for c in range(n_sub):
            nxt = product(c + 1) if c + 1 < n_sub else None
            cols = slice(c * MXU_COLS, (c + 1) * MXU_COLS)
            hv = x_ref[:, cols] + acc
            h_ref[:, cols] = hv
            sq = hv * hv
            hsq_ref[...] += sum(sq[:, k * HEAD_DIM:(k + 1) * HEAD_DIM] for k in range(MXU_COLS // HEAD_DIM))
            acc = nxt
        if with_mix:
            mix()

    has_mix = (i + 1 < pl.num_programs(0)) & (j < mix_steps)

    @pl.when(has_mix)
    def _():
        project(True)

    @pl.when(jnp.logical_not(has_mix))
    def _():
        project(False)


def _out_proj(x2, proj, y_b, w_out_bf, w_s, b_s_t, ln_g, ln_b, g_out_a):
    m, d = x2.shape
    w_a = proj.shape[1] // N_SEG
    w_b = y_b.shape[1]
    assert w_a == w_b
    n_groups = w_s.shape[0]
    rows = GMLP_CHUNKS_PER_STEP * CHUNK
    tn = _pick(d, 512)
    assert tn % MXU_COLS == 0
    n_j = d // tn
    tm = min(_pick(m, 1024), rows * n_j)
    mix_steps = tm // rows
    assert m % tm == 0 and tm % rows == 0
    n_rb = m // tm
    mix_params = (w_s, b_s_t, ln_g, ln_b, g_out_a)

    def mix_specs(idx):
        const = lambda *_: (0, 0)
        return [
            pl.BlockSpec((rows, w_a), idx(0)), pl.BlockSpec((rows, w_a), idx(1)), pl.BlockSpec((rows, w_a), idx(2)),
            pl.BlockSpec((n_groups, CHUNK, CHUNK), lambda *_: (0, 0, 0)),
            pl.BlockSpec((CHUNK, n_groups), const),
            pl.BlockSpec((1, w_a), const), pl.BlockSpec((1, w_a), const), pl.BlockSpec((1, w_a), const),
        ]

    ya0 = pl.pallas_call(
        _gmlp_kernel,
        grid=(mix_steps,),
        in_specs=mix_specs(lambda seg: (lambda s: (s, seg))),
        out_specs=pl.BlockSpec((rows, w_a), lambda s: (s, 0)),
        out_shape=jax.ShapeDtypeStruct((tm, w_a), BF16),
        scratch_shapes=[pltpu.VMEM((rows, w_a), F32)],
        compiler_params=_params(("arbitrary",)),
        name="gmlp_first",
    )(proj, proj, proj, *mix_params)

    nxt = lambda seg: (lambda i, j: (jnp.minimum(i + 1, n_rb - 1) * mix_steps + jnp.minimum(j, mix_steps - 1), seg))
    return pl.pallas_call(
        functools.partial(_out_proj_kernel, mix_steps=mix_steps),
        grid=(n_rb, n_j),
        in_specs=[
            pl.BlockSpec((tm, tn), lambda i, j: (i, j)),
            pl.BlockSpec((tm, w_a), lambda i, j: (0, 0)),
            pl.BlockSpec((tm, w_b), lambda i, j: (i, 0)),
            pl.BlockSpec((w_a, tn), lambda i, j: (0, j)),
            pl.BlockSpec((w_b, tn), lambda i, j: (1, j)),
        ] + mix_specs(nxt),
        out_specs=[pl.BlockSpec((tm, tn), lambda i, j: (i, j)),
                   pl.BlockSpec((tm, HEAD_DIM), lambda i, j: (i, 0))],
        out_shape=[jax.ShapeDtypeStruct((m, d), F32), jax.ShapeDtypeStruct((m, HEAD_DIM), F32)],
        scratch_shapes=[pltpu.VMEM((2, tm, w_a), BF16), pltpu.VMEM((rows, w_a), F32)],
        compiler_params=_params(("arbitrary", "arbitrary")),
        name="out_proj",
    )(x2, ya0, y_b, w_out_bf, w_out_bf, proj, proj, proj, *mix_params)


def _ple_kernel(h_ref, hsq_ref, g_ref, wg_ref, p_ref, wu_ref, o_ref, hn_ref):
    j = pl.program_id(1)

    @pl.when(j == 0)
    def _():
        ms = jnp.sum(hsq_ref[...], axis=-1, keepdims=True) * (1.0 / h_ref.shape[1])
        _scale_rows(h_ref, lax.rsqrt(ms + EPS), g_ref, hn_ref)

    tn = o_ref.shape[1]
    n_sub = tn // MXU_COLS
    p_bf = p_ref[...].astype(BF16)

    def product(c):
        return jnp.dot(hn_ref[...], wg_ref[:, c * MXU_COLS:(c + 1) * MXU_COLS], preferred_element_type=F32)

    acc = product(0)
    for c in range(n_sub):
        nxt = product(c + 1) if c + 1 < n_sub else None
        cols = slice(c * MXU_COLS, (c + 1) * MXU_COLS)
        up = jnp.dot(p_bf, wu_ref[:, cols], preferred_element_type=F32)
        h_cols = pl.ds(pl.multiple_of(j * tn + c * MXU_COLS, MXU_COLS), MXU_COLS)
        o_ref[:, cols] = h_ref[:, h_cols] + jax.nn.sigmoid(acc) * up
        acc = nxt


def _ple(h, hsq, g_ple, w_gate_bf, p2, w_up_bf):
    m, d = h.shape
    d_ple = p2.shape[1]
    tm = _pick(m, 512)
    tn = _pick(d, 1024)
    assert tn % MXU_COLS == 0
    return pl.pallas_call(
        _ple_kernel,
        grid=(m // tm, d // tn),
        in_specs=[
            pl.BlockSpec((tm, d), lambda i, j: (i, 0)),
            pl.BlockSpec((tm, HEAD_DIM), lambda i, j: (i, 0)),
            pl.BlockSpec((1, d), lambda i, j: (0, 0)),
            pl.BlockSpec((d, tn), lambda i, j: (0, j)),
            pl.BlockSpec((tm, d_ple), lambda i, j: (i, 0)),
            pl.BlockSpec((d_ple, tn), lambda i, j: (0, j)),
        ],
        out_specs=pl.BlockSpec((tm, tn), lambda i, j: (i, j)),
        out_shape=jax.ShapeDtypeStruct((m, d), F32),
        scratch_shapes=[pltpu.VMEM((tm, d), BF16)],
        compiler_params=_params(("parallel", "arbitrary")),
        name="ple",
    )(h, hsq, g_ple, w_gate_bf, p2, w_up_bf)


def kernel(x, p, g_pre, w_in, w_s, b_s, ln_v_g, ln_v_b, g_q, g_k, rel_bias, g_out_a, g_out_b, w_out, g_ple, w_ple_gate, w_ple_up):
    batch, seq, d = x.shape
    depth = p.shape[0]
    w_a = ln_v_g.shape[-1]
    w_b = g_out_b.shape[-1]
    d_in = w_in.shape[-1]
    assert w_a == w_b and d_in == N_SEG * w_a, "segments of the combined projection must be equally wide"
    assert seq % (MAX_DIL * BLK) == 0 and all(win // dil == BLK for win, dil in DILATED)
    assert sorted(dil for _, dil in DILATED) == [1, 4, MAX_DIL]
    m = batch * seq
    x2 = x.reshape(m, d)
    for i in range(depth):
        proj, qkv_rm, (w_out_bf, w_gate_bf) = _in_proj(
            x2, g_pre[i][None], w_in[i].astype(BF16), g_q[i][None], g_k[i][None],
            [w_out[i], w_ple_gate[i]], batch=batch, seq=seq)
        state = None
        for dil in sorted((dil for _, dil in DILATED), reverse=True):
            state = _attn_pass(dil, qkv_rm, proj, rel_bias, state, g_out_b[i][None],
                               batch=batch, seq=seq, w_b=w_b)
        y_b = state
        h, hsq = _out_proj(x2, proj, y_b, w_out_bf, w_s[i], b_s[i].T, ln_v_g[i][None], ln_v_b[i][None],
                      g_out_a[i][None])
        x2 = _ple(h, hsq, g_ple[i][None], w_gate_bf, p[i].reshape(m, -1), w_ple_up[i].astype(BF16))
    return x2.reshape(batch, seq, d)
```

```python
import functools
import math

import numpy as np
import jax
import jax.numpy as jnp
from jax import lax
from jax.experimental import pallas as pl
from jax.experimental.pallas import tpu as pltpu

HEAD_DIM = 128
CHUNK = 128
BLK = 128
QK_LOOKAHEAD = 3
ATTN_BLOCKS_PER_STEP = 4
GMLP_CHUNKS_PER_STEP = 2
DILATED = ((128, 1), (512, 4), (2048, 16))
MAX_DIL = 16
NUM_BUCKETS = 32
MAX_DISTANCE = 2048
EPS = 1e-6
NEG_INF = -1e30
N_SEG = 7
SEG_Q, SEG_K, SEG_V, SEG_BZ = 3, 4, 5, 6

V7X_VMEM_LIMIT_BYTES = 56 * 1024 * 1024
LANES = 128
BF16_SUBLANES = 16
MXU_COLS = 256
NORM_ROWS = 16
SUMSQ_ROWS = 64
SLAB_PITCH = 24

BF16 = jnp.bfloat16
F32 = jnp.float32


def _pick(n, pref):
    t = min(n, pref)
    while n % t:
        t //= 2
    return t


def _params(sem):
    return pltpu.CompilerParams(dimension_semantics=sem,
                                vmem_limit_bytes=V7X_VMEM_LIMIT_BYTES)


def _gelu(v):
    return 0.5 * v * (1.0 + lax.erf(v * (1.0 / math.sqrt(2.0))))


def _silu(v):
    return v * jax.nn.sigmoid(v)


def _scale_rows(x_ref, inv, g_ref, hn_ref):
    for r0 in range(0, x_ref.shape[0], NORM_ROWS):
        r = slice(r0, r0 + NORM_ROWS)
        hn_ref[r, :] = (x_ref[r, :] * inv[r] * g_ref[...]).astype(BF16)


def _normalise_rows(x_ref, g_ref, hn_ref):
    tm, d = x_ref.shape
    sums = []
    for r0 in range(0, tm, SUMSQ_ROWS):
        acc = jnp.zeros((SUMSQ_ROWS, LANES), F32)
        for k in range(d // LANES):
            xk = x_ref[r0:r0 + SUMSQ_ROWS, k * LANES:(k + 1) * LANES]
            acc = acc + xk * xk
        sums.append(jnp.sum(acc, axis=-1, keepdims=True))
    ms = jnp.concatenate(sums, axis=0) * (1.0 / d)
    _scale_rows(x_ref, lax.rsqrt(ms + EPS), g_ref, hn_ref)


def _in_proj_kernel(x_ref, g_ref, w_ref, gq_ref, gk_ref, *rest, tiles_per_seg, n_cast):
    cast_in, (o_ref, op_ref), cast_out = rest[:n_cast], rest[n_cast:n_cast + 2], rest[n_cast + 2:2 * n_cast + 2]
    hn_ref, slab_ref = rest[2 * n_cast + 2:]
    j = pl.program_id(1)

    @pl.when(j == 0)
    def _():
        _normalise_rows(x_ref, g_ref, hn_ref)

    seg = j // tiles_per_seg
    tm, tn = o_ref.shape
    n_sub = tn // MXU_COLS

    def sub_cols(c):
        return slice(c * MXU_COLS, (c + 1) * MXU_COLS)

    def product(c):
        return jnp.dot(hn_ref[...], w_ref[:, sub_cols(c)], preferred_element_type=F32)

    def tile(epilogue, residue_major=False):
        acc = product(0)
        for src_ref, dst_ref in zip(cast_in, cast_out):
            dst_ref[...] = src_ref[...].astype(BF16)
        for c in range(n_sub):
            nxt = product(c + 1) if c + 1 < n_sub else None
            slabs = MXU_COLS // LANES
            y = epilogue(acc)
            o_ref[:, sub_cols(c)] = y.astype(BF16)
            if residue_major:
                for k in range(slabs):
                    for a in range(tm // MAX_DIL):
                        slab_ref[c * slabs + k, a * SLAB_PITCH:a * SLAB_PITCH + MAX_DIL, :] = (
                            y[a * MAX_DIL:(a + 1) * MAX_DIL, k * LANES:(k + 1) * LANES])
                for r in range(MAX_DIL):
                    rows = [slab_ref[c * slabs + k, pl.ds(r, tm // MAX_DIL, stride=SLAB_PITCH), :]
                            for k in range(slabs)]
                    op_ref[0, r, :, sub_cols(c)] = jnp.concatenate(rows, axis=1).astype(BF16)
            acc = nxt

    def head_norm(g, scale):
        def epilogue(acc):
            out = []
            for c in range(MXU_COLS // HEAD_DIM):
                blk = acc[:, c * HEAD_DIM:(c + 1) * HEAD_DIM]
                ms = jnp.mean(blk * blk, axis=-1, keepdims=True)
                y = blk * lax.rsqrt(ms + EPS) * g
                out.append(y if scale is None else y * scale)
            return jnp.concatenate(out, axis=1)
        return epilogue

    @pl.when(seg <= 1)
    def _():
        tile(_gelu)

    @pl.when((seg == 2) | (seg == SEG_BZ))
    def _():
        tile(_silu)

    @pl.when(seg == SEG_Q)
    def _():
        tile(head_norm(gq_ref[...], HEAD_DIM ** -0.5), residue_major=True)

    @pl.when(seg == SEG_K)
    def _():
        tile(head_norm(gk_ref[...], None), residue_major=True)

    @pl.when(seg == SEG_V)
    def _():
        tile(lambda acc: acc, residue_major=True)


def _in_proj(x2, g_pre, w_in_bf, g_q, g_k, later_weights, *, batch, seq):
    m, d = x2.shape
    d_in = w_in_bf.shape[1]
    seg_w = d_in // N_SEG
    tm = _pick(seq, 512)
    tn = _pick(seg_w, 1024)
    assert tn % MXU_COLS == 0
    tps = seg_w // tn
    assert tm % (MAX_DIL * BF16_SUBLANES) == 0, "residue-major runs must cover whole bf16 sublane tiles"
    blocks_per_seq = seq // tm
    sub = seq // MAX_DIL
    runs = tm // MAX_DIL
    kern = functools.partial(_in_proj_kernel, tiles_per_seg=tps, n_cast=len(later_weights))
    n_j = d_in // tn
    steps = (m // tm) * n_j

    def perm_idx(i, j):
        jj = jnp.clip(j - SEG_Q * tps, 0, 3 * tps - 1)
        return (i // blocks_per_seq, 0, i % blocks_per_seq, jj)

    def cast_spec(w):
        rows = BF16_SUBLANES
        while w.shape[0] % rows or w.shape[0] // rows > steps:
            rows += BF16_SUBLANES
        last = w.shape[0] // rows - 1
        return pl.BlockSpec((rows, w.shape[1]), lambda i, j: (jnp.minimum(i * n_j + j, last), 0))

    out = pl.pallas_call(
        kern,
        grid=(m // tm, n_j),
        in_specs=[
            pl.BlockSpec((tm, d), lambda i, j: (i, 0)),
            pl.BlockSpec((1, d), lambda i, j: (0, 0)),
            pl.BlockSpec((d, tn), lambda i, j: (0, j)),
            pl.BlockSpec((1, HEAD_DIM), lambda i, j: (0, 0)),
            pl.BlockSpec((1, HEAD_DIM), lambda i, j: (0, 0)),
        ] + [cast_spec(w) for w in later_weights],
        out_specs=[
            pl.BlockSpec((tm, tn), lambda i, j: (i, j)),
            pl.BlockSpec((1, MAX_DIL, runs, tn), perm_idx),
        ] + [cast_spec(w) for w in later_weights],
        out_shape=[
            jax.ShapeDtypeStruct((m, d_in), BF16),
            jax.ShapeDtypeStruct((batch, MAX_DIL, sub, 3 * seg_w), BF16),
        ] + [jax.ShapeDtypeStruct(w.shape, BF16) for w in later_weights],
        scratch_shapes=[pltpu.VMEM((tm, d), BF16), pltpu.VMEM((tn // LANES, runs * SLAB_PITCH, LANES), F32)],
        compiler_params=_params(("arbitrary", "arbitrary")),
        name="in_proj",
    )(x2, g_pre, w_in_bf, g_q, g_k, *later_weights)
    return out[0], out[1], out[2:]


def _gmlp_rows(au_ref, av_ref, az_ref, ws_ref, bs_ref, lg_ref, lb_ref, go_ref, ya_ref):
    n_chunks = av_ref.shape[0] // CHUNK
    av = av_ref[...].astype(F32)
    mu = jnp.mean(av, axis=-1, keepdims=True)
    xc = av - mu
    var = jnp.mean(xc * xc, axis=-1, keepdims=True)
    avn = xc * lax.rsqrt(var + EPS) * lg_ref[...] + lb_ref[...]

    row = lax.broadcasted_iota(jnp.int32, (CHUNK, CHUNK), 0)
    col = lax.broadcasted_iota(jnp.int32, (CHUNK, CHUNK), 1)
    causal = col <= row
    n_groups = ws_ref.shape[0]
    for g in range(n_groups):
        sl = slice(g * HEAD_DIM, (g + 1) * HEAD_DIM)
        wm = jnp.where(causal, ws_ref[g], 0.0).astype(BF16)
        rhs = jnp.concatenate([avn[c * CHUNK:(c + 1) * CHUNK, sl] for c in range(n_chunks)], axis=1)
        z = jnp.dot(wm, rhs.astype(BF16), preferred_element_type=F32)
        for c in range(n_chunks):
            rows = slice(c * CHUNK, (c + 1) * CHUNK)
            zc = z[:, c * HEAD_DIM:(c + 1) * HEAD_DIM] + bs_ref[:, g:g + 1]
            ya_ref[rows, sl] = au_ref[rows, sl].astype(F32) * zc
    ya = ya_ref[...]
    ms = jnp.mean(ya * ya, axis=-1, keepdims=True)
    return (ya * lax.rsqrt(ms + EPS) * go_ref[...] * az_ref[...].astype(F32)).astype(BF16)


def _rel_bucket_np(dist):
    max_exact = NUM_BUCKETS // 2
    d = np.maximum(dist, 1).astype(np.float32)
    large = max_exact + (np.log(d / np.float32(max_exact)) / np.float32(math.log(MAX_DISTANCE / max_exact))
                         * np.float32(NUM_BUCKETS - max_exact)).astype(np.int32)
    large = np.minimum(large, NUM_BUCKETS - 1)
    return np.where(dist < max_exact, dist, large).astype(np.int32)


def _band_tables(dil, pos):
    i_q = pos[:, None]
    i_k = pos[None, :]
    in_cur = i_k <= i_q
    delta = np.where(in_cur, i_q - i_k, BLK + i_q - i_k)
    return in_cur.astype(np.int32), _rel_bucket_np(delta * dil)


class _TileIO:
    def __init__(self, prefix, run, rows_shape, g=0):
        self.prefix = prefix
        self.run = run
        self.rows_shape = rows_shape
        self.idx = prefix + (slice(g * run, (g + 1) * run),)

    def sub(self, g):
        return _TileIO(self.prefix, self.run, self.rows_shape, g)

    def load(self, ref, sl):
        return ref[self.idx + (sl,)].reshape(BLK, sl.stop - sl.start)

    def store(self, ref, sl, val):
        ref[self.idx + (sl,)] = val.reshape(self.rows_shape + (sl.stop - sl.start,))


def _attn_kernel(*refs, n_heads, n_sub, first, last, diag_bucket, io, io_state):
    it = iter(refs)
    cur_ref, bucket_ref, relb_ref = next(it), next(it), next(it)
    q_ref, kc_ref, vc_ref = next(it), next(it), next(it)
    if not first:
        o_in_ref, lse_in_ref = next(it), next(it)
    if last:
        bz_ref, go_ref = next(it), next(it)
        y_ref = next(it)
    else:
        o_out_ref, lse_out_ref = next(it), next(it)
    bias_ref, kp_ref, vp_ref = next(it), next(it), next(it)
    if last:
        yb_ref, st_ref = next(it), next(it)
    else:
        lse_ref = next(it)

    n = pl.program_id(2)
    first_step = (pl.program_id(0) == 0) & (pl.program_id(1) == 0) & (n == 0)
    prev_io = _TileIO((), BLK, (BLK,))

    @pl.when(first_step)
    def _():
        bk = bucket_ref[...]
        for h in range(n_heads):
            tab = jnp.zeros((BLK, BLK), F32)
            for b in range(NUM_BUCKETS):
                tab = jnp.where(bk == b, relb_ref[b, h], tab)
            bias_ref[h] = tab
        if not last:
            lse_ref[...] = jnp.zeros(lse_ref.shape, F32)

    def natural_rows(tile):
        st_ref[...] = tile
        per = BLK // MAX_DIL
        rows = [st_ref[pl.ds((MAX_DIL // 2) * per * (v % 2) + v // 2, 8, stride=per), :]
                for v in range(BLK // 8)]
        return jnp.concatenate(rows, axis=0)

    nt = (((1,), (1,)), ((), ()))

    def heads(g, has_prev):
        cur_io, st_io = io.sub(g), io_state.sub(g)
        kp_io, kp_src, vp_src = (io.sub(g - 1), kc_ref, vc_ref) if g else (prev_io, kp_ref, vp_ref)
        in_cur = cur_ref[...] != 0
        ones = jnp.ones((BLK, HEAD_DIM), BF16)
        if has_prev:
            eye = (lax.broadcasted_iota(jnp.int32, (BLK, BLK), 0)
                   == lax.broadcasted_iota(jnp.int32, (BLK, BLK), 1))
        if not first:
            lse_in = st_io.load(lse_in_ref, slice(0, BLK))
            if last:
                lse_in = natural_rows(lse_in)

        def logits(h):
            sl = slice(h * HEAD_DIM, (h + 1) * HEAD_DIM)
            keys = cur_io.load(kc_ref, sl)
            if has_prev:
                keys = jnp.concatenate([kp_io.load(kp_src, sl), keys], axis=0)
            return lax.dot_general(cur_io.load(q_ref, sl), keys, nt, preferred_element_type=F32)

        def softmax(h, s2):
            if has_prev:
                s_p, s_c = s2[:, :BLK], s2[:, BLK:]
                s = jnp.where(in_cur, s_c, s_p) + bias_ref[h]
                far = jnp.where(eye, s_p + relb_ref[diag_bucket, h], NEG_INF)
                mx = jnp.max(jnp.maximum(s, far), axis=-1, keepdims=True)
            else:
                s = jnp.where(in_cur, s2 + bias_ref[h], NEG_INF)
                mx = jnp.max(s, axis=-1, keepdims=True)
            lse_old = None
            if not first:
                lse_old = jnp.broadcast_to(lse_in[:, h:h + 1], (BLK, HEAD_DIM))
                mx = jnp.maximum(mx, lse_old)
            e = jnp.exp(s - mx)
            if has_prev:
                e_far = jnp.exp(far - mx)
                probs = jnp.concatenate([jnp.where(in_cur, e_far, e), jnp.where(in_cur, e, 0.0)], axis=1)
            else:
                probs = e
            return probs.astype(BF16), mx, lse_old

        def values(h, probs, mx, lse_old):
            sl = slice(h * HEAD_DIM, (h + 1) * HEAD_DIM)
            vals = jnp.concatenate([cur_io.load(vc_ref, sl), ones], axis=1)
            if has_prev:
                vals = jnp.concatenate([jnp.concatenate([kp_io.load(vp_src, sl), ones], axis=1), vals], axis=0)
            both = jnp.dot(probs, vals, preferred_element_type=F32)
            num, den = both[:, :HEAD_DIM], both[:, HEAD_DIM:]
            if first:
                total = den
                o = num / total
            else:
                o_old = st_io.load(o_in_ref, sl)
                if last:
                    o_old = natural_rows(o_old)
                w_old = jnp.exp(lse_old - mx)
                total = w_old + den
                o = (o_old * w_old + num) / total
            if last:
                yb_ref[g * BLK:(g + 1) * BLK, sl] = o
            else:
                st_io.store(o_out_ref, sl, o)
                lse = mx + jnp.log(total)
                lse_ref[g, :, h:h + 1] = lse[:, h:h + 1]

        ahead = {h: logits(h) for h in range(min(QK_LOOKAHEAD, n_heads))}
        soft = softmax(0, ahead.pop(0))
        for h in range(n_heads):
            if h + QK_LOOKAHEAD < n_heads:
                ahead[h + QK_LOOKAHEAD] = logits(h + QK_LOOKAHEAD)
            nxt = softmax(h + 1, ahead.pop(h + 1)) if h + 1 < n_heads else None
            values(h, *soft)
            soft = nxt

    @pl.when(n == 0)
    def _():
        heads(0, False)

    @pl.when(n > 0)
    def _():
        heads(0, True)

    for g in range(1, n_sub):
        heads(g, True)

    whole = slice(0, kp_ref.shape[1])
    kp_ref[...] = io.sub(n_sub - 1).load(kc_ref, whole)
    vp_ref[...] = io.sub(n_sub - 1).load(vc_ref, whole)
    if last:
        yb = yb_ref[...]
        ms = jnp.mean(yb * yb, axis=-1, keepdims=True)
        y_ref[...] = (yb * lax.rsqrt(ms + EPS) * go_ref[...] * bz_ref[...].astype(F32)).astype(BF16)
    else:
        for g in range(n_sub):
            io_state.sub(g).store(lse_out_ref, slice(0, BLK), lse_ref[g])


def _attn_pass(dil, qkv_rm, proj, rel_bias, state, g_out_b, *, batch, seq, w_b):
    first = state is None
    last = dil == 1
    n_heads = w_b // HEAD_DIM
    assert 2 * n_heads <= BLK
    sub = seq // MAX_DIL
    nb = seq // dil // BLK
    rep = MAX_DIL // dil
    runs = BLK // rep
    n_sub = min(ATTN_BLOCKS_PER_STEP, nb)
    assert nb % n_sub == 0
    col_q, col_k, col_v = 0, 1, 2

    if last:
        pos = np.arange(BLK)
        io = _TileIO((), BLK, (BLK,))
        io_state = _TileIO((0, slice(None)), BLK // MAX_DIL, (MAX_DIL, BLK // MAX_DIL))
        blk = (n_sub * BLK, w_b)
        src = proj
        cur = lambda seg: (lambda b, r, n: (b * (nb // n_sub) + n, seg))
        col_q, col_k, col_v = SEG_Q, SEG_K, SEG_V
        st_blk = lambda w: (1, MAX_DIL, n_sub * BLK // MAX_DIL, w)
        st_idx = lambda b, r, n: (b, 0, n, 0)
        view = lambda a: a
    else:
        rho = np.arange(BLK)
        pos = rep * (rho % runs) + rho // runs
        io = io_state = _TileIO((0, slice(None), 0), runs, (rep, runs))
        blk = (1, rep, 1, n_sub * runs, w_b)
        src = qkv_rm.reshape(batch, rep, dil, sub, 3 * w_b)
        cur = lambda col: (lambda b, r, n: (b, 0, r, n, col))
        st_blk = lambda w: (1, rep, 1, n_sub * runs, w)
        st_idx = lambda b, r, n: (b, 0, r, n, 0)
        view = lambda a: a.reshape(batch, rep, dil, sub, a.shape[-1])

    in_cur, bucket = _band_tables(dil, pos)
    diag_bucket = int(_rel_bucket_np(np.array([BLK * dil]))[0])
    const = lambda: pl.BlockSpec((BLK, BLK), lambda b, r, n: (0, 0))
    in_specs = [
        const(), const(), pl.BlockSpec(memory_space=pltpu.SMEM),
        pl.BlockSpec(blk, cur(col_q)), pl.BlockSpec(blk, cur(col_k)), pl.BlockSpec(blk, cur(col_v)),
    ]
    args = [jnp.asarray(in_cur), jnp.asarray(bucket), rel_bias, src, src, src]
    acc_spec = pl.BlockSpec(st_blk(w_b), st_idx)
    ml_spec = pl.BlockSpec(st_blk(BLK), st_idx)
    if not first:
        acc, ml = state
        in_specs += [acc_spec, ml_spec]
        args += [view(acc), view(ml)]
    scratch = [pltpu.VMEM((n_heads, BLK, BLK), F32),
               pltpu.VMEM((BLK, w_b), BF16), pltpu.VMEM((BLK, w_b), BF16)]
    if last:
        in_specs += [pl.BlockSpec(blk, cur(SEG_BZ)), pl.BlockSpec((1, w_b), lambda b, r, n: (0, 0))]
        args += [proj, g_out_b]
        out_specs = pl.BlockSpec(blk, cur(0))
        out_shape = jax.ShapeDtypeStruct((batch * seq, w_b), BF16)
        scratch += [pltpu.VMEM((n_sub * BLK, w_b), F32), pltpu.VMEM((BLK, LANES), F32)]
    else:
        out_specs = [acc_spec, ml_spec]
        scratch.append(pltpu.VMEM((n_sub, BLK, BLK), F32))
        out_shape = [jax.ShapeDtypeStruct((batch, rep, dil, sub, w_b), F32),
                     jax.ShapeDtypeStruct((batch, rep, dil, sub, BLK), F32)]
    kern = functools.partial(_attn_kernel, n_heads=n_heads, n_sub=n_sub, first=first, last=last,
                             diag_bucket=diag_bucket, io=io, io_state=io_state)
    out = pl.pallas_call(
        kern,
        grid=(batch, dil, nb // n_sub),
        in_specs=in_specs,
        out_specs=out_specs,
        out_shape=out_shape,
        scratch_shapes=scratch,
        compiler_params=_params(("arbitrary", "arbitrary", "arbitrary")),
        name=f"attn_d{dil}",
    )(*args)
    if last:
        return out
    acc, ml = out
    return (acc.reshape(batch, MAX_DIL, sub, w_b), ml.reshape(batch, MAX_DIL, sub, BLK))


def _gmlp_kernel(au_ref, av_ref, az_ref, ws_ref, bs_ref, lg_ref, lb_ref, go_ref, o_ref, tmp_ref):
    o_ref[...] = _gmlp_rows(au_ref, av_ref, az_ref, ws_ref, bs_ref, lg_ref, lb_ref, go_ref, tmp_ref)


def _out_proj_kernel(x_ref, ya0_ref, yb_ref, wa_ref, wb_ref, au_ref, av_ref, az_ref, ws_ref, bs_ref, lg_ref,
                     lb_ref, go_ref, h_ref, hsq_ref, ya_ref, tmp_ref, *, mix_steps):
    i, j = pl.program_id(0), pl.program_id(1)
    rows = au_ref.shape[0]
    n_sub = h_ref.shape[1] // MXU_COLS

    @pl.when((i == 0) & (j == 0))
    def _():
        ya_ref[0] = ya0_ref[...]

    @pl.when(j == 0)
    def _():
        hsq_ref[...] = jnp.zeros(hsq_ref.shape, F32)

    def mix():
        y = _gmlp_rows(au_ref, av_ref, az_ref, ws_ref, bs_ref, lg_ref, lb_ref, go_ref, tmp_ref)
        ya_ref[(i + 1) % 2, pl.ds(pl.multiple_of(j * rows, rows), rows), :] = y

    def product(c):
        cols = slice(c * MXU_COLS, (c + 1) * MXU_COLS)
        return (jnp.dot(ya_ref[i % 2], wa_ref[:, cols], preferred_element_type=F32)
                + jnp.dot(yb_ref[...], wb_ref[:, cols], preferred_element_type=F32))

    def project(with_mix):
        acc = product(0)
        for c in range(n_sub):
            nxt = product(c + 1) if c + 1 < n_sub else None
            cols = slice(c * MXU_COLS, (c + 1) * MXU_COLS)
            hv = x_ref[:, cols] + acc
            h_ref[:, cols] = hv
            sq = hv * hv
            hsq_ref[...] += sum(sq[:, k * LANES:(k + 1) * LANES] for k in range(MXU_COLS // LANES))
            acc = nxt
        if with_mix:
            mix()

    has_mix = (i + 1 < pl.num_programs(0)) & (j < mix_steps)

    @pl.when(has_mix)
    def _():
        project(True)

    @pl.when(jnp.logical_not(has_mix))
    def _():
        project(False)


def _out_proj(x2, proj, y_b, w_out_bf, w_s, b_s_t, ln_g, ln_b, g_out_a):
    m, d = x2.shape
    w_a = proj.shape[1] // N_SEG
    w_b = y_b.shape[1]
    assert w_a == w_b
    n_groups = w_s.shape[0]
    rows = GMLP_CHUNKS_PER_STEP * CHUNK
    tn = _pick(d, 512)
    assert tn % MXU_COLS == 0
    n_j = d // tn
    tm = min(_pick(m, 1024), rows * n_j)
    mix_steps = tm // rows
    assert m % tm == 0 and tm % rows == 0
    n_rb = m // tm
    mix_params = (w_s, b_s_t, ln_g, ln_b, g_out_a)

    def mix_specs(idx):
        const = lambda *_: (0, 0)
        return [
            pl.BlockSpec((rows, w_a), idx(0)), pl.BlockSpec((rows, w_a), idx(1)), pl.BlockSpec((rows, w_a), idx(2)),
            pl.BlockSpec((n_groups, CHUNK, CHUNK), lambda *_: (0, 0, 0)),
            pl.BlockSpec((CHUNK, n_groups), const),
            pl.BlockSpec((1, w_a), const), pl.BlockSpec((1, w_a), const), pl.BlockSpec((1, w_a), const),
        ]

    ya0 = pl.pallas_call(
        _gmlp_kernel,
        grid=(mix_steps,),
        in_specs=mix_specs(lambda seg: (lambda s: (s, seg))),
        out_specs=pl.BlockSpec((rows, w_a), lambda s: (s, 0)),
        out_shape=jax.ShapeDtypeStruct((tm, w_a), BF16),
        scratch_shapes=[pltpu.VMEM((rows, w_a), F32)],
        compiler_params=_params(("arbitrary",)),
        name="gmlp_first",
    )(proj, proj, proj, *mix_params)

    nxt = lambda seg: (lambda i, j: (jnp.minimum(i + 1, n_rb - 1) * mix_steps + jnp.minimum(j, mix_steps - 1), seg))
    return pl.pallas_call(
        functools.partial(_out_proj_kernel, mix_steps=mix_steps),
        grid=(n_rb, n_j),
        in_specs=[
            pl.BlockSpec((tm, tn), lambda i, j: (i, j)),
            pl.BlockSpec((tm, w_a), lambda i, j: (0, 0)),
            pl.BlockSpec((tm, w_b), lambda i, j: (i, 0)),
            pl.BlockSpec((w_a, tn), lambda i, j: (0, j)),
            pl.BlockSpec((w_b, tn), lambda i, j: (1, j)),
        ] + mix_specs(nxt),
        out_specs=[pl.BlockSpec((tm, tn), lambda i, j: (i, j)),
                   pl.BlockSpec((tm, LANES), lambda i, j: (i, 0))],
        out_shape=[jax.ShapeDtypeStruct((m, d), F32), jax.ShapeDtypeStruct((m, LANES), F32)],
        scratch_shapes=[pltpu.VMEM((2, tm, w_a), BF16), pltpu.VMEM((rows, w_a), F32)],
        compiler_params=_params(("arbitrary", "arbitrary")),
        name="out_proj",
    )(x2, ya0, y_b, w_out_bf, w_out_bf, proj, proj, proj, *mix_params)


def _ple_kernel(h_ref, hsq_ref, g_ref, wg_ref, p_ref, wu_ref, o_ref, hn_ref):
    j = pl.program_id(1)

    @pl.when(j == 0)
    def _():
        ms = jnp.sum(hsq_ref[...], axis=-1, keepdims=True) * (1.0 / h_ref.shape[1])
        _scale_rows(h_ref, lax.rsqrt(ms + EPS), g_ref, hn_ref)

    tn = o_ref.shape[1]
    n_sub = tn // MXU_COLS
    p_bf = p_ref[...].astype(BF16)

    def product(c):
        return jnp.dot(hn_ref[...], wg_ref[:, c * MXU_COLS:(c + 1) * MXU_COLS], preferred_element_type=F32)

    acc = product(0)
    for c in range(n_sub):
        nxt = product(c + 1) if c + 1 < n_sub else None
        cols = slice(c * MXU_COLS, (c + 1) * MXU_COLS)
        up = jnp.dot(p_bf, wu_ref[:, cols], preferred_element_type=F32)
        h_cols = pl.ds(pl.multiple_of(j * tn + c * MXU_COLS, MXU_COLS), MXU_COLS)
        o_ref[:, cols] = h_ref[:, h_cols] + jax.nn.sigmoid(acc) * up
        acc = nxt


def _ple(h, hsq, g_ple, w_gate_bf, p2, w_up_bf):
    m, d = h.shape
    d_ple = p2.shape[1]
    tm = _pick(m, 512)
    tn = _pick(d, 1024)
    assert tn % MXU_COLS == 0
    return pl.pallas_call(
        _ple_kernel,
        grid=(m // tm, d // tn),
        in_specs=[
            pl.BlockSpec((tm, d), lambda i, j: (i, 0)),
            pl.BlockSpec((tm, LANES), lambda i, j: (i, 0)),
            pl.BlockSpec((1, d), lambda i, j: (0, 0)),
            pl.BlockSpec((d, tn), lambda i, j: (0, j)),
            pl.BlockSpec((tm, d_ple), lambda i, j: (i, 0)),
            pl.BlockSpec((d_ple, tn), lambda i, j: (0, j)),
        ],
        out_specs=pl.BlockSpec((tm, tn), lambda i, j: (i, j)),
        out_shape=jax.ShapeDtypeStruct((m, d), F32),
        scratch_shapes=[pltpu.VMEM((tm, d), BF16)],
        compiler_params=_params(("parallel", "arbitrary")),
        name="ple",
    )(h, hsq, g_ple, w_gate_bf, p2, w_up_bf)


def kernel(x, p, g_pre, w_in, w_s, b_s, ln_v_g, ln_v_b, g_q, g_k, rel_bias, g_out_a, g_out_b, w_out, g_ple, w_ple_gate, w_ple_up):
    batch, seq, d = x.shape
    depth = p.shape[0]
    w_a = ln_v_g.shape[-1]
    w_b = g_out_b.shape[-1]
    d_in = w_in.shape[-1]
    assert w_a == w_b and d_in == N_SEG * w_a, "segments of the combined projection must be equally wide"
    assert seq % (MAX_DIL * BLK) == 0 and all(win // dil == BLK for win, dil in DILATED)
    assert sorted(dil for _, dil in DILATED) == [1, 4, MAX_DIL]
    m = batch * seq
    x2 = x.reshape(m, d)
    for i in range(depth):
        proj, qkv_rm, (w_out_bf, w_gate_bf) = _in_proj(
            x2, g_pre[i][None], w_in[i].astype(BF16), g_q[i][None], g_k[i][None],
            [w_out[i], w_ple_gate[i]], batch=batch, seq=seq)
        state = None
        for dil in sorted((dil for _, dil in DILATED), reverse=True):
            state = _attn_pass(dil, qkv_rm, proj, rel_bias, state, g_out_b[i][None],
                               batch=batch, seq=seq, w_b=w_b)
        y_b = state
        h, hsq = _out_proj(x2, proj, y_b, w_out_bf, w_s[i], b_s[i].T, ln_v_g[i][None], ln_v_b[i][None],
                      g_out_a[i][None])
        x2 = _ple(h, hsq, g_ple[i][None], w_gate_bf, p[i].reshape(m, -1), w_ple_up[i].astype(BF16))
    return x2.reshape(batch, seq, d)
```

```python
import functools
import math

import numpy as np
import jax
import jax.numpy as jnp
from jax import lax
from jax.experimental import pallas as pl
from jax.experimental.pallas import tpu as pltpu

HEAD_DIM = 128
CHUNK = 128
BLK = 128
QK_LOOKAHEAD = 3
ATTN_BLOCKS_PER_STEP = 4
GMLP_CHUNKS_PER_STEP = 2
DILATED = ((128, 1), (512, 4), (2048, 16))
MAX_DIL = 16
NUM_BUCKETS = 32
MAX_DISTANCE = 2048
EPS = 1e-6
NEG_INF = -1e30
N_SEG = 7
SEG_Q, SEG_K, SEG_V, SEG_BZ = 3, 4, 5, 6

V7X_VMEM_LIMIT_BYTES = 56 * 1024 * 1024
LANES = 128
BF16_SUBLANES = 16
MXU_COLS = 256
NORM_ROWS = 16
SUMSQ_ROWS = 64
SLAB_PITCH = 24

BF16 = jnp.bfloat16
F32 = jnp.float32


def _pick(n, pref):
    t = min(n, pref)
    while n % t:
        t //= 2
    return t


def _params(sem):
    return pltpu.CompilerParams(dimension_semantics=sem,
                                vmem_limit_bytes=V7X_VMEM_LIMIT_BYTES)


def _gelu(v):
    return 0.5 * v * (1.0 + lax.erf(v * (1.0 / math.sqrt(2.0))))


def _sigmoid(v):
    return 0.5 * (1.0 + jnp.tanh(0.5 * v))


def _silu(v):
    return v * _sigmoid(v)


def _scale_rows(x_ref, inv, g_ref, hn_ref):
    for r0 in range(0, x_ref.shape[0], NORM_ROWS):
        r = slice(r0, r0 + NORM_ROWS)
        hn_ref[r, :] = (x_ref[r, :] * inv[r] * g_ref[...]).astype(BF16)


def _normalise_rows(x_ref, g_ref, hn_ref):
    tm, d = x_ref.shape
    sums = []
    for r0 in range(0, tm, SUMSQ_ROWS):
        acc = jnp.zeros((SUMSQ_ROWS, LANES), F32)
        for k in range(d // LANES):
            xk = x_ref[r0:r0 + SUMSQ_ROWS, k * LANES:(k + 1) * LANES]
            acc = acc + xk * xk
        sums.append(jnp.sum(acc, axis=-1, keepdims=True))
    ms = jnp.concatenate(sums, axis=0) * (1.0 / d)
    _scale_rows(x_ref, lax.rsqrt(ms + EPS), g_ref, hn_ref)


def _in_proj_kernel(x_ref, g_ref, w_ref, gq_ref, gk_ref, *rest, tiles_per_seg, n_cast):
    cast_in, (o_ref, op_ref), cast_out = rest[:n_cast], rest[n_cast:n_cast + 2], rest[n_cast + 2:2 * n_cast + 2]
    hn_ref, slab_ref = rest[2 * n_cast + 2:]
    j = pl.program_id(1)

    @pl.when(j == 0)
    def _():
        _normalise_rows(x_ref, g_ref, hn_ref)

    seg = j // tiles_per_seg
    tm, tn = o_ref.shape
    n_sub = tn // MXU_COLS

    def sub_cols(c):
        return slice(c * MXU_COLS, (c + 1) * MXU_COLS)

    def product(c):
        return jnp.dot(hn_ref[...], w_ref[:, sub_cols(c)], preferred_element_type=F32)

    def tile(epilogue, residue_major=False):
        acc = product(0)
        for src_ref, dst_ref in zip(cast_in, cast_out):
            dst_ref[...] = src_ref[...].astype(BF16)
        for c in range(n_sub):
            nxt = product(c + 1) if c + 1 < n_sub else None
            slabs = MXU_COLS // LANES
            y = epilogue(acc)
            o_ref[:, sub_cols(c)] = y.astype(BF16)
            if residue_major:
                for k in range(slabs):
                    for a in range(tm // MAX_DIL):
                        slab_ref[c * slabs + k, a * SLAB_PITCH:a * SLAB_PITCH + MAX_DIL, :] = (
                            y[a * MAX_DIL:(a + 1) * MAX_DIL, k * LANES:(k + 1) * LANES])
                for r in range(MAX_DIL):
                    rows = [slab_ref[c * slabs + k, pl.ds(r, tm // MAX_DIL, stride=SLAB_PITCH), :]
                            for k in range(slabs)]
                    op_ref[0, r, :, sub_cols(c)] = jnp.concatenate(rows, axis=1).astype(BF16)
            acc = nxt

    def head_norm(g, scale):
        def epilogue(acc):
            out = []
            for c in range(MXU_COLS // HEAD_DIM):
                blk = acc[:, c * HEAD_DIM:(c + 1) * HEAD_DIM]
                ms = jnp.mean(blk * blk, axis=-1, keepdims=True)
                y = blk * lax.rsqrt(ms + EPS) * g
                out.append(y if scale is None else y * scale)
            return jnp.concatenate(out, axis=1)
        return epilogue

    @pl.when(seg <= 1)
    def _():
        tile(_gelu)

    @pl.when((seg == 2) | (seg == SEG_BZ))
    def _():
        tile(_silu)

    @pl.when(seg == SEG_Q)
    def _():
        tile(head_norm(gq_ref[...], HEAD_DIM ** -0.5), residue_major=True)

    @pl.when(seg == SEG_K)
    def _():
        tile(head_norm(gk_ref[...], None), residue_major=True)

    @pl.when(seg == SEG_V)
    def _():
        tile(lambda acc: acc, residue_major=True)


def _in_proj(x2, g_pre, w_in_bf, g_q, g_k, later_weights, *, batch, seq):
    m, d = x2.shape
    d_in = w_in_bf.shape[1]
    seg_w = d_in // N_SEG
    tm = _pick(seq, 512)
    tn = _pick(seg_w, 1024)
    assert tn % MXU_COLS == 0
    tps = seg_w // tn
    assert tm % (MAX_DIL * BF16_SUBLANES) == 0, "residue-major runs must cover whole bf16 sublane tiles"
    blocks_per_seq = seq // tm
    sub = seq // MAX_DIL
    runs = tm // MAX_DIL
    kern = functools.partial(_in_proj_kernel, tiles_per_seg=tps, n_cast=len(later_weights))
    n_j = d_in // tn
    steps = (m // tm) * n_j

    def perm_idx(i, j):
        jj = jnp.clip(j - SEG_Q * tps, 0, 3 * tps - 1)
        return (i // blocks_per_seq, 0, i % blocks_per_seq, jj)

    def cast_spec(w):
        rows = BF16_SUBLANES
        while w.shape[0] % rows or w.shape[0] // rows > steps:
            rows += BF16_SUBLANES
        last = w.shape[0] // rows - 1
        return pl.BlockSpec((rows, w.shape[1]), lambda i, j: (jnp.minimum(i * n_j + j, last), 0))

    out = pl.pallas_call(
        kern,
        grid=(m // tm, n_j),
        in_specs=[
            pl.BlockSpec((tm, d), lambda i, j: (i, 0)),
            pl.BlockSpec((1, d), lambda i, j: (0, 0)),
            pl.BlockSpec((d, tn), lambda i, j: (0, j)),
            pl.BlockSpec((1, HEAD_DIM), lambda i, j: (0, 0)),
            pl.BlockSpec((1, HEAD_DIM), lambda i, j: (0, 0)),
        ] + [cast_spec(w) for w in later_weights],
        out_specs=[
            pl.BlockSpec((tm, tn), lambda i, j: (i, j)),
            pl.BlockSpec((1, MAX_DIL, runs, tn), perm_idx),
        ] + [cast_spec(w) for w in later_weights],
        out_shape=[
            jax.ShapeDtypeStruct((m, d_in), BF16),
            jax.ShapeDtypeStruct((batch, MAX_DIL, sub, 3 * seg_w), BF16),
        ] + [jax.ShapeDtypeStruct(w.shape, BF16) for w in later_weights],
        scratch_shapes=[pltpu.VMEM((tm, d), BF16), pltpu.VMEM((tn // LANES, runs * SLAB_PITCH, LANES), F32)],
        compiler_params=_params(("arbitrary", "arbitrary")),
        name="in_proj",
    )(x2, g_pre, w_in_bf, g_q, g_k, *later_weights)
    return out[0], out[1], out[2:]


def _gmlp_rows(au_ref, av_ref, az_ref, ws_ref, bs_ref, lg_ref, lb_ref, go_ref, ya_ref):
    n_chunks = av_ref.shape[0] // CHUNK
    av = av_ref[...].astype(F32)
    mu = jnp.mean(av, axis=-1, keepdims=True)
    xc = av - mu
    var = jnp.mean(xc * xc, axis=-1, keepdims=True)
    avn = xc * lax.rsqrt(var + EPS) * lg_ref[...] + lb_ref[...]

    row = lax.broadcasted_iota(jnp.int32, (CHUNK, CHUNK), 0)
    col = lax.broadcasted_iota(jnp.int32, (CHUNK, CHUNK), 1)
    causal = col <= row
    n_groups = ws_ref.shape[0]
    for g in range(n_groups):
        sl = slice(g * HEAD_DIM, (g + 1) * HEAD_DIM)
        wm = jnp.where(causal, ws_ref[g], 0.0).astype(BF16)
        rhs = jnp.concatenate([avn[c * CHUNK:(c + 1) * CHUNK, sl] for c in range(n_chunks)], axis=1)
        z = jnp.dot(wm, rhs.astype(BF16), preferred_element_type=F32)
        for c in range(n_chunks):
            rows = slice(c * CHUNK, (c + 1) * CHUNK)
            zc = z[:, c * HEAD_DIM:(c + 1) * HEAD_DIM] + bs_ref[:, g:g + 1]
            ya_ref[rows, sl] = au_ref[rows, sl].astype(F32) * zc
    ya = ya_ref[...]
    ms = jnp.mean(ya * ya, axis=-1, keepdims=True)
    return (ya * lax.rsqrt(ms + EPS) * go_ref[...] * az_ref[...].astype(F32)).astype(BF16)


def _rel_bucket_np(dist):
    max_exact = NUM_BUCKETS // 2
    d = np.maximum(dist, 1).astype(np.float32)
    large = max_exact + (np.log(d / np.float32(max_exact)) / np.float32(math.log(MAX_DISTANCE / max_exact))
                         * np.float32(NUM_BUCKETS - max_exact)).astype(np.int32)
    large = np.minimum(large, NUM_BUCKETS - 1)
    return np.where(dist < max_exact, dist, large).astype(np.int32)


def _band_tables(dil, pos):
    i_q = pos[:, None]
    i_k = pos[None, :]
    in_cur = i_k <= i_q
    delta = np.where(in_cur, i_q - i_k, BLK + i_q - i_k)
    return in_cur.astype(np.int32), _rel_bucket_np(delta * dil)


class _TileIO:
    def __init__(self, prefix, run, rows_shape, g=0):
        self.prefix = prefix
        self.run = run
        self.rows_shape = rows_shape
        self.idx = prefix + (slice(g * run, (g + 1) * run),)

    def sub(self, g):
        return _TileIO(self.prefix, self.run, self.rows_shape, g)

    def load(self, ref, sl):
        return ref[self.idx + (sl,)].reshape(BLK, sl.stop - sl.start)

    def store(self, ref, sl, val):
        ref[self.idx + (sl,)] = val.reshape(self.rows_shape + (sl.stop - sl.start,))


def _attn_kernel(*refs, n_heads, n_sub, first, last, diag_bucket, io, io_state):
    it = iter(refs)
    cur_ref, bucket_ref, relb_ref = next(it), next(it), next(it)
    q_ref, kc_ref, vc_ref = next(it), next(it), next(it)
    if not first:
        o_in_ref, lse_in_ref = next(it), next(it)
    if last:
        bz_ref, go_ref = next(it), next(it)
        y_ref = next(it)
    else:
        o_out_ref, lse_out_ref = next(it), next(it)
    bias_ref, kp_ref, vp_ref = next(it), next(it), next(it)
    if last:
        yb_ref, st_ref = next(it), next(it)
    else:
        lse_ref = next(it)

    n = pl.program_id(2)
    first_step = (pl.program_id(0) == 0) & (pl.program_id(1) == 0) & (n == 0)
    prev_io = _TileIO((), BLK, (BLK,))

    @pl.when(first_step)
    def _():
        bk = bucket_ref[...]
        for h in range(n_heads):
            tab = jnp.zeros((BLK, BLK), F32)
            for b in range(NUM_BUCKETS):
                tab = jnp.where(bk == b, relb_ref[b, h], tab)
            bias_ref[h] = tab
        if not last:
            lse_ref[...] = jnp.zeros(lse_ref.shape, F32)

    def natural_rows(tile):
        st_ref[...] = tile
        per = BLK // MAX_DIL
        rows = [st_ref[pl.ds((MAX_DIL // 2) * per * (v % 2) + v // 2, 8, stride=per), :]
                for v in range(BLK // 8)]
        return jnp.concatenate(rows, axis=0)

    nt = (((1,), (1,)), ((), ()))

    def heads(g, has_prev):
        cur_io, st_io = io.sub(g), io_state.sub(g)
        kp_io, kp_src, vp_src = (io.sub(g - 1), kc_ref, vc_ref) if g else (prev_io, kp_ref, vp_ref)
        in_cur = cur_ref[...] != 0
        ones = jnp.ones((BLK, HEAD_DIM), BF16)
        if has_prev:
            eye = (lax.broadcasted_iota(jnp.int32, (BLK, BLK), 0)
                   == lax.broadcasted_iota(jnp.int32, (BLK, BLK), 1))
        if not first:
            lse_in = st_io.load(lse_in_ref, slice(0, BLK))
            if last:
                lse_in = natural_rows(lse_in)

        def logits(h):
            sl = slice(h * HEAD_DIM, (h + 1) * HEAD_DIM)
            keys = cur_io.load(kc_ref, sl)
            if has_prev:
                keys = jnp.concatenate([kp_io.load(kp_src, sl), keys], axis=0)
            return lax.dot_general(cur_io.load(q_ref, sl), keys, nt, preferred_element_type=F32)

        def softmax(h, s2):
            if has_prev:
                s_p, s_c = s2[:, :BLK], s2[:, BLK:]
                s = jnp.where(in_cur, s_c, s_p) + bias_ref[h]
                far = jnp.where(eye, s_p + relb_ref[diag_bucket, h], NEG_INF)
                mx = jnp.max(jnp.maximum(s, far), axis=-1, keepdims=True)
            else:
                s = jnp.where(in_cur, s2 + bias_ref[h], NEG_INF)
                mx = jnp.max(s, axis=-1, keepdims=True)
            lse_old = None
            if not first:
                lse_old = jnp.broadcast_to(lse_in[:, h:h + 1], (BLK, HEAD_DIM))
                mx = jnp.maximum(mx, lse_old)
            e = jnp.exp(s - mx)
            if has_prev:
                e_far = jnp.exp(far - mx)
                probs = jnp.concatenate([jnp.where(in_cur, e_far, e), jnp.where(in_cur, e, 0.0)], axis=1)
            else:
                probs = e
            return probs.astype(BF16), mx, lse_old

        def values(h, probs, mx, lse_old):
            sl = slice(h * HEAD_DIM, (h + 1) * HEAD_DIM)
            vals = jnp.concatenate([cur_io.load(vc_ref, sl), ones], axis=1)
            if has_prev:
                vals = jnp.concatenate([jnp.concatenate([kp_io.load(vp_src, sl), ones], axis=1), vals], axis=0)
            both = jnp.dot(probs, vals, preferred_element_type=F32)
            num, den = both[:, :HEAD_DIM], both[:, HEAD_DIM:]
            if first:
                total = den
                o = num / total
            else:
                o_old = st_io.load(o_in_ref, sl)
                if last:
                    o_old = natural_rows(o_old)
                w_old = jnp.exp(lse_old - mx)
                total = w_old + den
                o = (o_old * w_old + num) / total
            if last:
                yb_ref[g * BLK:(g + 1) * BLK, sl] = o
            else:
                st_io.store(o_out_ref, sl, o)
                lse = mx + jnp.log(total)
                lse_ref[g, :, h:h + 1] = lse[:, h:h + 1]

        ahead = {h: logits(h) for h in range(min(QK_LOOKAHEAD, n_heads))}
        soft = softmax(0, ahead.pop(0))
        for h in range(n_heads):
            if h + QK_LOOKAHEAD < n_heads:
                ahead[h + QK_LOOKAHEAD] = logits(h + QK_LOOKAHEAD)
            nxt = softmax(h + 1, ahead.pop(h + 1)) if h + 1 < n_heads else None
            values(h, *soft)
            soft = nxt

    @pl.when(n == 0)
    def _():
        heads(0, False)

    @pl.when(n > 0)
    def _():
        heads(0, True)

    for g in range(1, n_sub):
        heads(g, True)

    whole = slice(0, kp_ref.shape[1])
    kp_ref[...] = io.sub(n_sub - 1).load(kc_ref, whole)
    vp_ref[...] = io.sub(n_sub - 1).load(vc_ref, whole)
    if last:
        yb = yb_ref[...]
        ms = jnp.mean(yb * yb, axis=-1, keepdims=True)
        y_ref[...] = (yb * lax.rsqrt(ms + EPS) * go_ref[...] * bz_ref[...].astype(F32)).astype(BF16)
    else:
        for g in range(n_sub):
            io_state.sub(g).store(lse_out_ref, slice(0, BLK), lse_ref[g])


def _attn_pass(dil, qkv_rm, proj, rel_bias, state, g_out_b, *, batch, seq, w_b):
    first = state is None
    last = dil == 1
    n_heads = w_b // HEAD_DIM
    assert 2 * n_heads <= BLK
    sub = seq // MAX_DIL
    nb = seq // dil // BLK
    rep = MAX_DIL // dil
    runs = BLK // rep
    n_sub = min(ATTN_BLOCKS_PER_STEP, nb)
    assert nb % n_sub == 0
    col_q, col_k, col_v = 0, 1, 2

    if last:
        pos = np.arange(BLK)
        io = _TileIO((), BLK, (BLK,))
        io_state = _TileIO((0, slice(None)), BLK // MAX_DIL, (MAX_DIL, BLK // MAX_DIL))
        blk = (n_sub * BLK, w_b)
        src = proj
        cur = lambda seg: (lambda b, r, n: (b * (nb // n_sub) + n, seg))
        col_q, col_k, col_v = SEG_Q, SEG_K, SEG_V
        st_blk = lambda w: (1, MAX_DIL, n_sub * BLK // MAX_DIL, w)
        st_idx = lambda b, r, n: (b, 0, n, 0)
        view = lambda a: a
    else:
        rho = np.arange(BLK)
        pos = rep * (rho % runs) + rho // runs
        io = io_state = _TileIO((0, slice(None), 0), runs, (rep, runs))
        blk = (1, rep, 1, n_sub * runs, w_b)
        src = qkv_rm.reshape(batch, rep, dil, sub, 3 * w_b)
        cur = lambda col: (lambda b, r, n: (b, 0, r, n, col))
        st_blk = lambda w: (1, rep, 1, n_sub * runs, w)
        st_idx = lambda b, r, n: (b, 0, r, n, 0)
        view = lambda a: a.reshape(batch, rep, dil, sub, a.shape[-1])

    in_cur, bucket = _band_tables(dil, pos)
    diag_bucket = int(_rel_bucket_np(np.array([BLK * dil]))[0])
    const = lambda: pl.BlockSpec((BLK, BLK), lambda b, r, n: (0, 0))
    in_specs = [
        const(), const(), pl.BlockSpec(memory_space=pltpu.SMEM),
        pl.BlockSpec(blk, cur(col_q)), pl.BlockSpec(blk, cur(col_k)), pl.BlockSpec(blk, cur(col_v)),
    ]
    args = [jnp.asarray(in_cur), jnp.asarray(bucket), rel_bias, src, src, src]
    acc_spec = pl.BlockSpec(st_blk(w_b), st_idx)
    ml_spec = pl.BlockSpec(st_blk(BLK), st_idx)
    if not first:
        acc, ml = state
        in_specs += [acc_spec, ml_spec]
        args += [view(acc), view(ml)]
    scratch = [pltpu.VMEM((n_heads, BLK, BLK), F32),
               pltpu.VMEM((BLK, w_b), BF16), pltpu.VMEM((BLK, w_b), BF16)]
    if last:
        in_specs += [pl.BlockSpec(blk, cur(SEG_BZ)), pl.BlockSpec((1, w_b), lambda b, r, n: (0, 0))]
        args += [proj, g_out_b]
        out_specs = pl.BlockSpec(blk, cur(0))
        out_shape = jax.ShapeDtypeStruct((batch * seq, w_b), BF16)
        scratch += [pltpu.VMEM((n_sub * BLK, w_b), F32), pltpu.VMEM((BLK, LANES), F32)]
    else:
        out_specs = [acc_spec, ml_spec]
        scratch.append(pltpu.VMEM((n_sub, BLK, BLK), F32))
        out_shape = [jax.ShapeDtypeStruct((batch, rep, dil, sub, w_b), F32),
                     jax.ShapeDtypeStruct((batch, rep, dil, sub, BLK), F32)]
    kern = functools.partial(_attn_kernel, n_heads=n_heads, n_sub=n_sub, first=first, last=last,
                             diag_bucket=diag_bucket, io=io, io_state=io_state)
    out = pl.pallas_call(
        kern,
        grid=(batch, dil, nb // n_sub),
        in_specs=in_specs,
        out_specs=out_specs,
        out_shape=out_shape,
        scratch_shapes=scratch,
        compiler_params=_params(("arbitrary", "arbitrary", "arbitrary")),
        name=f"attn_d{dil}",
    )(*args)
    if last:
        return out
    acc, ml = out
    return (acc.reshape(batch, MAX_DIL, sub, w_b), ml.reshape(batch, MAX_DIL, sub, BLK))


def _gmlp_kernel(au_ref, av_ref, az_ref, ws_ref, bs_ref, lg_ref, lb_ref, go_ref, o_ref, tmp_ref):
    o_ref[...] = _gmlp_rows(au_ref, av_ref, az_ref, ws_ref, bs_ref, lg_ref, lb_ref, go_ref, tmp_ref)


def _out_proj_kernel(x_ref, ya0_ref, yb_ref, wa_ref, wb_ref, au_ref, av_ref, az_ref, ws_ref, bs_ref, lg_ref,
                     lb_ref, go_ref, h_ref, hsq_ref, ya_ref, tmp_ref, *, mix_steps):
    i, j = pl.program_id(0), pl.program_id(1)
    rows = au_ref.shape[0]
    n_sub = h_ref.shape[1] // MXU_COLS

    @pl.when((i == 0) & (j == 0))
    def _():
        ya_ref[0] = ya0_ref[...]

    @pl.when(j == 0)
    def _():
        hsq_ref[...] = jnp.zeros(hsq_ref.shape, F32)

    def mix():
        y = _gmlp_rows(au_ref, av_ref, az_ref, ws_ref, bs_ref, lg_ref, lb_ref, go_ref, tmp_ref)
        ya_ref[(i + 1) % 2, pl.ds(pl.multiple_of(j * rows, rows), rows), :] = y

    def product(c):
        cols = slice(c * MXU_COLS, (c + 1) * MXU_COLS)
        return (jnp.dot(ya_ref[i % 2], wa_ref[:, cols], preferred_element_type=F32)
                + jnp.dot(yb_ref[...], wb_ref[:, cols], preferred_element_type=F32))

    def project(with_mix):
        acc = product(0)
        for c in range(n_sub):
            nxt = product(c + 1) if c + 1 < n_sub else None
            cols = slice(c * MXU_COLS, (c + 1) * MXU_COLS)
            hv = x_ref[:, cols] + acc
            h_ref[:, cols] = hv
            sq = hv * hv
            hsq_ref[...] += sum(sq[:, k * LANES:(k + 1) * LANES] for k in range(MXU_COLS // LANES))
            acc = nxt
        if with_mix:
            mix()

    has_mix = (i + 1 < pl.num_programs(0)) & (j < mix_steps)

    @pl.when(has_mix)
    def _():
        project(True)

    @pl.when(jnp.logical_not(has_mix))
    def _():
        project(False)


def _out_proj(x2, proj, y_b, w_out_bf, w_s, b_s_t, ln_g, ln_b, g_out_a):
    m, d = x2.shape
    w_a = proj.shape[1] // N_SEG
    w_b = y_b.shape[1]
    assert w_a == w_b
    n_groups = w_s.shape[0]
    rows = GMLP_CHUNKS_PER_STEP * CHUNK
    tn = _pick(d, 512)
    assert tn % MXU_COLS == 0
    n_j = d // tn
    tm = min(_pick(m, 1024), rows * n_j)
    mix_steps = tm // rows
    assert m % tm == 0 and tm % rows == 0
    n_rb = m // tm
    mix_params = (w_s, b_s_t, ln_g, ln_b, g_out_a)

    def mix_specs(idx):
        const = lambda *_: (0, 0)
        return [
            pl.BlockSpec((rows, w_a), idx(0)), pl.BlockSpec((rows, w_a), idx(1)), pl.BlockSpec((rows, w_a), idx(2)),
            pl.BlockSpec((n_groups, CHUNK, CHUNK), lambda *_: (0, 0, 0)),
            pl.BlockSpec((CHUNK, n_groups), const),
            pl.BlockSpec((1, w_a), const), pl.BlockSpec((1, w_a), const), pl.BlockSpec((1, w_a), const),
        ]

    ya0 = pl.pallas_call(
        _gmlp_kernel,
        grid=(mix_steps,),
        in_specs=mix_specs(lambda seg: (lambda s: (s, seg))),
        out_specs=pl.BlockSpec((rows, w_a), lambda s: (s, 0)),
        out_shape=jax.ShapeDtypeStruct((tm, w_a), BF16),
        scratch_shapes=[pltpu.VMEM((rows, w_a), F32)],
        compiler_params=_params(("arbitrary",)),
        name="gmlp_first",
    )(proj, proj, proj, *mix_params)

    nxt = lambda seg: (lambda i, j: (jnp.minimum(i + 1, n_rb - 1) * mix_steps + jnp.minimum(j, mix_steps - 1), seg))
    return pl.pallas_call(
        functools.partial(_out_proj_kernel, mix_steps=mix_steps),
        grid=(n_rb, n_j),
        in_specs=[
            pl.BlockSpec((tm, tn), lambda i, j: (i, j)),
            pl.BlockSpec((tm, w_a), lambda i, j: (0, 0)),
            pl.BlockSpec((tm, w_b), lambda i, j: (i, 0)),
            pl.BlockSpec((w_a, tn), lambda i, j: (0, j)),
            pl.BlockSpec((w_b, tn), lambda i, j: (1, j)),
        ] + mix_specs(nxt),
        out_specs=[pl.BlockSpec((tm, tn), lambda i, j: (i, j)),
                   pl.BlockSpec((tm, LANES), lambda i, j: (i, 0))],
        out_shape=[jax.ShapeDtypeStruct((m, d), F32), jax.ShapeDtypeStruct((m, LANES), F32)],
        scratch_shapes=[pltpu.VMEM((2, tm, w_a), BF16), pltpu.VMEM((rows, w_a), F32)],
        compiler_params=_params(("arbitrary", "arbitrary")),
        name="out_proj",
    )(x2, ya0, y_b, w_out_bf, w_out_bf, proj, proj, proj, *mix_params)


def _ple_kernel(h_ref, hsq_ref, g_ref, wg_ref, p_ref, wu_ref, o_ref, hn_ref):
    j = pl.program_id(1)

    @pl.when(j == 0)
    def _():
        ms = jnp.sum(hsq_ref[...], axis=-1, keepdims=True) * (1.0 / h_ref.shape[1])
        _scale_rows(h_ref, lax.rsqrt(ms + EPS), g_ref, hn_ref)

    tn = o_ref.shape[1]
    n_sub = tn // MXU_COLS
    p_bf = p_ref[...].astype(BF16)

    def product(c):
        return jnp.dot(hn_ref[...], wg_ref[:, c * MXU_COLS:(c + 1) * MXU_COLS], preferred_element_type=F32)

    acc = product(0)
    for c in range(n_sub):
        nxt = product(c + 1) if c + 1 < n_sub else None
        cols = slice(c * MXU_COLS, (c + 1) * MXU_COLS)
        up = jnp.dot(p_bf, wu_ref[:, cols], preferred_element_type=F32)
        h_cols = pl.ds(pl.multiple_of(j * tn + c * MXU_COLS, MXU_COLS), MXU_COLS)
        o_ref[:, cols] = h_ref[:, h_cols] + _sigmoid(acc) * up
        acc = nxt


def _ple(h, hsq, g_ple, w_gate_bf, p2, w_up_bf):
    m, d = h.shape
    d_ple = p2.shape[1]
    tm = _pick(m, 512)
    tn = _pick(d, 1024)
    assert tn % MXU_COLS == 0
    return pl.pallas_call(
        _ple_kernel,
        grid=(m // tm, d // tn),
        in_specs=[
            pl.BlockSpec((tm, d), lambda i, j: (i, 0)),
            pl.BlockSpec((tm, LANES), lambda i, j: (i, 0)),
            pl.BlockSpec((1, d), lambda i, j: (0, 0)),
            pl.BlockSpec((d, tn), lambda i, j: (0, j)),
            pl.BlockSpec((tm, d_ple), lambda i, j: (i, 0)),
            pl.BlockSpec((d_ple, tn), lambda i, j: (0, j)),
        ],
        out_specs=pl.BlockSpec((tm, tn), lambda i, j: (i, j)),
        out_shape=jax.ShapeDtypeStruct((m, d), F32),
        scratch_shapes=[pltpu.VMEM((tm, d), BF16)],
        compiler_params=_params(("parallel", "arbitrary")),
        name="ple",
    )(h, hsq, g_ple, w_gate_bf, p2, w_up_bf)


def kernel(x, p, g_pre, w_in, w_s, b_s, ln_v_g, ln_v_b, g_q, g_k, rel_bias, g_out_a, g_out_b, w_out, g_ple, w_ple_gate, w_ple_up):
    batch, seq, d = x.shape
    depth = p.shape[0]
    w_a = ln_v_g.shape[-1]
    w_b = g_out_b.shape[-1]
    d_in = w_in.shape[-1]
    assert w_a == w_b and d_in == N_SEG * w_a, "segments of the combined projection must be equally wide"
    assert seq % (MAX_DIL * BLK) == 0 and all(win // dil == BLK for win, dil in DILATED)
    assert sorted(dil for _, dil in DILATED) == [1, 4, MAX_DIL]
    m = batch * seq
    x2 = x.reshape(m, d)
    for i in range(depth):
        proj, qkv_rm, (w_out_bf, w_gate_bf) = _in_proj(
            x2, g_pre[i][None], w_in[i].astype(BF16), g_q[i][None], g_k[i][None],
            [w_out[i], w_ple_gate[i]], batch=batch, seq=seq)
        state = None
        for dil in sorted((dil for _, dil in DILATED), reverse=True):
            state = _attn_pass(dil, qkv_rm, proj, rel_bias, state, g_out_b[i][None],
                               batch=batch, seq=seq, w_b=w_b)
        y_b = state
        h, hsq = _out_proj(x2, proj, y_b, w_out_bf, w_s[i], b_s[i].T, ln_v_g[i][None], ln_v_b[i][None],
                      g_out_a[i][None])
        x2 = _ple(h, hsq, g_ple[i][None], w_gate_bf, p[i].reshape(m, -1), w_ple_up[i].astype(BF16))
    return x2.reshape(batch, seq, d)
```

```python
import functools
import math

import numpy as np
import jax
import jax.numpy as jnp
from jax import lax
from jax.experimental import pallas as pl
from jax.experimental.pallas import tpu as pltpu

HEAD_DIM = 128
CHUNK = 128
BLK = 128
QK_LOOKAHEAD = 3
ATTN_BLOCKS_PER_STEP = 4
GMLP_CHUNKS_PER_STEP = 2
DILATED = ((128, 1), (512, 4), (2048, 16))
MAX_DIL = 16
NUM_BUCKETS = 32
MAX_DISTANCE = 2048
EPS = 1e-6
NEG_INF = -1e30
N_SEG = 7
SEG_Q, SEG_K, SEG_V, SEG_BZ = 3, 4, 5, 6

V7X_VMEM_LIMIT_BYTES = 56 * 1024 * 1024
LANES = 128
BF16_SUBLANES = 16
MXU_COLS = 256
NORM_ROWS = 16
SUMSQ_ROWS = 64
SLAB_PITCH = 24

BF16 = jnp.bfloat16
F32 = jnp.float32


def _pick(n, pref):
    t = min(n, pref)
    while n % t:
        t //= 2
    return t


def _params(sem):
    return pltpu.CompilerParams(dimension_semantics=sem,
                                vmem_limit_bytes=V7X_VMEM_LIMIT_BYTES)


def _gelu(v):
    return 0.5 * v * (1.0 + lax.erf(v * (1.0 / math.sqrt(2.0))))


def _sigmoid(v):
    return 0.5 * (1.0 + jnp.tanh(0.5 * v))


def _silu(v):
    return v * _sigmoid(v)


def _scale_rows(x_ref, inv, g_ref, hn_ref):
    for r0 in range(0, x_ref.shape[0], NORM_ROWS):
        r = slice(r0, r0 + NORM_ROWS)
        hn_ref[r, :] = (x_ref[r, :] * inv[r] * g_ref[...]).astype(BF16)


def _normalise_rows(x_ref, g_ref, hn_ref):
    tm, d = x_ref.shape
    sums = []
    for r0 in range(0, tm, SUMSQ_ROWS):
        acc = jnp.zeros((SUMSQ_ROWS, LANES), F32)
        for k in range(d // LANES):
            xk = x_ref[r0:r0 + SUMSQ_ROWS, k * LANES:(k + 1) * LANES]
            acc = acc + xk * xk
        sums.append(jnp.sum(acc, axis=-1, keepdims=True))
    ms = jnp.concatenate(sums, axis=0) * (1.0 / d)
    _scale_rows(x_ref, lax.rsqrt(ms + EPS), g_ref, hn_ref)


def _in_proj_kernel(x_ref, g_ref, w_ref, gq_ref, gk_ref, *rest, tiles_per_seg, n_cast):
    cast_in, (o_ref, op_ref), cast_out = rest[:n_cast], rest[n_cast:n_cast + 2], rest[n_cast + 2:2 * n_cast + 2]
    hn_ref, slab_ref = rest[2 * n_cast + 2:]
    j = pl.program_id(1)

    @pl.when(j == 0)
    def _():
        _normalise_rows(x_ref, g_ref, hn_ref)

    seg = j // tiles_per_seg
    tm, tn = o_ref.shape
    n_sub = tn // MXU_COLS

    def sub_cols(c):
        return slice(c * MXU_COLS, (c + 1) * MXU_COLS)

    def product(c):
        return jnp.dot(hn_ref[...], w_ref[:, sub_cols(c)], preferred_element_type=F32)

    def tile(epilogue, residue_major=False):
        acc = product(0)
        for src_ref, dst_ref in zip(cast_in, cast_out):
            dst_ref[...] = src_ref[...].astype(BF16)
        for c in range(n_sub):
            nxt = product(c + 1) if c + 1 < n_sub else None
            slabs = MXU_COLS // LANES
            y = epilogue(acc)
            o_ref[:, sub_cols(c)] = y.astype(BF16)
            if residue_major:
                for k in range(slabs):
                    for a in range(tm // MAX_DIL):
                        slab_ref[c * slabs + k, a * SLAB_PITCH:a * SLAB_PITCH + MAX_DIL, :] = (
                            y[a * MAX_DIL:(a + 1) * MAX_DIL, k * LANES:(k + 1) * LANES])
                for r in range(MAX_DIL):
                    rows = [slab_ref[c * slabs + k, pl.ds(r, tm // MAX_DIL, stride=SLAB_PITCH), :]
                            for k in range(slabs)]
                    op_ref[0, r, :, sub_cols(c)] = jnp.concatenate(rows, axis=1).astype(BF16)
            acc = nxt

    def head_norm(g, scale):
        def epilogue(acc):
            out = []
            for c in range(MXU_COLS // HEAD_DIM):
                blk = acc[:, c * HEAD_DIM:(c + 1) * HEAD_DIM]
                ms = jnp.mean(blk * blk, axis=-1, keepdims=True)
                y = blk * lax.rsqrt(ms + EPS) * g
                out.append(y if scale is None else y * scale)
            return jnp.concatenate(out, axis=1)
        return epilogue

    @pl.when(seg <= 1)
    def _():
        tile(_gelu)

    @pl.when((seg == 2) | (seg == SEG_BZ))
    def _():
        tile(_silu)

    @pl.when(seg == SEG_Q)
    def _():
        tile(head_norm(gq_ref[...], HEAD_DIM ** -0.5), residue_major=True)

    @pl.when(seg == SEG_K)
    def _():
        tile(head_norm(gk_ref[...], None), residue_major=True)

    @pl.when(seg == SEG_V)
    def _():
        tile(lambda acc: acc, residue_major=True)


def _in_proj(x2, g_pre, w_in_bf, g_q, g_k, later_weights, *, batch, seq):
    m, d = x2.shape
    d_in = w_in_bf.shape[1]
    seg_w = d_in // N_SEG
    tm = _pick(seq, 512)
    tn = _pick(seg_w, 1024)
    assert tn % MXU_COLS == 0
    tps = seg_w // tn
    assert tm % (MAX_DIL * BF16_SUBLANES) == 0, "residue-major runs must cover whole bf16 sublane tiles"
    blocks_per_seq = seq // tm
    sub = seq // MAX_DIL
    runs = tm // MAX_DIL
    kern = functools.partial(_in_proj_kernel, tiles_per_seg=tps, n_cast=len(later_weights))
    n_j = d_in // tn
    steps = (m // tm) * n_j

    def perm_idx(i, j):
        jj = jnp.clip(j - SEG_Q * tps, 0, 3 * tps - 1)
        return (i // blocks_per_seq, 0, i % blocks_per_seq, jj)

    def cast_spec(w):
        rows = BF16_SUBLANES
        while w.shape[0] % rows or w.shape[0] // rows > steps:
            rows += BF16_SUBLANES
        last = w.shape[0] // rows - 1
        return pl.BlockSpec((rows, w.shape[1]), lambda i, j: (jnp.minimum(i * n_j + j, last), 0))

    out = pl.pallas_call(
        kern,
        grid=(m // tm, n_j),
        in_specs=[
            pl.BlockSpec((tm, d), lambda i, j: (i, 0)),
            pl.BlockSpec((1, d), lambda i, j: (0, 0)),
            pl.BlockSpec((d, tn), lambda i, j: (0, j)),
            pl.BlockSpec((1, HEAD_DIM), lambda i, j: (0, 0)),
            pl.BlockSpec((1, HEAD_DIM), lambda i, j: (0, 0)),
        ] + [cast_spec(w) for w in later_weights],
        out_specs=[
            pl.BlockSpec((tm, tn), lambda i, j: (i, j)),
            pl.BlockSpec((1, MAX_DIL, runs, tn), perm_idx),
        ] + [cast_spec(w) for w in later_weights],
        out_shape=[
            jax.ShapeDtypeStruct((m, d_in), BF16),
            jax.ShapeDtypeStruct((batch, MAX_DIL, sub, 3 * seg_w), BF16),
        ] + [jax.ShapeDtypeStruct(w.shape, BF16) for w in later_weights],
        scratch_shapes=[pltpu.VMEM((tm, d), BF16), pltpu.VMEM((tn // LANES, runs * SLAB_PITCH, LANES), F32)],
        compiler_params=_params(("arbitrary", "arbitrary")),
        name="in_proj",
    )(x2, g_pre, w_in_bf, g_q, g_k, *later_weights)
    return out[0], out[1], out[2:]


def _gmlp_rows(au_ref, av_ref, az_ref, ws_ref, bs_ref, lg_ref, lb_ref, go_ref, ya_ref):
    n_chunks = av_ref.shape[0] // CHUNK
    av = av_ref[...].astype(F32)
    mu = jnp.mean(av, axis=-1, keepdims=True)
    xc = av - mu
    var = jnp.mean(xc * xc, axis=-1, keepdims=True)
    avn = xc * lax.rsqrt(var + EPS) * lg_ref[...] + lb_ref[...]

    row = lax.broadcasted_iota(jnp.int32, (CHUNK, CHUNK), 0)
    col = lax.broadcasted_iota(jnp.int32, (CHUNK, CHUNK), 1)
    causal = col <= row
    n_groups = ws_ref.shape[0]
    for g in range(n_groups):
        sl = slice(g * HEAD_DIM, (g + 1) * HEAD_DIM)
        wm = jnp.where(causal, ws_ref[g], 0.0).astype(BF16)
        rhs = jnp.concatenate([avn[c * CHUNK:(c + 1) * CHUNK, sl] for c in range(n_chunks)], axis=1)
        z = jnp.dot(wm, rhs.astype(BF16), preferred_element_type=F32)
        for c in range(n_chunks):
            rows = slice(c * CHUNK, (c + 1) * CHUNK)
            zc = z[:, c * HEAD_DIM:(c + 1) * HEAD_DIM] + bs_ref[:, g:g + 1]
            ya_ref[rows, sl] = au_ref[rows, sl].astype(F32) * zc
    ya = ya_ref[...]
    ms = jnp.mean(ya * ya, axis=-1, keepdims=True)
    return (ya * lax.rsqrt(ms + EPS) * go_ref[...] * az_ref[...].astype(F32)).astype(BF16)


def _rel_bucket_np(dist):
    max_exact = NUM_BUCKETS // 2
    d = np.maximum(dist, 1).astype(np.float32)
    large = max_exact + (np.log(d / np.float32(max_exact)) / np.float32(math.log(MAX_DISTANCE / max_exact))
                         * np.float32(NUM_BUCKETS - max_exact)).astype(np.int32)
    large = np.minimum(large, NUM_BUCKETS - 1)
    return np.where(dist < max_exact, dist, large).astype(np.int32)


def _band_tables(dil, pos):
    i_q = pos[:, None]
    i_k = pos[None, :]
    in_cur = i_k <= i_q
    delta = np.where(in_cur, i_q - i_k, BLK + i_q - i_k)
    return in_cur.astype(np.int32), _rel_bucket_np(delta * dil)


class _TileIO:
    def __init__(self, prefix, run, rows_shape, g=0):
        self.prefix = prefix
        self.run = run
        self.rows_shape = rows_shape
        self.idx = prefix + (slice(g * run, (g + 1) * run),)

    def sub(self, g):
        return _TileIO(self.prefix, self.run, self.rows_shape, g)

    def load(self, ref, sl):
        return ref[self.idx + (sl,)].reshape(BLK, sl.stop - sl.start)

    def store(self, ref, sl, val):
        ref[self.idx + (sl,)] = val.reshape(self.rows_shape + (sl.stop - sl.start,))


def _attn_kernel(*refs, n_heads, n_sub, first, last, diag_bucket, io, io_state):
    it = iter(refs)
    cur_ref, bucket_ref, relb_ref = next(it), next(it), next(it)
    q_ref, kc_ref, vc_ref = next(it), next(it), next(it)
    if not first:
        o_in_ref, lse_in_ref = next(it), next(it)
    if last:
        bz_ref, go_ref = next(it), next(it)
        y_ref = next(it)
    else:
        o_out_ref, lse_out_ref = next(it), next(it)
    bias_ref, kp_ref, vp_ref = next(it), next(it), next(it)
    if last:
        yb_ref, st_ref = next(it), next(it)
    else:
        lse_ref = next(it)

    n = pl.program_id(2)
    first_step = (pl.program_id(0) == 0) & (pl.program_id(1) == 0) & (n == 0)
    prev_io = _TileIO((), BLK, (BLK,))

    @pl.when(first_step)
    def _():
        bk = bucket_ref[...]
        for h in range(n_heads):
            tab = jnp.zeros((BLK, BLK), F32)
            for b in range(NUM_BUCKETS):
                tab = jnp.where(bk == b, relb_ref[b, h], tab)
            bias_ref[h] = tab
        if not last:
            lse_ref[...] = jnp.zeros(lse_ref.shape, F32)

    def natural_rows(tile):
        st_ref[...] = tile
        per = BLK // MAX_DIL
        rows = [st_ref[pl.ds((MAX_DIL // 2) * per * (v % 2) + v // 2, 8, stride=per), :]
                for v in range(BLK // 8)]
        return jnp.concatenate(rows, axis=0)

    nt = (((1,), (1,)), ((), ()))

    def heads(g, has_prev):
        cur_io, st_io = io.sub(g), io_state.sub(g)
        kp_io, kp_src, vp_src = (io.sub(g - 1), kc_ref, vc_ref) if g else (prev_io, kp_ref, vp_ref)
        in_cur = cur_ref[...] != 0
        ones = jnp.ones((BLK, HEAD_DIM), BF16)
        if has_prev:
            eye = (lax.broadcasted_iota(jnp.int32, (BLK, BLK), 0)
                   == lax.broadcasted_iota(jnp.int32, (BLK, BLK), 1))
        if not first:
            lse_in = st_io.load(lse_in_ref, slice(0, BLK))
            if last:
                lse_in = natural_rows(lse_in)

        def logits(h):
            sl = slice(h * HEAD_DIM, (h + 1) * HEAD_DIM)
            keys = cur_io.load(kc_ref, sl)
            if has_prev:
                keys = jnp.concatenate([kp_io.load(kp_src, sl), keys], axis=0)
            return lax.dot_general(cur_io.load(q_ref, sl), keys, nt, preferred_element_type=F32)

        def softmax(h, s2):
            if has_prev:
                s_p, s_c = s2[:, :BLK], s2[:, BLK:]
                s = jnp.where(in_cur, s_c, s_p) + bias_ref[h]
                far = jnp.where(eye, s_p + relb_ref[diag_bucket, h], NEG_INF)
                mx = jnp.max(jnp.maximum(s, far), axis=-1, keepdims=True)
            else:
                s = jnp.where(in_cur, s2 + bias_ref[h], NEG_INF)
                mx = jnp.max(s, axis=-1, keepdims=True)
            lse_old = None
            if not first:
                lse_old = jnp.broadcast_to(lse_in[:, h:h + 1], (BLK, HEAD_DIM))
                mx = jnp.maximum(mx, lse_old)
            e = jnp.exp(s - mx)
            if has_prev:
                e_far = jnp.exp(far - mx)
                probs = jnp.concatenate([jnp.where(in_cur, e_far, e), jnp.where(in_cur, e, 0.0)], axis=1)
            else:
                probs = e
            return probs.astype(BF16), mx, lse_old

        def values(h, probs, mx, lse_old):
            sl = slice(h * HEAD_DIM, (h + 1) * HEAD_DIM)
            vals = jnp.concatenate([cur_io.load(vc_ref, sl), ones], axis=1)
            if has_prev:
                vals = jnp.concatenate([jnp.concatenate([kp_io.load(vp_src, sl), ones], axis=1), vals], axis=0)
            both = jnp.dot(probs, vals, preferred_element_type=F32)
            num, den = both[:, :HEAD_DIM], both[:, HEAD_DIM:]
            if first:
                total = den
                o = num / total
            else:
                o_old = st_io.load(o_in_ref, sl).astype(F32)
                if last:
                    o_old = natural_rows(o_old)
                w_old = jnp.exp(lse_old - mx)
                total = w_old + den
                o = (o_old * w_old + num) / total
            if last:
                yb_ref[g * BLK:(g + 1) * BLK, sl] = o
            else:
                st_io.store(o_out_ref, sl, o.astype(o_out_ref.dtype))
                lse = mx + jnp.log(total)
                lse_ref[g, :, h:h + 1] = lse[:, h:h + 1]

        ahead = {h: logits(h) for h in range(min(QK_LOOKAHEAD, n_heads))}
        soft = softmax(0, ahead.pop(0))
        for h in range(n_heads):
            if h + QK_LOOKAHEAD < n_heads:
                ahead[h + QK_LOOKAHEAD] = logits(h + QK_LOOKAHEAD)
            nxt = softmax(h + 1, ahead.pop(h + 1)) if h + 1 < n_heads else None
            values(h, *soft)
            soft = nxt

    @pl.when(n == 0)
    def _():
        heads(0, False)

    @pl.when(n > 0)
    def _():
        heads(0, True)

    for g in range(1, n_sub):
        heads(g, True)

    whole = slice(0, kp_ref.shape[1])
    kp_ref[...] = io.sub(n_sub - 1).load(kc_ref, whole)
    vp_ref[...] = io.sub(n_sub - 1).load(vc_ref, whole)
    if last:
        yb = yb_ref[...]
        ms = jnp.mean(yb * yb, axis=-1, keepdims=True)
        y_ref[...] = (yb * lax.rsqrt(ms + EPS) * go_ref[...] * bz_ref[...].astype(F32)).astype(BF16)
    else:
        for g in range(n_sub):
            io_state.sub(g).store(lse_out_ref, slice(0, BLK), lse_ref[g])


def _attn_pass(dil, qkv_rm, proj, rel_bias, state, g_out_b, *, batch, seq, w_b):
    first = state is None
    last = dil == 1
    n_heads = w_b // HEAD_DIM
    assert 2 * n_heads <= BLK
    sub = seq // MAX_DIL
    nb = seq // dil // BLK
    rep = MAX_DIL // dil
    runs = BLK // rep
    n_sub = min(ATTN_BLOCKS_PER_STEP, nb)
    assert nb % n_sub == 0
    col_q, col_k, col_v = 0, 1, 2

    if last:
        pos = np.arange(BLK)
        io = _TileIO((), BLK, (BLK,))
        io_state = _TileIO((0, slice(None)), BLK // MAX_DIL, (MAX_DIL, BLK // MAX_DIL))
        blk = (n_sub * BLK, w_b)
        src = proj
        cur = lambda seg: (lambda b, r, n: (b * (nb // n_sub) + n, seg))
        col_q, col_k, col_v = SEG_Q, SEG_K, SEG_V
        st_blk = lambda w: (1, MAX_DIL, n_sub * BLK // MAX_DIL, w)
        st_idx = lambda b, r, n: (b, 0, n, 0)
        view = lambda a: a
    else:
        rho = np.arange(BLK)
        pos = rep * (rho % runs) + rho // runs
        io = io_state = _TileIO((0, slice(None), 0), runs, (rep, runs))
        blk = (1, rep, 1, n_sub * runs, w_b)
        src = qkv_rm.reshape(batch, rep, dil, sub, 3 * w_b)
        cur = lambda col: (lambda b, r, n: (b, 0, r, n, col))
        st_blk = lambda w: (1, rep, 1, n_sub * runs, w)
        st_idx = lambda b, r, n: (b, 0, r, n, 0)
        view = lambda a: a.reshape(batch, rep, dil, sub, a.shape[-1])

    in_cur, bucket = _band_tables(dil, pos)
    diag_bucket = int(_rel_bucket_np(np.array([BLK * dil]))[0])
    const = lambda: pl.BlockSpec((BLK, BLK), lambda b, r, n: (0, 0))
    in_specs = [
        const(), const(), pl.BlockSpec(memory_space=pltpu.SMEM),
        pl.BlockSpec(blk, cur(col_q)), pl.BlockSpec(blk, cur(col_k)), pl.BlockSpec(blk, cur(col_v)),
    ]
    args = [jnp.asarray(in_cur), jnp.asarray(bucket), rel_bias, src, src, src]
    acc_spec = pl.BlockSpec(st_blk(w_b), st_idx)
    ml_spec = pl.BlockSpec(st_blk(BLK), st_idx)
    if not first:
        acc, ml = state
        in_specs += [acc_spec, ml_spec]
        args += [view(acc), view(ml)]
    scratch = [pltpu.VMEM((n_heads, BLK, BLK), F32),
               pltpu.VMEM((BLK, w_b), BF16), pltpu.VMEM((BLK, w_b), BF16)]
    if last:
        in_specs += [pl.BlockSpec(blk, cur(SEG_BZ)), pl.BlockSpec((1, w_b), lambda b, r, n: (0, 0))]
        args += [proj, g_out_b]
        out_specs = pl.BlockSpec(blk, cur(0))
        out_shape = jax.ShapeDtypeStruct((batch * seq, w_b), BF16)
        scratch += [pltpu.VMEM((n_sub * BLK, w_b), F32), pltpu.VMEM((BLK, LANES), F32)]
    else:
        out_specs = [acc_spec, ml_spec]
        scratch.append(pltpu.VMEM((n_sub, BLK, BLK), F32))
        o_dtype = BF16 if first else F32
        out_shape = [jax.ShapeDtypeStruct((batch, rep, dil, sub, w_b), o_dtype),
                     jax.ShapeDtypeStruct((batch, rep, dil, sub, BLK), F32)]
    kern = functools.partial(_attn_kernel, n_heads=n_heads, n_sub=n_sub, first=first, last=last,
                             diag_bucket=diag_bucket, io=io, io_state=io_state)
    out = pl.pallas_call(
        kern,
        grid=(batch, dil, nb // n_sub),
        in_specs=in_specs,
        out_specs=out_specs,
        out_shape=out_shape,
        scratch_shapes=scratch,
        compiler_params=_params(("arbitrary", "arbitrary", "arbitrary")),
        name=f"attn_d{dil}",
    )(*args)
    if last:
        return out
    acc, ml = out
    return (acc.reshape(batch, MAX_DIL, sub, w_b), ml.reshape(batch, MAX_DIL, sub, BLK))


def _gmlp_kernel(au_ref, av_ref, az_ref, ws_ref, bs_ref, lg_ref, lb_ref, go_ref, o_ref, tmp_ref):
    o_ref[...] = _gmlp_rows(au_ref, av_ref, az_ref, ws_ref, bs_ref, lg_ref, lb_ref, go_ref, tmp_ref)


def _out_proj_kernel(x_ref, ya0_ref, yb_ref, wa_ref, wb_ref, au_ref, av_ref, az_ref, ws_ref, bs_ref, lg_ref,
                     lb_ref, go_ref, h_ref, hsq_ref, ya_ref, tmp_ref, *, mix_steps):
    i, j = pl.program_id(0), pl.program_id(1)
    rows = au_ref.shape[0]
    n_sub = h_ref.shape[1] // MXU_COLS

    @pl.when((i == 0) & (j == 0))
    def _():
        ya_ref[0] = ya0_ref[...]

    @pl.when(j == 0)
    def _():
        hsq_ref[...] = jnp.zeros(hsq_ref.shape, F32)

    def mix():
        y = _gmlp_rows(au_ref, av_ref, az_ref, ws_ref, bs_ref, lg_ref, lb_ref, go_ref, tmp_ref)
        ya_ref[(i + 1) % 2, pl.ds(pl.multiple_of(j * rows, rows), rows), :] = y

    def product(c):
        cols = slice(c * MXU_COLS, (c + 1) * MXU_COLS)
        return (jnp.dot(ya_ref[i % 2], wa_ref[:, cols], preferred_element_type=F32)
                + jnp.dot(yb_ref[...], wb_ref[:, cols], preferred_element_type=F32))

    def project(with_mix):
        acc = product(0)
        for c in range(n_sub):
            nxt = product(c + 1) if c + 1 < n_sub else None
            cols = slice(c * MXU_COLS, (c + 1) * MXU_COLS)
            hv = x_ref[:, cols] + acc
            h_ref[:, cols] = hv
            sq = hv * hv
            hsq_ref[...] += sum(sq[:, k * LANES:(k + 1) * LANES] for k in range(MXU_COLS // LANES))
            acc = nxt
        if with_mix:
            mix()

    has_mix = (i + 1 < pl.num_programs(0)) & (j < mix_steps)

    @pl.when(has_mix)
    def _():
        project(True)

    @pl.when(jnp.logical_not(has_mix))
    def _():
        project(False)


def _out_proj(x2, proj, y_b, w_out_bf, w_s, b_s_t, ln_g, ln_b, g_out_a):
    m, d = x2.shape
    w_a = proj.shape[1] // N_SEG
    w_b = y_b.shape[1]
    assert w_a == w_b
    n_groups = w_s.shape[0]
    rows = GMLP_CHUNKS_PER_STEP * CHUNK
    tn = _pick(d, 512)
    assert tn % MXU_COLS == 0
    n_j = d // tn
    tm = min(_pick(m, 1024), rows * n_j)
    mix_steps = tm // rows
    assert m % tm == 0 and tm % rows == 0
    n_rb = m // tm
    mix_params = (w_s, b_s_t, ln_g, ln_b, g_out_a)

    def mix_specs(idx):
        const = lambda *_: (0, 0)
        return [
            pl.BlockSpec((rows, w_a), idx(0)), pl.BlockSpec((rows, w_a), idx(1)), pl.BlockSpec((rows, w_a), idx(2)),
            pl.BlockSpec((n_groups, CHUNK, CHUNK), lambda *_: (0, 0, 0)),
            pl.BlockSpec((CHUNK, n_groups), const),
            pl.BlockSpec((1, w_a), const), pl.BlockSpec((1, w_a), const), pl.BlockSpec((1, w_a), const),
        ]

    ya0 = pl.pallas_call(
        _gmlp_kernel,
        grid=(mix_steps,),
        in_specs=mix_specs(lambda seg: (lambda s: (s, seg))),
        out_specs=pl.BlockSpec((rows, w_a), lambda s: (s, 0)),
        out_shape=jax.ShapeDtypeStruct((tm, w_a), BF16),
        scratch_shapes=[pltpu.VMEM((rows, w_a), F32)],
        compiler_params=_params(("arbitrary",)),
        name="gmlp_first",
    )(proj, proj, proj, *mix_params)

    nxt = lambda seg: (lambda i, j: (jnp.minimum(i + 1, n_rb - 1) * mix_steps + jnp.minimum(j, mix_steps - 1), seg))
    return pl.pallas_call(
        functools.partial(_out_proj_kernel, mix_steps=mix_steps),
        grid=(n_rb, n_j),
        in_specs=[
            pl.BlockSpec((tm, tn), lambda i, j: (i, j)),
            pl.BlockSpec((tm, w_a), lambda i, j: (0, 0)),
            pl.BlockSpec((tm, w_b), lambda i, j: (i, 0)),
            pl.BlockSpec((w_a, tn), lambda i, j: (0, j)),
            pl.BlockSpec((w_b, tn), lambda i, j: (1, j)),
        ] + mix_specs(nxt),
        out_specs=[pl.BlockSpec((tm, tn), lambda i, j: (i, j)),
                   pl.BlockSpec((tm, LANES), lambda i, j: (i, 0))],
        out_shape=[jax.ShapeDtypeStruct((m, d), F32), jax.ShapeDtypeStruct((m, LANES), F32)],
        scratch_shapes=[pltpu.VMEM((2, tm, w_a), BF16), pltpu.VMEM((rows, w_a), F32)],
        compiler_params=_params(("arbitrary", "arbitrary")),
        name="out_proj",
    )(x2, ya0, y_b, w_out_bf, w_out_bf, proj, proj, proj, *mix_params)


def _ple_kernel(h_ref, hsq_ref, g_ref, wg_ref, p_ref, wu_ref, o_ref, hn_ref):
    j = pl.program_id(1)

    @pl.when(j == 0)
    def _():
        ms = jnp.sum(hsq_ref[...], axis=-1, keepdims=True) * (1.0 / h_ref.shape[1])
        _scale_rows(h_ref, lax.rsqrt(ms + EPS), g_ref, hn_ref)

    tn = o_ref.shape[1]
    n_sub = tn // MXU_COLS
    p_bf = p_ref[...].astype(BF16)

    def product(c):
        return jnp.dot(hn_ref[...], wg_ref[:, c * MXU_COLS:(c + 1) * MXU_COLS], preferred_element_type=F32)

    acc = product(0)
    for c in range(n_sub):
        nxt = product(c + 1) if c + 1 < n_sub else None
        cols = slice(c * MXU_COLS, (c + 1) * MXU_COLS)
        up = jnp.dot(p_bf, wu_ref[:, cols], preferred_element_type=F32)
        h_cols = pl.ds(pl.multiple_of(j * tn + c * MXU_COLS, MXU_COLS), MXU_COLS)
        o_ref[:, cols] = h_ref[:, h_cols] + _sigmoid(acc) * up
        acc = nxt


def _ple(h, hsq, g_ple, w_gate_bf, p2, w_up_bf):
    m, d = h.shape
    d_ple = p2.shape[1]
    tm = _pick(m, 512)
    tn = _pick(d, 1024)
    assert tn % MXU_COLS == 0
    return pl.pallas_call(
        _ple_kernel,
        grid=(m // tm, d // tn),
        in_specs=[
            pl.BlockSpec((tm, d), lambda i, j: (i, 0)),
            pl.BlockSpec((tm, LANES), lambda i, j: (i, 0)),
            pl.BlockSpec((1, d), lambda i, j: (0, 0)),
            pl.BlockSpec((d, tn), lambda i, j: (0, j)),
            pl.BlockSpec((tm, d_ple), lambda i, j: (i, 0)),
            pl.BlockSpec((d_ple, tn), lambda i, j: (0, j)),
        ],
        out_specs=pl.BlockSpec((tm, tn), lambda i, j: (i, j)),
        out_shape=jax.ShapeDtypeStruct((m, d), F32),
        scratch_shapes=[pltpu.VMEM((tm, d), BF16)],
        compiler_params=_params(("parallel", "arbitrary")),
        name="ple",
    )(h, hsq, g_ple, w_gate_bf, p2, w_up_bf)


def kernel(x, p, g_pre, w_in, w_s, b_s, ln_v_g, ln_v_b, g_q, g_k, rel_bias, g_out_a, g_out_b, w_out, g_ple, w_ple_gate, w_ple_up):
    batch, seq, d = x.shape
    depth = p.shape[0]
    w_a = ln_v_g.shape[-1]
    w_b = g_out_b.shape[-1]
    d_in = w_in.shape[-1]
    assert w_a == w_b and d_in == N_SEG * w_a, "segments of the combined projection must be equally wide"
    assert seq % (MAX_DIL * BLK) == 0 and all(win // dil == BLK for win, dil in DILATED)
    assert sorted(dil for _, dil in DILATED) == [1, 4, MAX_DIL]
    m = batch * seq
    x2 = x.reshape(m, d)
    for i in range(depth):
        proj, qkv_rm, (w_out_bf, w_gate_bf) = _in_proj(
            x2, g_pre[i][None], w_in[i].astype(BF16), g_q[i][None], g_k[i][None],
            [w_out[i], w_ple_gate[i]], batch=batch, seq=seq)
        state = None
        for dil in sorted((dil for _, dil in DILATED), reverse=True):
            state = _attn_pass(dil, qkv_rm, proj, rel_bias, state, g_out_b[i][None],
                               batch=batch, seq=seq, w_b=w_b)
        y_b = state
        h, hsq = _out_proj(x2, proj, y_b, w_out_bf, w_s[i], b_s[i].T, ln_v_g[i][None], ln_v_b[i][None],
                      g_out_a[i][None])
        x2 = _ple(h, hsq, g_ple[i][None], w_gate_bf, p[i].reshape(m, -1), w_ple_up[i].astype(BF16))
    return x2.reshape(batch, seq, d)
```

```python
import functools
import math

import numpy as np
import jax
import jax.numpy as jnp
from jax import lax
from jax.experimental import pallas as pl
from jax.experimental.pallas import tpu as pltpu

HEAD_DIM = 128
CHUNK = 128
BLK = 128
QK_LOOKAHEAD = 3
ATTN_BLOCKS_PER_STEP = 4
GMLP_CHUNKS_PER_STEP = 2
DILATED = ((128, 1), (512, 4), (2048, 16))
MAX_DIL = 16
NUM_BUCKETS = 32
MAX_DISTANCE = 2048
EPS = 1e-6
NEG_INF = -1e30
N_SEG = 7
SEG_Q, SEG_K, SEG_V, SEG_BZ = 3, 4, 5, 6

V7X_VMEM_LIMIT_BYTES = 56 * 1024 * 1024
LANES = 128
BF16_SUBLANES = 16
MXU_COLS = 256
NORM_ROWS = 16
SUMSQ_ROWS = 64
SLAB_PITCH = 24

BF16 = jnp.bfloat16
F32 = jnp.float32


def _pick(n, pref):
    t = min(n, pref)
    while n % t:
        t //= 2
    return t


def _params(sem, fuse_inputs=None):
    return pltpu.CompilerParams(dimension_semantics=sem, allow_input_fusion=fuse_inputs,
                                vmem_limit_bytes=V7X_VMEM_LIMIT_BYTES)


def _gelu(v):
    return 0.5 * v * (1.0 + lax.erf(v * (1.0 / math.sqrt(2.0))))


def _sigmoid(v):
    return 0.5 * (1.0 + jnp.tanh(0.5 * v))


def _silu(v):
    return v * _sigmoid(v)


def _scale_rows(x_ref, inv, g_ref, hn_ref):
    for r0 in range(0, x_ref.shape[0], NORM_ROWS):
        r = slice(r0, r0 + NORM_ROWS)
        hn_ref[r, :] = (x_ref[r, :] * inv[r] * g_ref[...]).astype(BF16)


def _normalise_rows(x_ref, g_ref, hn_ref):
    tm, d = x_ref.shape
    sums = []
    for r0 in range(0, tm, SUMSQ_ROWS):
        acc = jnp.zeros((SUMSQ_ROWS, LANES), F32)
        for k in range(d // LANES):
            xk = x_ref[r0:r0 + SUMSQ_ROWS, k * LANES:(k + 1) * LANES]
            acc = acc + xk * xk
        sums.append(jnp.sum(acc, axis=-1, keepdims=True))
    ms = jnp.concatenate(sums, axis=0) * (1.0 / d)
    _scale_rows(x_ref, lax.rsqrt(ms + EPS), g_ref, hn_ref)


def _in_proj_kernel(x_ref, g_ref, w_ref, gq_ref, gk_ref, *rest, tiles_per_seg, n_cast):
    cast_in, (o_ref, op_ref), cast_out = rest[:n_cast], rest[n_cast:n_cast + 2], rest[n_cast + 2:2 * n_cast + 2]
    hn_ref, slab_ref = rest[2 * n_cast + 2:]
    j = pl.program_id(1)

    @pl.when(j == 0)
    def _():
        _normalise_rows(x_ref, g_ref, hn_ref)

    seg = j // tiles_per_seg
    tm, tn = o_ref.shape
    n_sub = tn // MXU_COLS

    def sub_cols(c):
        return slice(c * MXU_COLS, (c + 1) * MXU_COLS)

    def product(c):
        return jnp.dot(hn_ref[...], w_ref[:, sub_cols(c)], preferred_element_type=F32)

    def tile(epilogue, residue_major=False):
        acc = product(0)
        for src_ref, dst_ref in zip(cast_in, cast_out):
            dst_ref[...] = src_ref[...].astype(BF16)
        for c in range(n_sub):
            nxt = product(c + 1) if c + 1 < n_sub else None
            slabs = MXU_COLS // LANES
            y = epilogue(acc)
            o_ref[:, sub_cols(c)] = y.astype(BF16)
            if residue_major:
                for k in range(slabs):
                    for a in range(tm // MAX_DIL):
                        slab_ref[c * slabs + k, a * SLAB_PITCH:a * SLAB_PITCH + MAX_DIL, :] = (
                            y[a * MAX_DIL:(a + 1) * MAX_DIL, k * LANES:(k + 1) * LANES])
                for r in range(MAX_DIL):
                    rows = [slab_ref[c * slabs + k, pl.ds(r, tm // MAX_DIL, stride=SLAB_PITCH), :]
                            for k in range(slabs)]
                    op_ref[0, r, :, sub_cols(c)] = jnp.concatenate(rows, axis=1).astype(BF16)
            acc = nxt

    def head_norm(g, scale):
        def epilogue(acc):
            out = []
            for c in range(MXU_COLS // HEAD_DIM):
                blk = acc[:, c * HEAD_DIM:(c + 1) * HEAD_DIM]
                ms = jnp.mean(blk * blk, axis=-1, keepdims=True)
                y = blk * lax.rsqrt(ms + EPS) * g
                out.append(y if scale is None else y * scale)
            return jnp.concatenate(out, axis=1)
        return epilogue

    @pl.when(seg <= 1)
    def _():
        tile(_gelu)

    @pl.when((seg == 2) | (seg == SEG_BZ))
    def _():
        tile(_silu)

    @pl.when(seg == SEG_Q)
    def _():
        tile(head_norm(gq_ref[...], HEAD_DIM ** -0.5), residue_major=True)

    @pl.when(seg == SEG_K)
    def _():
        tile(head_norm(gk_ref[...], None), residue_major=True)

    @pl.when(seg == SEG_V)
    def _():
        tile(lambda acc: acc, residue_major=True)


def _in_proj(x2, g_pre, w_in_bf, g_q, g_k, later_weights, *, batch, seq):
    m, d = x2.shape
    d_in = w_in_bf.shape[1]
    seg_w = d_in // N_SEG
    tm = _pick(seq, 512)
    tn = _pick(seg_w, 1024)
    assert tn % MXU_COLS == 0
    tps = seg_w // tn
    assert tm % (MAX_DIL * BF16_SUBLANES) == 0, "residue-major runs must cover whole bf16 sublane tiles"
    blocks_per_seq = seq // tm
    sub = seq // MAX_DIL
    runs = tm // MAX_DIL
    kern = functools.partial(_in_proj_kernel, tiles_per_seg=tps, n_cast=len(later_weights))
    n_j = d_in // tn
    steps = (m // tm) * n_j

    def perm_idx(i, j):
        jj = jnp.clip(j - SEG_Q * tps, 0, 3 * tps - 1)
        return (i // blocks_per_seq, 0, i % blocks_per_seq, jj)

    def cast_spec(w):
        rows = BF16_SUBLANES
        while w.shape[0] % rows or w.shape[0] // rows > steps:
            rows += BF16_SUBLANES
        last = w.shape[0] // rows - 1
        return pl.BlockSpec((rows, w.shape[1]), lambda i, j: (jnp.minimum(i * n_j + j, last), 0))

    out = pl.pallas_call(
        kern,
        grid=(m // tm, n_j),
        in_specs=[
            pl.BlockSpec((tm, d), lambda i, j: (i, 0)),
            pl.BlockSpec((1, d), lambda i, j: (0, 0)),
            pl.BlockSpec((d, tn), lambda i, j: (0, j)),
            pl.BlockSpec((1, HEAD_DIM), lambda i, j: (0, 0)),
            pl.BlockSpec((1, HEAD_DIM), lambda i, j: (0, 0)),
        ] + [cast_spec(w) for w in later_weights],
        out_specs=[
            pl.BlockSpec((tm, tn), lambda i, j: (i, j)),
            pl.BlockSpec((1, MAX_DIL, runs, tn), perm_idx),
        ] + [cast_spec(w) for w in later_weights],
        out_shape=[
            jax.ShapeDtypeStruct((m, d_in), BF16),
            jax.ShapeDtypeStruct((batch, MAX_DIL, sub, 3 * seg_w), BF16),
        ] + [jax.ShapeDtypeStruct(w.shape, BF16) for w in later_weights],
        scratch_shapes=[pltpu.VMEM((tm, d), BF16), pltpu.VMEM((tn // LANES, runs * SLAB_PITCH, LANES), F32)],
        compiler_params=_params(("arbitrary", "arbitrary"),
                                fuse_inputs=[k == 2 for k in range(5 + len(later_weights))]),
        name="in_proj",
    )(x2, g_pre, w_in_bf, g_q, g_k, *later_weights)
    return out[0], out[1], out[2:]


def _gmlp_rows(au_ref, av_ref, az_ref, ws_ref, bs_ref, lg_ref, lb_ref, go_ref, ya_ref):
    n_chunks = av_ref.shape[0] // CHUNK
    av = av_ref[...].astype(F32)
    mu = jnp.mean(av, axis=-1, keepdims=True)
    xc = av - mu
    var = jnp.mean(xc * xc, axis=-1, keepdims=True)
    avn = xc * lax.rsqrt(var + EPS) * lg_ref[...] + lb_ref[...]

    row = lax.broadcasted_iota(jnp.int32, (CHUNK, CHUNK), 0)
    col = lax.broadcasted_iota(jnp.int32, (CHUNK, CHUNK), 1)
    causal = col <= row
    n_groups = ws_ref.shape[0]
    for g in range(n_groups):
        sl = slice(g * HEAD_DIM, (g + 1) * HEAD_DIM)
        wm = jnp.where(causal, ws_ref[g], 0.0).astype(BF16)
        rhs = jnp.concatenate([avn[c * CHUNK:(c + 1) * CHUNK, sl] for c in range(n_chunks)], axis=1)
        z = jnp.dot(wm, rhs.astype(BF16), preferred_element_type=F32)
        for c in range(n_chunks):
            rows = slice(c * CHUNK, (c + 1) * CHUNK)
            zc = z[:, c * HEAD_DIM:(c + 1) * HEAD_DIM] + bs_ref[:, g:g + 1]
            ya_ref[rows, sl] = au_ref[rows, sl].astype(F32) * zc
    ya = ya_ref[...]
    ms = jnp.mean(ya * ya, axis=-1, keepdims=True)
    return (ya * lax.rsqrt(ms + EPS) * go_ref[...] * az_ref[...].astype(F32)).astype(BF16)


def _rel_bucket_np(dist):
    max_exact = NUM_BUCKETS // 2
    d = np.maximum(dist, 1).astype(np.float32)
    large = max_exact + (np.log(d / np.float32(max_exact)) / np.float32(math.log(MAX_DISTANCE / max_exact))
                         * np.float32(NUM_BUCKETS - max_exact)).astype(np.int32)
    large = np.minimum(large, NUM_BUCKETS - 1)
    return np.where(dist < max_exact, dist, large).astype(np.int32)


def _band_tables(dil, pos):
    i_q = pos[:, None]
    i_k = pos[None, :]
    in_cur = i_k <= i_q
    delta = np.where(in_cur, i_q - i_k, BLK + i_q - i_k)
    return in_cur.astype(np.int32), _rel_bucket_np(delta * dil)


class _TileIO:
    def __init__(self, prefix, run, rows_shape, g=0):
        self.prefix = prefix
        self.run = run
        self.rows_shape = rows_shape
        self.idx = prefix + (slice(g * run, (g + 1) * run),)

    def sub(self, g):
        return _TileIO(self.prefix, self.run, self.rows_shape, g)

    def load(self, ref, sl):
        return ref[self.idx + (sl,)].reshape(BLK, sl.stop - sl.start)

    def store(self, ref, sl, val):
        ref[self.idx + (sl,)] = val.reshape(self.rows_shape + (sl.stop - sl.start,))


def _attn_kernel(*refs, n_heads, n_sub, first, last, diag_bucket, io, io_state):
    it = iter(refs)
    cur_ref, bucket_ref, relb_ref = next(it), next(it), next(it)
    q_ref, kc_ref, vc_ref = next(it), next(it), next(it)
    if not first:
        o_in_ref, lse_in_ref = next(it), next(it)
    if last:
        bz_ref, go_ref = next(it), next(it)
        y_ref = next(it)
    else:
        o_out_ref, lse_out_ref = next(it), next(it)
    bias_ref, kp_ref, vp_ref = next(it), next(it), next(it)
    if last:
        yb_ref, st_ref = next(it), next(it)
    else:
        lse_ref = next(it)

    n = pl.program_id(2)
    first_step = (pl.program_id(0) == 0) & (pl.program_id(1) == 0) & (n == 0)
    prev_io = _TileIO((), BLK, (BLK,))

    @pl.when(first_step)
    def _():
        bk = bucket_ref[...]
        for h in range(n_heads):
            tab = jnp.zeros((BLK, BLK), F32)
            for b in range(NUM_BUCKETS):
                tab = jnp.where(bk == b, relb_ref[b, h], tab)
            bias_ref[h] = tab
        if not last:
            lse_ref[...] = jnp.zeros(lse_ref.shape, F32)

    def natural_rows(tile):
        st_ref[...] = tile
        per = BLK // MAX_DIL
        rows = [st_ref[pl.ds((MAX_DIL // 2) * per * (v % 2) + v // 2, 8, stride=per), :]
                for v in range(BLK // 8)]
        return jnp.concatenate(rows, axis=0)

    nt = (((1,), (1,)), ((), ()))

    def heads(g, has_prev):
        cur_io, st_io = io.sub(g), io_state.sub(g)
        kp_io, kp_src, vp_src = (io.sub(g - 1), kc_ref, vc_ref) if g else (prev_io, kp_ref, vp_ref)
        in_cur = cur_ref[...] != 0
        ones = jnp.ones((BLK, HEAD_DIM), BF16)
        if has_prev:
            eye = (lax.broadcasted_iota(jnp.int32, (BLK, BLK), 0)
                   == lax.broadcasted_iota(jnp.int32, (BLK, BLK), 1))
        if not first:
            lse_in = st_io.load(lse_in_ref, slice(0, BLK))
            if last:
                lse_in = natural_rows(lse_in)

        def logits(h):
            sl = slice(h * HEAD_DIM, (h + 1) * HEAD_DIM)
            keys = cur_io.load(kc_ref, sl)
            if has_prev:
                keys = jnp.concatenate([kp_io.load(kp_src, sl), keys], axis=0)
            return lax.dot_general(cur_io.load(q_ref, sl), keys, nt, preferred_element_type=F32)

        def softmax(h, s2):
            if has_prev:
                s_p, s_c = s2[:, :BLK], s2[:, BLK:]
                s = jnp.where(in_cur, s_c, s_p) + bias_ref[h]
                far = jnp.where(eye, s_p + relb_ref[diag_bucket, h], NEG_INF)
                mx = jnp.max(jnp.maximum(s, far), axis=-1, keepdims=True)
            else:
                s = jnp.where(in_cur, s2 + bias_ref[h], NEG_INF)
                mx = jnp.max(s, axis=-1, keepdims=True)
            lse_old = None
            if not first:
                lse_old = jnp.broadcast_to(lse_in[:, h:h + 1], (BLK, HEAD_DIM))
                mx = jnp.maximum(mx, lse_old)
            e = jnp.exp(s - mx)
            if has_prev:
                e_far = jnp.exp(far - mx)
                probs = jnp.concatenate([jnp.where(in_cur, e_far, e), jnp.where(in_cur, e, 0.0)], axis=1)
            else:
                probs = e
            return probs.astype(BF16), mx, lse_old

        def values(h, probs, mx, lse_old):
            sl = slice(h * HEAD_DIM, (h + 1) * HEAD_DIM)
            vals = jnp.concatenate([cur_io.load(vc_ref, sl), ones], axis=1)
            if has_prev:
                vals = jnp.concatenate([jnp.concatenate([kp_io.load(vp_src, sl), ones], axis=1), vals], axis=0)
            both = jnp.dot(probs, vals, preferred_element_type=F32)
            num, den = both[:, :HEAD_DIM], both[:, HEAD_DIM:]
            if first:
                total = den
                o = num / total
            else:
                o_old = st_io.load(o_in_ref, sl)
                if last:
                    o_old = natural_rows(o_old)
                w_old = jnp.exp(lse_old - mx)
                total = w_old + den
                o = (o_old * w_old + num) / total
            if last:
                yb_ref[g * BLK:(g + 1) * BLK, sl] = o
            else:
                st_io.store(o_out_ref, sl, o)
                lse = mx + jnp.log(total)
                lse_ref[g, :, h:h + 1] = lse[:, h:h + 1]

        ahead = {h: logits(h) for h in range(min(QK_LOOKAHEAD, n_heads))}
        soft = softmax(0, ahead.pop(0))
        for h in range(n_heads):
            if h + QK_LOOKAHEAD < n_heads:
                ahead[h + QK_LOOKAHEAD] = logits(h + QK_LOOKAHEAD)
            nxt = softmax(h + 1, ahead.pop(h + 1)) if h + 1 < n_heads else None
            values(h, *soft)
            soft = nxt

    @pl.when(n == 0)
    def _():
        heads(0, False)

    @pl.when(n > 0)
    def _():
        heads(0, True)

    for g in range(1, n_sub):
        heads(g, True)

    whole = slice(0, kp_ref.shape[1])
    kp_ref[...] = io.sub(n_sub - 1).load(kc_ref, whole)
    vp_ref[...] = io.sub(n_sub - 1).load(vc_ref, whole)
    if last:
        yb = yb_ref[...]
        ms = jnp.mean(yb * yb, axis=-1, keepdims=True)
        y_ref[...] = (yb * lax.rsqrt(ms + EPS) * go_ref[...] * bz_ref[...].astype(F32)).astype(BF16)
    else:
        for g in range(n_sub):
            io_state.sub(g).store(lse_out_ref, slice(0, BLK), lse_ref[g])


def _attn_pass(dil, qkv_rm, proj, rel_bias, state, g_out_b, *, batch, seq, w_b):
    first = state is None
    last = dil == 1
    n_heads = w_b // HEAD_DIM
    assert 2 * n_heads <= BLK
    sub = seq // MAX_DIL
    nb = seq // dil // BLK
    rep = MAX_DIL // dil
    runs = BLK // rep
    n_sub = min(ATTN_BLOCKS_PER_STEP, nb)
    assert nb % n_sub == 0
    col_q, col_k, col_v = 0, 1, 2

    if last:
        pos = np.arange(BLK)
        io = _TileIO((), BLK, (BLK,))
        io_state = _TileIO((0, slice(None)), BLK // MAX_DIL, (MAX_DIL, BLK // MAX_DIL))
        blk = (n_sub * BLK, w_b)
        src = proj
        cur = lambda seg: (lambda b, r, n: (b * (nb // n_sub) + n, seg))
        col_q, col_k, col_v = SEG_Q, SEG_K, SEG_V
        st_blk = lambda w: (1, MAX_DIL, n_sub * BLK // MAX_DIL, w)
        st_idx = lambda b, r, n: (b, 0, n, 0)
        view = lambda a: a
    else:
        rho = np.arange(BLK)
        pos = rep * (rho % runs) + rho // runs
        io = io_state = _TileIO((0, slice(None), 0), runs, (rep, runs))
        blk = (1, rep, 1, n_sub * runs, w_b)
        src = qkv_rm.reshape(batch, rep, dil, sub, 3 * w_b)
        cur = lambda col: (lambda b, r, n: (b, 0, r, n, col))
        st_blk = lambda w: (1, rep, 1, n_sub * runs, w)
        st_idx = lambda b, r, n: (b, 0, r, n, 0)
        view = lambda a: a.reshape(batch, rep, dil, sub, a.shape[-1])

    in_cur, bucket = _band_tables(dil, pos)
    diag_bucket = int(_rel_bucket_np(np.array([BLK * dil]))[0])
    const = lambda: pl.BlockSpec((BLK, BLK), lambda b, r, n: (0, 0))
    in_specs = [
        const(), const(), pl.BlockSpec(memory_space=pltpu.SMEM),
        pl.BlockSpec(blk, cur(col_q)), pl.BlockSpec(blk, cur(col_k)), pl.BlockSpec(blk, cur(col_v)),
    ]
    args = [jnp.asarray(in_cur), jnp.asarray(bucket), rel_bias, src, src, src]
    acc_spec = pl.BlockSpec(st_blk(w_b), st_idx)
    ml_spec = pl.BlockSpec(st_blk(BLK), st_idx)
    if not first:
        acc, ml = state
        in_specs += [acc_spec, ml_spec]
        args += [view(acc), view(ml)]
    scratch = [pltpu.VMEM((n_heads, BLK, BLK), F32),
               pltpu.VMEM((BLK, w_b), BF16), pltpu.VMEM((BLK, w_b), BF16)]
    if last:
        in_specs += [pl.BlockSpec(blk, cur(SEG_BZ)), pl.BlockSpec((1, w_b), lambda b, r, n: (0, 0))]
        args += [proj, g_out_b]
        out_specs = pl.BlockSpec(blk, cur(0))
        out_shape = jax.ShapeDtypeStruct((batch * seq, w_b), BF16)
        scratch += [pltpu.VMEM((n_sub * BLK, w_b), F32), pltpu.VMEM((BLK, LANES), F32)]
    else:
        out_specs = [acc_spec, ml_spec]
        scratch.append(pltpu.VMEM((n_sub, BLK, BLK), F32))
        out_shape = [jax.ShapeDtypeStruct((batch, rep, dil, sub, w_b), F32),
                     jax.ShapeDtypeStruct((batch, rep, dil, sub, BLK), F32)]
    kern = functools.partial(_attn_kernel, n_heads=n_heads, n_sub=n_sub, first=first, last=last,
                             diag_bucket=diag_bucket, io=io, io_state=io_state)
    out = pl.pallas_call(
        kern,
        grid=(batch, dil, nb // n_sub),
        in_specs=in_specs,
        out_specs=out_specs,
        out_shape=out_shape,
        scratch_shapes=scratch,
        compiler_params=_params(("arbitrary", "arbitrary", "arbitrary")),
        name=f"attn_d{dil}",
    )(*args)
    if last:
        return out
    acc, ml = out
    return (acc.reshape(batch, MAX_DIL, sub, w_b), ml.reshape(batch, MAX_DIL, sub, BLK))


def _gmlp_kernel(au_ref, av_ref, az_ref, ws_ref, bs_ref, lg_ref, lb_ref, go_ref, o_ref, tmp_ref):
    o_ref[...] = _gmlp_rows(au_ref, av_ref, az_ref, ws_ref, bs_ref, lg_ref, lb_ref, go_ref, tmp_ref)


def _out_proj_kernel(x_ref, ya0_ref, yb_ref, wa_ref, wb_ref, au_ref, av_ref, az_ref, ws_ref, bs_ref, lg_ref,
                     lb_ref, go_ref, h_ref, hsq_ref, ya_ref, tmp_ref, *, mix_steps):
    i, j = pl.program_id(0), pl.program_id(1)
    rows = au_ref.shape[0]
    n_sub = h_ref.shape[1] // MXU_COLS

    @pl.when((i == 0) & (j == 0))
    def _():
        ya_ref[0] = ya0_ref[...]

    @pl.when(j == 0)
    def _():
        hsq_ref[...] = jnp.zeros(hsq_ref.shape, F32)

    def mix():
        y = _gmlp_rows(au_ref, av_ref, az_ref, ws_ref, bs_ref, lg_ref, lb_ref, go_ref, tmp_ref)
        ya_ref[(i + 1) % 2, pl.ds(pl.multiple_of(j * rows, rows), rows), :] = y

    def product(c):
        cols = slice(c * MXU_COLS, (c + 1) * MXU_COLS)
        return (jnp.dot(ya_ref[i % 2], wa_ref[:, cols], preferred_element_type=F32)
                + jnp.dot(yb_ref[...], wb_ref[:, cols], preferred_element_type=F32))

    def project(with_mix):
        acc = product(0)
        for c in range(n_sub):
            nxt = product(c + 1) if c + 1 < n_sub else None
            cols = slice(c * MXU_COLS, (c + 1) * MXU_COLS)
            hv = x_ref[:, cols] + acc
            h_ref[:, cols] = hv
            sq = hv * hv
            hsq_ref[...] += sum(sq[:, k * LANES:(k + 1) * LANES] for k in range(MXU_COLS // LANES))
            acc = nxt
        if with_mix:
            mix()

    has_mix = (i + 1 < pl.num_programs(0)) & (j < mix_steps)

    @pl.when(has_mix)
    def _():
        project(True)

    @pl.when(jnp.logical_not(has_mix))
    def _():
        project(False)


def _out_proj(x2, proj, y_b, w_out_bf, w_s, b_s_t, ln_g, ln_b, g_out_a):
    m, d = x2.shape
    w_a = proj.shape[1] // N_SEG
    w_b = y_b.shape[1]
    assert w_a == w_b
    n_groups = w_s.shape[0]
    rows = GMLP_CHUNKS_PER_STEP * CHUNK
    tn = _pick(d, 512)
    assert tn % MXU_COLS == 0
    n_j = d // tn
    tm = min(_pick(m, 1024), rows * n_j)
    mix_steps = tm // rows
    assert m % tm == 0 and tm % rows == 0
    n_rb = m // tm
    mix_params = (w_s, b_s_t, ln_g, ln_b, g_out_a)

    def mix_specs(idx):
        const = lambda *_: (0, 0)
        return [
            pl.BlockSpec((rows, w_a), idx(0)), pl.BlockSpec((rows, w_a), idx(1)), pl.BlockSpec((rows, w_a), idx(2)),
            pl.BlockSpec((n_groups, CHUNK, CHUNK), lambda *_: (0, 0, 0)),
            pl.BlockSpec((CHUNK, n_groups), const),
            pl.BlockSpec((1, w_a), const), pl.BlockSpec((1, w_a), const), pl.BlockSpec((1, w_a), const),
        ]

    ya0 = pl.pallas_call(
        _gmlp_kernel,
        grid=(mix_steps,),
        in_specs=mix_specs(lambda seg: (lambda s: (s, seg))),
        out_specs=pl.BlockSpec((rows, w_a), lambda s: (s, 0)),
        out_shape=jax.ShapeDtypeStruct((tm, w_a), BF16),
        scratch_shapes=[pltpu.VMEM((rows, w_a), F32)],
        compiler_params=_params(("arbitrary",)),
        name="gmlp_first",
    )(proj, proj, proj, *mix_params)

    nxt = lambda seg: (lambda i, j: (jnp.minimum(i + 1, n_rb - 1) * mix_steps + jnp.minimum(j, mix_steps - 1), seg))
    return pl.pallas_call(
        functools.partial(_out_proj_kernel, mix_steps=mix_steps),
        grid=(n_rb, n_j),
        in_specs=[
            pl.BlockSpec((tm, tn), lambda i, j: (i, j)),
            pl.BlockSpec((tm, w_a), lambda i, j: (0, 0)),
            pl.BlockSpec((tm, w_b), lambda i, j: (i, 0)),
            pl.BlockSpec((w_a, tn), lambda i, j: (0, j)),
            pl.BlockSpec((w_b, tn), lambda i, j: (1, j)),
        ] + mix_specs(nxt),
        out_specs=[pl.BlockSpec((tm, tn), lambda i, j: (i, j)),
                   pl.BlockSpec((tm, LANES), lambda i, j: (i, 0))],
        out_shape=[jax.ShapeDtypeStruct((m, d), F32), jax.ShapeDtypeStruct((m, LANES), F32)],
        scratch_shapes=[pltpu.VMEM((2, tm, w_a), BF16), pltpu.VMEM((rows, w_a), F32)],
        compiler_params=_params(("arbitrary", "arbitrary")),
        name="out_proj",
    )(x2, ya0, y_b, w_out_bf, w_out_bf, proj, proj, proj, *mix_params)


def _ple_kernel(h_ref, hsq_ref, g_ref, wg_ref, p_ref, wu_ref, o_ref, hn_ref):
    j = pl.program_id(1)

    @pl.when(j == 0)
    def _():
        ms = jnp.sum(hsq_ref[...], axis=-1, keepdims=True) * (1.0 / h_ref.shape[1])
        _scale_rows(h_ref, lax.rsqrt(ms + EPS), g_ref, hn_ref)

    tn = o_ref.shape[1]
    n_sub = tn // MXU_COLS
    p_bf = p_ref[...].astype(BF16)

    def product(c):
        return jnp.dot(hn_ref[...], wg_ref[:, c * MXU_COLS:(c + 1) * MXU_COLS], preferred_element_type=F32)

    acc = product(0)
    for c in range(n_sub):
        nxt = product(c + 1) if c + 1 < n_sub else None
        cols = slice(c * MXU_COLS, (c + 1) * MXU_COLS)
        up = jnp.dot(p_bf, wu_ref[:, cols], preferred_element_type=F32)
        h_cols = pl.ds(pl.multiple_of(j * tn + c * MXU_COLS, MXU_COLS), MXU_COLS)
        o_ref[:, cols] = h_ref[:, h_cols] + _sigmoid(acc) * up
        acc = nxt


def _ple(h, hsq, g_ple, w_gate_bf, p2, w_up_bf):
    m, d = h.shape
    d_ple = p2.shape[1]
    tm = _pick(m, 512)
    tn = _pick(d, 1024)
    assert tn % MXU_COLS == 0
    return pl.pallas_call(
        _ple_kernel,
        grid=(m // tm, d // tn),
        in_specs=[
            pl.BlockSpec((tm, d), lambda i, j: (i, 0)),
            pl.BlockSpec((tm, LANES), lambda i, j: (i, 0)),
            pl.BlockSpec((1, d), lambda i, j: (0, 0)),
            pl.BlockSpec((d, tn), lambda i, j: (0, j)),
            pl.BlockSpec((tm, d_ple), lambda i, j: (i, 0)),
            pl.BlockSpec((d_ple, tn), lambda i, j: (0, j)),
        ],
        out_specs=pl.BlockSpec((tm, tn), lambda i, j: (i, j)),
        out_shape=jax.ShapeDtypeStruct((m, d), F32),
        scratch_shapes=[pltpu.VMEM((tm, d), BF16)],
        compiler_params=_params(("parallel", "arbitrary")),
        name="ple",
    )(h, hsq, g_ple, w_gate_bf, p2, w_up_bf)


def kernel(x, p, g_pre, w_in, w_s, b_s, ln_v_g, ln_v_b, g_q, g_k, rel_bias, g_out_a, g_out_b, w_out, g_ple, w_ple_gate, w_ple_up):
    batch, seq, d = x.shape
    depth = p.shape[0]
    w_a = ln_v_g.shape[-1]
    w_b = g_out_b.shape[-1]
    d_in = w_in.shape[-1]
    assert w_a == w_b and d_in == N_SEG * w_a, "segments of the combined projection must be equally wide"
    assert seq % (MAX_DIL * BLK) == 0 and all(win // dil == BLK for win, dil in DILATED)
    assert sorted(dil for _, dil in DILATED) == [1, 4, MAX_DIL]
    m = batch * seq
    x2 = x.reshape(m, d)
    for i in range(depth):
        proj, qkv_rm, (w_out_bf, w_gate_bf) = _in_proj(
            x2, g_pre[i][None], w_in[i].astype(BF16), g_q[i][None], g_k[i][None],
            [w_out[i], w_ple_gate[i]], batch=batch, seq=seq)
        state = None
        for dil in sorted((dil for _, dil in DILATED), reverse=True):
            state = _attn_pass(dil, qkv_rm, proj, rel_bias, state, g_out_b[i][None],
                               batch=batch, seq=seq, w_b=w_b)
        y_b = state
        h, hsq = _out_proj(x2, proj, y_b, w_out_bf, w_s[i], b_s[i].T, ln_v_g[i][None], ln_v_b[i][None],
                      g_out_a[i][None])
        x2 = _ple(h, hsq, g_ple[i][None], w_gate_bf, p[i].reshape(m, -1), w_ple_up[i].astype(BF16))
    return x2.reshape(batch, seq, d)
```
